```python
import math
import jax, jax.numpy as jnp
from jax import lax
import numpy as np

D_MODEL = 4096
BATCH = 4
SEQ = 4096
DEPTH = 1

CHUNK = 64
Q_BLOCK = 128
HEAD_DIM = 128
N_FOX_HEADS = D_MODEL // 256
N_DIFF_HEADS = D_MODEL // 512
FOX_WIDTH = N_FOX_HEADS * HEAD_DIM
DIFF_WIDTH = N_DIFF_HEADS * 2 * HEAD_DIM
D_FF = ((8 * D_MODEL // 3 + 255) // 256) * 256
ROPE_THETA = 500000.0
ROPE_DIM = HEAD_DIM // 4
N_MOD = 9
EPS = 1e-6
IN_SIZES = (FOX_WIDTH, FOX_WIDTH, FOX_WIDTH, N_FOX_HEADS,
            DIFF_WIDTH, DIFF_WIDTH, DIFF_WIDTH, D_MODEL, D_MODEL)
IN_COLS = 3 * FOX_WIDTH + N_FOX_HEADS + 3 * DIFF_WIDTH + 2 * D_MODEL

kernel_name = 'hybrid_fox_diff_macaron_adaln_block'


def rms_norm(x, g):
    xf = x.astype(jnp.float32)
    y = xf * lax.rsqrt(jnp.mean(xf * xf, axis=-1, keepdims=True) + EPS)
    return (y * g.astype(jnp.float32)).astype(x.dtype)


def modulate(h, shift, scale):
    return h * (1 + scale[:, None, :]) + shift[:, None, :]


def swiglu(h, w_in, w_out):
    gate, up = jnp.split(h @ w_in, 2, axis=-1)
    return (jax.nn.silu(gate) * up) @ w_out


def rope_tables(positions):
    inv_freq = ROPE_THETA ** (-jnp.arange(0, ROPE_DIM, 2, dtype=jnp.float32) / ROPE_DIM)
    ang = positions.astype(jnp.float32)[..., None] * inv_freq
    return jnp.cos(ang), jnp.sin(ang)


def apply_partial_rope(t, cos, sin):
    cos = cos[:, :, None, None, :]
    sin = sin[:, :, None, None, :]
    half = ROPE_DIM // 2
    r1 = t[..., :half].astype(jnp.float32)
    r2 = t[..., half:ROPE_DIM].astype(jnp.float32)
    rotated = jnp.concatenate([r1 * cos - r2 * sin, r2 * cos + r1 * sin], axis=-1).astype(t.dtype)
    return jnp.concatenate([rotated, t[..., ROPE_DIM:]], axis=-1)


def fox_attention(q, k, v, log_f_cum):
    S = q.shape[2]
    scale = q.shape[-1] ** -0.5
    outs = []
    for i in range(S // Q_BLOCK):
        q0, q1 = i * Q_BLOCK, (i + 1) * Q_BLOCK
        logits = jnp.einsum('bhqd,bhkd->bhqk', q[:, :, q0:q1], k[:, :, :q1],
                            preferred_element_type=jnp.float32) * scale
        logits = logits + log_f_cum[:, :, q0:q1, None] - log_f_cum[:, :, None, :q1]
        t_idx = jnp.arange(q0, q1)[:, None]
        s_idx = jnp.arange(q1)[None, :]
        logits = jnp.where(s_idx <= t_idx, logits, -jnp.inf)
        p = jax.nn.softmax(logits, axis=-1)
        outs.append(jnp.einsum('bhqk,bhkd->bhqd', p.astype(v.dtype), v[:, :, :q1]))
    return jnp.concatenate(outs, axis=2)


def diff_attention(q, k, v, lam):
    S = q.shape[3]
    scale = q.shape[-1] ** -0.5
    outs = []
    for i in range(S // Q_BLOCK):
        q0, q1 = i * Q_BLOCK, (i + 1) * Q_BLOCK
        logits = jnp.einsum('bhnqd,bhnkd->bhnqk', q[:, :, :, q0:q1], k[:, :, :, :q1],
                            preferred_element_type=jnp.float32) * scale
        t_chunk = jnp.arange(q0, q1)[:, None] // CHUNK
        s_chunk = jnp.arange(q1)[None, :] // CHUNK
        logits = jnp.where(s_chunk <= t_chunk, logits, -jnp.inf)
        p = jax.nn.softmax(logits, axis=-1)
        attn = p[:, :, 0] - lam * p[:, :, 1]
        outs.append(jnp.einsum('bhqk,bhkd->bhqd', attn.astype(v.dtype), v[:, :, :q1]))
    return jnp.concatenate(outs, axis=2)


def setup_inputs(seed: int = 0) -> dict:
    key = jax.random.key(seed)
    ks = jax.random.split(key, 24)
    f32 = jnp.float32

    def w(k, shape, fan_in):
        return jax.random.normal(k, shape, f32) * (fan_in ** -0.5)

    def gain(k, shape):
        return 1.0 + 0.1 * jax.random.normal(k, shape, f32)

    def small(k, shape, s=0.02):
        return s * jax.random.normal(k, shape, f32)

    x = jax.random.normal(ks[0], (BATCH, SEQ, D_MODEL), f32)
    c = jax.random.normal(ks[1], (BATCH, D_MODEL), f32)
    offsets = jax.random.randint(ks[2], (BATCH, 1), 0, 16, dtype=jnp.int32) * CHUNK
    positions = (offsets + jnp.arange(SEQ, dtype=jnp.int32)[None, :]).astype(jnp.int32)
    return {
        'x': x,
        'c': c,
        'positions': positions,
        'ada_w': w(ks[3], (DEPTH, D_MODEL, N_MOD * D_MODEL), D_MODEL),
        'ada_b': small(ks[4], (DEPTH, N_MOD * D_MODEL)),
        'norm_ffn1': gain(ks[5], (DEPTH, D_MODEL)),
        'ffn1_w_in': w(ks[6], (DEPTH, D_MODEL, 2 * D_FF), D_MODEL),
        'ffn1_w_out': w(ks[7], (DEPTH, D_FF, D_MODEL), D_FF),
        'norm_mix': gain(ks[8], (DEPTH, D_MODEL)),
        'w_in': w(ks[9], (DEPTH, D_MODEL, IN_COLS), D_MODEL),
        'b_forget': 2.0 + 0.5 * jax.random.normal(ks[10], (DEPTH, N_FOX_HEADS), f32),
        'b_gate': small(ks[11], (DEPTH, 2 * D_MODEL), 0.1),
        'diff_lambda': 0.1 * jax.random.normal(ks[12], (DEPTH, 4, HEAD_DIM), f32),
        'diff_subln': gain(ks[13], (DEPTH, 2 * HEAD_DIM)),
        'w_o_fox': w(ks[14], (DEPTH, FOX_WIDTH, D_MODEL), FOX_WIDTH),
        'w_o_diff': w(ks[15], (DEPTH, DIFF_WIDTH, D_MODEL), DIFF_WIDTH),
        'w_out': w(ks[16], (DEPTH, D_MODEL, D_MODEL), D_MODEL),
        'norm_ffn2': gain(ks[17], (DEPTH, D_MODEL)),
        'ffn2_w_in': w(ks[18], (DEPTH, D_MODEL, 2 * D_FF), D_MODEL),
        'ffn2_w_out': w(ks[19], (DEPTH, D_FF, D_MODEL), D_FF),
        'final_ada_w': w(ks[20], (D_MODEL, 2 * D_MODEL), D_MODEL),
        'final_ada_b': small(ks[21], (2 * D_MODEL,)),
        'norm_final': gain(ks[22], (D_MODEL,)),
    }


def reference(x, c, positions, ada_w, ada_b, norm_ffn1, ffn1_w_in, ffn1_w_out, norm_mix,
              w_in, b_forget, b_gate, diff_lambda, diff_subln, w_o_fox, w_o_diff, w_out,
              norm_ffn2, ffn2_w_in, ffn2_w_out, final_ada_w, final_ada_b, norm_final):
    B, S, D = x.shape
    sc = jax.nn.silu(c)
    cos, sin = rope_tables(positions)
    split_points = []
    acc = 0
    for size in IN_SIZES[:-1]:
        acc += size
        split_points.append(acc)

    for l in range(DEPTH):
        lambda_init = 0.8 - 0.6 * math.exp(-0.3 * l)
        mod = sc @ ada_w[l] + ada_b[l]
        sh1, sc1, g1, sh2, sc2, g2, sh3, sc3, g3 = jnp.split(mod, N_MOD, axis=-1)

        h = modulate(rms_norm(x, norm_ffn1[l]), sh1, sc1)
        x = x + 0.5 * g1[:, None, :] * swiglu(h, ffn1_w_in[l], ffn1_w_out[l])

        h = modulate(rms_norm(x, norm_mix[l]), sh2, sc2)
        proj = h @ w_in[l]
        fq, fk, fv, ff, dq, dk, dv, ga, gb = jnp.split(proj, split_points, axis=-1)

        qa = fq.reshape(B, S, N_FOX_HEADS, HEAD_DIM).transpose(0, 2, 1, 3)
        ka = fk.reshape(B, S, N_FOX_HEADS, HEAD_DIM).transpose(0, 2, 1, 3)
        va = fv.reshape(B, S, N_FOX_HEADS, HEAD_DIM).transpose(0, 2, 1, 3)
        log_f = jax.nn.log_sigmoid(ff.astype(jnp.float32) + b_forget[l].astype(jnp.float32))
        log_f_cum = jnp.cumsum(log_f, axis=1).transpose(0, 2, 1)
        ya = fox_attention(qa, ka, va, log_f_cum)
        ya = ya.transpose(0, 2, 1, 3).reshape(B, S, FOX_WIDTH)

        qb = apply_partial_rope(dq.reshape(B, S, N_DIFF_HEADS, 2, HEAD_DIM), cos, sin)
        kb = apply_partial_rope(dk.reshape(B, S, N_DIFF_HEADS, 2, HEAD_DIM), cos, sin)
        qb = qb.transpose(0, 2, 3, 1, 4)
        kb = kb.transpose(0, 2, 3, 1, 4)
        vb = dv.reshape(B, S, N_DIFF_HEADS, 2 * HEAD_DIM).transpose(0, 2, 1, 3)
        lam_p = diff_lambda[l].astype(jnp.float32)
        lam = (jnp.exp(jnp.sum(lam_p[0] * lam_p[1])) - jnp.exp(jnp.sum(lam_p[2] * lam_p[3]))
               + lambda_init)
        yb = diff_attention(qb, kb, vb, lam)
        yb = rms_norm(yb, diff_subln[l]) * (1 - lambda_init)
        yb = yb.transpose(0, 2, 1, 3).reshape(B, S, DIFF_WIDTH)

        gate_a, gate_b = jnp.split(jnp.concatenate([ga, gb], axis=-1) + b_gate[l], 2, axis=-1)
        merged = jax.nn.sigmoid(gate_a) * (ya @ w_o_fox[l]) + jax.nn.sigmoid(gate_b) * (yb @ w_o_diff[l])
        x = x + g2[:, None, :] * (merged @ w_out[l])

        h = modulate(rms_norm(x, norm_ffn2[l]), sh3, sc3)
        x = x + 0.5 * g3[:, None, :] * swiglu(h, ffn2_w_in[l], ffn2_w_out[l])

    shf, scf = jnp.split(sc @ final_ada_w + final_ada_b, 2, axis=-1)
    return modulate(rms_norm(x, norm_final), shf, scf)
```

```python
import functools
import math

import jax
import jax.numpy as jnp
from jax import lax
from jax.experimental import pallas as pl
from jax.experimental.pallas import tpu as pltpu

HEAD_DIM = 128
CHUNK = 64
ROPE_THETA = 500000.0
ROPE_DIM = HEAD_DIM // 4
ROPE_HALF = ROPE_DIM // 2
EPS = 1e-6
N_MOD = 9
LOG2E = 1.4426950408889634
MASKED = -1e30
LANE = 128
VMEM_LIMIT = 56 * 1024 * 1024

F32 = jnp.float32
BF16 = jnp.bfloat16


def _params(*sem):
    return pltpu.CompilerParams(dimension_semantics=sem, vmem_limit_bytes=VMEM_LIMIT)


def _tile(n, pref):
    t = min(n, pref)
    while n % t:
        t //= 2
    return t


def _round_up(n, m):
    return (n + m - 1) // m * m


def _ada_kernel(c_ref, w_ref, b_ref, o_ref):
    c = c_ref[...]
    sc = (c * jax.nn.sigmoid(c)).astype(BF16)
    o_ref[...] = jnp.dot(sc, w_ref[...].astype(BF16), preferred_element_type=F32) + b_ref[...]


def _ada(c_pad, w, b):
    d, n = w.shape
    tn = _tile(n, 512)
    return pl.pallas_call(
        _ada_kernel,
        grid=(n // tn,),
        in_specs=[
            pl.BlockSpec((c_pad.shape[0], d), lambda j: (0, 0)),
            pl.BlockSpec((d, tn), lambda j: (0, j)),
            pl.BlockSpec((1, tn), lambda j: (0, j)),
        ],
        out_specs=pl.BlockSpec((c_pad.shape[0], tn), lambda j: (0, j)),
        out_shape=jax.ShapeDtypeStruct((c_pad.shape[0], n), F32),
        compiler_params=_params("arbitrary"),
    )(c_pad, w, b.reshape(1, n))


def _norm_mod_kernel(x_ref, g_ref, sh_ref, sc_ref, o_ref):
    x = x_ref[...]
    ms = jnp.mean(x * x, axis=-1, keepdims=True)
    y = x * lax.rsqrt(ms + EPS) * g_ref[...]
    o_ref[...] = (y * (1.0 + sc_ref[0]) + sh_ref[0]).astype(o_ref.dtype)


def _norm_mod(x2, gain, mod3, n_mod, shift_j, scale_j, seq, out_dtype):
    m, d = x2.shape
    tm = _tile(seq, 256)
    per_b = seq // tm
    return pl.pallas_call(
        _norm_mod_kernel,
        grid=(m // tm,),
        in_specs=[
            pl.BlockSpec((tm, d), lambda i: (i, 0)),
            pl.BlockSpec((1, d), lambda i: (0, 0)),
            pl.BlockSpec((1, 1, d), lambda i: ((i // per_b) * n_mod + shift_j, 0, 0)),
            pl.BlockSpec((1, 1, d), lambda i: ((i // per_b) * n_mod + scale_j, 0, 0)),
        ],
        out_specs=pl.BlockSpec((tm, d), lambda i: (i, 0)),
        out_shape=jax.ShapeDtypeStruct((m, d), out_dtype),
        compiler_params=_params("arbitrary"),
    )(x2, gain.reshape(1, d), mod3, mod3)


def _ffn_in_kernel(h_ref, wg_ref, wu_ref, o_ref):
    h = h_ref[...]
    g = jnp.dot(h, wg_ref[...], preferred_element_type=F32)
    u = jnp.dot(h, wu_ref[...], preferred_element_type=F32)
    o_ref[...] = (g * jax.nn.sigmoid(g) * u).astype(o_ref.dtype)


def _ffn_in(h, wg, wu, bm, bn):
    m, d = h.shape
    fp = wg.shape[1]
    return pl.pallas_call(
        _ffn_in_kernel,
        grid=(m // bm, fp // bn),
        in_specs=[
            pl.BlockSpec((bm, d), lambda i, j: (i, 0)),
            pl.BlockSpec((d, bn), lambda i, j: (0, j)),
            pl.BlockSpec((d, bn), lambda i, j: (0, j)),
        ],
        out_specs=pl.BlockSpec((bm, bn), lambda i, j: (i, j)),
        out_shape=jax.ShapeDtypeStruct((m, fp), BF16),
        compiler_params=_params("arbitrary", "arbitrary"),
    )(h, wg, wu)


def _res_matmul_kernel(a_ref, w_ref, res_ref, gate_ref, o_ref, acc_ref, *, coef):
    k = pl.program_id(2)

    @pl.when(k == 0)
    def _():
        acc_ref[...] = jnp.zeros_like(acc_ref)

    acc_ref[...] += jnp.dot(a_ref[...], w_ref[...], preferred_element_type=F32)

    @pl.when(k == pl.num_programs(2) - 1)
    def _():
        o_ref[...] = res_ref[...] + (coef * gate_ref[0]) * acc_ref[...]


def _res_matmul(a, w, res, mod3, gate_j, coef, seq, bm, bn, bk):
    m, kdim = a.shape
    n = w.shape[1]
    per_b = seq // bm
    return pl.pallas_call(
        functools.partial(_res_matmul_kernel, coef=coef),
        grid=(m // bm, n // bn, kdim // bk),
        in_specs=[
            pl.BlockSpec((bm, bk), lambda i, j, k: (i, k)),
            pl.BlockSpec((bk, bn), lambda i, j, k: (k, j)),
            pl.BlockSpec((bm, bn), lambda i, j, k: (i, j)),
            pl.BlockSpec((1, 1, bn), lambda i, j, k: ((i // per_b) * N_MOD + gate_j, 0, j)),
        ],
        out_specs=pl.BlockSpec((bm, bn), lambda i, j, k: (i, j)),
        out_shape=jax.ShapeDtypeStruct((m, n), F32),
        scratch_shapes=[pltpu.VMEM((bm, bn), F32)],
        compiler_params=_params("arbitrary", "arbitrary", "arbitrary"),
    )(a, w, res, mod3)


def _proj_kernel(h_ref, w_ref, cos_ref, sin_ref, bg_ref, o_ref, acc_ref, *, nb_f, nb_d, bn, qscale):
    j = pl.program_id(1)
    acc_ref[...] = jnp.dot(h_ref[...], w_ref[...], preferred_element_type=F32)
    e_fq = nb_f
    e_fkv = 3 * nb_f
    e_dq = e_fkv + nb_d
    e_dk = e_dq + nb_d
    e_dv = e_dk + nb_d

    def rope_store(scale):
        cosf = cos_ref[...]
        sinf = sin_ref[...]
        lane = lax.broadcasted_iota(jnp.int32, cosf.shape, 1)
        for c in range(bn // HEAD_DIM):
            cols = slice(c * HEAD_DIM, (c + 1) * HEAD_DIM)
            t = acc_ref[:, cols]
            partner = jnp.where(lane < ROPE_HALF,
                                pltpu.roll(t, HEAD_DIM - ROPE_HALF, 1),
                                pltpu.roll(t, ROPE_HALF, 1))
            r = t * cosf + partner * sinf
            if scale is not None:
                r = r * scale
            o_ref[:, cols] = r.astype(o_ref.dtype)

    @pl.when(j < e_fq)
    def _():
        o_ref[...] = (acc_ref[...] * qscale).astype(o_ref.dtype)

    @pl.when(((j >= e_fq) & (j < e_fkv)) | ((j >= e_dk) & (j < e_dv)))
    def _():
        o_ref[...] = acc_ref[...].astype(o_ref.dtype)

    @pl.when((j >= e_fkv) & (j < e_dq))
    def _():
        rope_store(qscale)

    @pl.when((j >= e_dq) & (j < e_dk))
    def _():
        rope_store(None)

    @pl.when(j >= e_dv)
    def _():
        o_ref[...] = jax.nn.sigmoid(acc_ref[...] + bg_ref[...]).astype(o_ref.dtype)


def _proj(h, w, cosf, sinf, b_gate, fox_w, diff_w, bm, bn):
    m, d = h.shape
    n = w.shape[1]
    nb_f, nb_d = fox_w // bn, diff_w // bn
    e_dv = 3 * nb_f + 3 * nb_d
    qscale = HEAD_DIM ** -0.5 * LOG2E
    return pl.pallas_call(
        functools.partial(_proj_kernel, nb_f=nb_f, nb_d=nb_d, bn=bn, qscale=qscale),
        grid=(m // bm, n // bn),
        in_specs=[
            pl.BlockSpec((bm, d), lambda i, j: (i, 0)),
            pl.BlockSpec((d, bn), lambda i, j: (0, j)),
            pl.BlockSpec((bm, HEAD_DIM), lambda i, j: (i, 0)),
            pl.BlockSpec((bm, HEAD_DIM), lambda i, j: (i, 0)),
            pl.BlockSpec((1, bn), lambda i, j: (0, jnp.maximum(j - e_dv, 0))),
        ],
        out_specs=pl.BlockSpec((bm, bn), lambda i, j: (i, j)),
        out_shape=jax.ShapeDtypeStruct((m, n), BF16),
        scratch_shapes=[pltpu.VMEM((bm, bn), F32)],
        compiler_params=_params("arbitrary", "arbitrary"),
    )(h, w, cosf, sinf, b_gate.reshape(1, -1))


def _forget_kernel(h_ref, w_ref, b_ref, o_ref, carry_ref):
    @pl.when(pl.program_id(1) == 0)
    def _():
        carry_ref[...] = jnp.zeros_like(carry_ref)

    ff = jnp.dot(h_ref[...], w_ref[...], preferred_element_type=F32) + b_ref[...]
    logf = jnp.minimum(ff, 0.0) - jnp.log1p(jnp.exp(-jnp.abs(ff)))
    tm = logf.shape[0]
    row = lax.broadcasted_iota(jnp.int32, (tm, tm), 0)
    col = lax.broadcasted_iota(jnp.int32, (tm, tm), 1)
    tri = (row >= col).astype(F32)
    cum = jnp.dot(tri, logf, preferred_element_type=F32,
                  precision=lax.Precision.HIGHEST) + carry_ref[...]
    carry_ref[...] = cum[tm - 1:tm, :]
    o_ref[0] = (cum * LOG2E).T


def _forget(h, w_ff, b_ff, batch, seq):
    m, d = h.shape
    tm = _tile(seq, 256)
    per_b = seq // tm
    return pl.pallas_call(
        _forget_kernel,
        grid=(batch, per_b),
        in_specs=[
            pl.BlockSpec((tm, d), lambda b, i: (b * per_b + i, 0)),
            pl.BlockSpec((d, LANE), lambda b, i: (0, 0)),
            pl.BlockSpec((1, LANE), lambda b, i: (0, 0)),
        ],
        out_specs=pl.BlockSpec((1, LANE, tm), lambda b, i: (b, 0, i)),
        out_shape=jax.ShapeDtypeStruct((batch, LANE, seq), F32),
        scratch_shapes=[pltpu.VMEM((1, LANE), F32)],
        compiler_params=_params("arbitrary", "arbitrary"),
    )(h, w_ff, b_ff)


def _online_softmax_step(s, v, m, l, acc):
    m_new = jnp.maximum(m, jnp.max(s, axis=-1, keepdims=True))
    alpha = jnp.exp2(m - m_new)
    p = jnp.exp2(s - m_new)
    l_new = alpha * l + jnp.sum(p, axis=-1, keepdims=True)
    acc_new = alpha * acc + jnp.dot(p.astype(v.dtype), v, preferred_element_type=F32)
    return m_new, l_new, acc_new


def _qk(q, k):
    return lax.dot_general(q, k, (((1,), (1,)), ((), ())), preferred_element_type=F32)


def _fox_kernel(q_ref, k_ref, v_ref, f_ref, o_ref, *, tq):
    qi = pl.program_id(2)
    q = q_ref[...]

    def block(kj, carry, diagonal):
        ks = pl.multiple_of(kj * tq, tq)
        k = k_ref[pl.ds(ks, tq), :]
        v = v_ref[pl.ds(ks, tq), :]
        s = _qk(q, k) - f_ref[0, kj]
        if diagonal:
            row = lax.broadcasted_iota(jnp.int32, s.shape, 0)
            col = lax.broadcasted_iota(jnp.int32, s.shape, 1)
            s = jnp.where(col <= row, s, MASKED)
        return _online_softmax_step(s, v, *carry)

    init = (jnp.full((tq, 1), MASKED, F32), jnp.zeros((tq, 1), F32),
            jnp.zeros((tq, HEAD_DIM), F32))
    carry = lax.fori_loop(0, qi, lambda kj, c: block(kj, c, False), init)
    _, l, acc = block(qi, carry, True)
    o_ref[...] = (acc / l).astype(o_ref.dtype)


def _fox(proj, f_rows, batch, seq, n_heads, tq):
    m = proj.shape[0]
    nq = seq // tq
    return pl.pallas_call(
        functools.partial(_fox_kernel, tq=tq),
        grid=(batch, n_heads, nq),
        in_specs=[
            pl.BlockSpec((tq, HEAD_DIM), lambda b, h, i: (b * nq + i, h)),
            pl.BlockSpec((seq, HEAD_DIM), lambda b, h, i: (b, n_heads + h)),
            pl.BlockSpec((seq, HEAD_DIM), lambda b, h, i: (b, 2 * n_heads + h)),
            pl.BlockSpec((1, nq, 1, tq), lambda b, h, i: (b * n_heads + h, 0, 0, 0)),
        ],
        out_specs=pl.BlockSpec((tq, HEAD_DIM), lambda b, h, i: (b * nq + i, h)),
        out_shape=jax.ShapeDtypeStruct((m, n_heads * HEAD_DIM), BF16),
        compiler_params=_params("arbitrary", "arbitrary", "arbitrary"),
    )(proj, proj, proj, f_rows)


def _diff_kernel(q_ref, k_ref, v_ref, lam_ref, g_ref, o_ref, *, tq, lambda_init):
    qi = pl.program_id(2)
    q1 = q_ref[:, :HEAD_DIM]
    q2 = q_ref[:, HEAD_DIM:]

    def block(kj, carry, diagonal):
        ks = pl.multiple_of(kj * tq, tq)
        k1 = k_ref[pl.ds(ks, tq), :HEAD_DIM]
        k2 = k_ref[pl.ds(ks, tq), HEAD_DIM:]
        v = v_ref[pl.ds(ks, tq), :]
        s1 = _qk(q1, k1)
        s2 = _qk(q2, k2)
        if diagonal:
            row = lax.broadcasted_iota(jnp.int32, s1.shape, 0) // CHUNK
            col = lax.broadcasted_iota(jnp.int32, s1.shape, 1) // CHUNK
            visible = col <= row
            s1 = jnp.where(visible, s1, MASKED)
            s2 = jnp.where(visible, s2, MASKED)
        c1 = _online_softmax_step(s1, v, *carry[:3])
        c2 = _online_softmax_step(s2, v, *carry[3:])
        return c1 + c2

    one = (jnp.full((tq, 1), MASKED, F32), jnp.zeros((tq, 1), F32),
           jnp.zeros((tq, 2 * HEAD_DIM), F32))
    carry = lax.fori_loop(0, qi, lambda kj, c: block(kj, c, False), one + one)
    _, l1, a1, _, l2, a2 = block(qi, carry, True)

    lp = lam_ref[...]
    lam = (jnp.exp(jnp.sum(lp[0:1] * lp[1:2], axis=-1, keepdims=True))
           - jnp.exp(jnp.sum(lp[2:3] * lp[3:4], axis=-1, keepdims=True)) + lambda_init)
    y = a1 / l1 - lam * (a2 / l2)
    ms = jnp.mean(y * y, axis=-1, keepdims=True)
    y = y * lax.rsqrt(ms + EPS) * g_ref[...]
    o_ref[...] = (y * (1.0 - lambda_init)).astype(o_ref.dtype)


def _diff(proj, lam_p, subln, batch, seq, n_heads, col0, tq, lambda_init):
    m = proj.shape[0]
    nq = seq // tq
    hw = 2 * HEAD_DIM
    cb = col0 // hw
    return pl.pallas_call(
        functools.partial(_diff_kernel, tq=tq, lambda_init=lambda_init),
        grid=(batch, n_heads, nq),
        in_specs=[
            pl.BlockSpec((tq, hw), lambda b, h, i: (b * nq + i, cb + h)),
            pl.BlockSpec((seq, hw), lambda b, h, i: (b, cb + n_heads + h)),
            pl.BlockSpec((seq, hw), lambda b, h, i: (b, cb + 2 * n_heads + h)),
            pl.BlockSpec((4, HEAD_DIM), lambda b, h, i: (0, 0)),
            pl.BlockSpec((1, hw), lambda b, h, i: (0, 0)),
        ],
        out_specs=pl.BlockSpec((tq, hw), lambda b, h, i: (b * nq + i, h)),
        out_shape=jax.ShapeDtypeStruct((m, n_heads * hw), BF16),
        compiler_params=_params("arbitrary", "arbitrary", "arbitrary"),
    )(proj, proj, proj, lam_p, subln.reshape(1, hw))


def _merge_kernel(ya_ref, yb_ref, wa_ref, wb_ref, sa_ref, sb_ref, o_ref):
    pa = jnp.dot(ya_ref[...], wa_ref[...], preferred_element_type=F32)
    pb = jnp.dot(yb_ref[...], wb_ref[...], preferred_element_type=F32)
    o_ref[...] = (sa_ref[...].astype(F32) * pa + sb_ref[...].astype(F32) * pb).astype(o_ref.dtype)


def _merge(ya, yb, wa, wb, proj, gate_col0, bm, bn):
    m = ya.shape[0]
    d = wa.shape[1]
    ga0 = gate_col0 // bn
    gb0 = (gate_col0 + d) // bn
    return pl.pallas_call(
        _merge_kernel,
        grid=(m // bm, d // bn),
        in_specs=[
            pl.BlockSpec((bm, ya.shape[1]), lambda i, j: (i, 0)),
            pl.BlockSpec((bm, yb.shape[1]), lambda i, j: (i, 0)),
            pl.BlockSpec((wa.shape[0], bn), lambda i, j: (0, j)),
            pl.BlockSpec((wb.shape[0], bn), lambda i, j: (0, j)),
            pl.BlockSpec((bm, bn), lambda i, j: (i, ga0 + j)),
            pl.BlockSpec((bm, bn), lambda i, j: (i, gb0 + j)),
        ],
        out_specs=pl.BlockSpec((bm, bn), lambda i, j: (i, j)),
        out_shape=jax.ShapeDtypeStruct((m, d), BF16),
        compiler_params=_params("arbitrary", "arbitrary"),
    )(ya, yb, wa, wb, proj, proj)


def _ffn(x2, gain, w_in, w_out, mod3, j0, seq):
    m, d = x2.shape
    f = w_out.shape[0]
    bm = _tile(seq, 1024)
    fp = _round_up(f, 1024) if f >= 1024 else f
    bn_in = _tile(fp, 512)
    wg = jnp.pad(w_in[:, :f].astype(BF16), ((0, 0), (0, fp - f)))
    wu = jnp.pad(w_in[:, f:].astype(BF16), ((0, 0), (0, fp - f)))
    wo = jnp.pad(w_out.astype(BF16), ((0, fp - f), (0, 0)))
    h = _norm_mod(x2, gain, mod3, N_MOD, j0, j0 + 1, seq, BF16)
    a = _ffn_in(h, wg, wu, bm, bn_in)
    bk = fp // 4 if (fp // 4) % LANE == 0 else fp
    return _res_matmul(a, wo, x2, mod3, j0 + 2, 0.5, seq, bm, _tile(d, 1024), bk)


def kernel(x, c, positions, ada_w, ada_b, norm_ffn1, ffn1_w_in, ffn1_w_out, norm_mix, w_in, b_forget, b_gate, diff_lambda, diff_subln, w_o_fox, w_o_diff, w_out, norm_ffn2, ffn2_w_in, ffn2_w_out, final_ada_w, final_ada_b, norm_final):
    batch, seq, d = x.shape
    depth = ada_w.shape[0]
    m = batch * seq
    n_fox = b_forget.shape[1]
    fox_w = n_fox * HEAD_DIM
    diff_w = w_o_diff.shape[1]
    n_diff = diff_w // (2 * HEAD_DIM)
    assert w_in.shape[2] == 3 * fox_w + n_fox + 3 * diff_w + 2 * d
    assert n_fox <= LANE and seq % CHUNK == 0

    c_pad = jnp.pad(c, ((0, 8 - batch % 8 if batch % 8 else 0), (0, 0)))
    x2 = x.reshape(m, d)

    inv_freq = ROPE_THETA ** (-jnp.arange(0, ROPE_DIM, 2, dtype=F32) / ROPE_DIM)
    ang = positions.astype(F32).reshape(m, 1) * inv_freq
    cos, sin = jnp.cos(ang), jnp.sin(ang)
    cosf = jnp.concatenate([cos, cos, jnp.ones((m, HEAD_DIM - ROPE_DIM), F32)], axis=-1)
    sinf = jnp.concatenate([-sin, sin, jnp.zeros((m, HEAD_DIM - ROPE_DIM), F32)], axis=-1)

    bm = _tile(seq, 1024)
    tq = _tile(seq, 256)
    for l in range(depth):
        lambda_init = 0.8 - 0.6 * math.exp(-0.3 * l)
        mod = _ada(c_pad, ada_w[l], ada_b[l])[:batch]
        mod3 = mod.reshape(batch * N_MOD, 1, d)

        x2 = _ffn(x2, norm_ffn1[l], ffn1_w_in[l], ffn1_w_out[l], mod3, 0, seq)

        h = _norm_mod(x2, norm_mix[l], mod3, N_MOD, 3, 4, seq, BF16)
        wl = w_in[l]
        ff0 = 3 * fox_w
        w_proj = jnp.concatenate([wl[:, :ff0], wl[:, ff0 + n_fox:]], axis=1).astype(BF16)
        w_ff = jnp.pad(wl[:, ff0:ff0 + n_fox].astype(BF16), ((0, 0), (0, LANE - n_fox)))
        b_ff = jnp.pad(b_forget[l].astype(F32), (0, LANE - n_fox)).reshape(1, LANE)
        proj = _proj(h, w_proj, cosf, sinf, b_gate[l], fox_w, diff_w, bm, _tile(math.gcd(fox_w, diff_w), 1024))
        f_cum = _forget(h, w_ff, b_ff, batch, seq)
        f_rows = f_cum[:, :n_fox, :].reshape(batch * n_fox, seq // tq, 1, tq)

        ya = _fox(proj, f_rows, batch, seq, n_fox, tq)
        yb = _diff(proj, diff_lambda[l].astype(F32), diff_subln[l], batch, seq, n_diff,
                   3 * fox_w, tq, lambda_init)
        merged = _merge(ya, yb, w_o_fox[l].astype(BF16), w_o_diff[l].astype(BF16), proj,
                        3 * fox_w + 3 * diff_w, bm, _tile(d, 512))
        x2 = _res_matmul(merged, w_out[l].astype(BF16), x2, mod3, 5, 1.0, seq, bm,
                         _tile(d, 512), d)

        x2 = _ffn(x2, norm_ffn2[l], ffn2_w_in[l], ffn2_w_out[l], mod3, 6, seq)

    fmod = _ada(c_pad, final_ada_w, final_ada_b)[:batch]
    out = _norm_mod(x2, norm_final, fmod.reshape(batch * 2, 1, d), 2, 0, 1, seq, F32)
    return out.reshape(batch, seq, d)
```

```python
import functools
import math

import numpy as np
import jax
import jax.numpy as jnp
from jax import lax
from jax.experimental import pallas as pl
from jax.experimental.pallas import tpu as pltpu

HEAD_DIM = 128
CHUNK = 64
ROPE_THETA = 500000.0
ROPE_DIM = HEAD_DIM // 4
ROPE_HALF = ROPE_DIM // 2
EPS = 1e-6
N_MOD = 9
LOG2E = 1.4426950408889634
MASKED = -1e30
LANE = 128
VMEM_LIMIT = 56 * 1024 * 1024
N_SPLIT = 3
FOX_HEADS_PER_STEP = 4
DIFF_HEADS_PER_STEP = 2

F32 = jnp.float32
BF16 = jnp.bfloat16


def _params(*sem):
    return pltpu.CompilerParams(dimension_semantics=sem, vmem_limit_bytes=VMEM_LIMIT)


def _tile(n, pref):
    t = min(n, pref)
    while n % t:
        t //= 2
    return t


def _round_up(n, m):
    return (n + m - 1) // m * m


def _nt_dot(a, b):
    return lax.dot_general(a, b, (((1,), (1,)), ((), ())), preferred_element_type=F32)


def _ada_kernel(c_ref, w_ref, b_ref, o_ref):
    c = c_ref[...]
    sc = (c * jax.nn.sigmoid(c)).astype(BF16)
    o_ref[...] = jnp.dot(sc, w_ref[...].astype(BF16), preferred_element_type=F32) + b_ref[...]


def _ada(c_pad, w, b):
    d, n = w.shape
    tn = _tile(n, 512)
    return pl.pallas_call(
        _ada_kernel,
        grid=(n // tn,),
        in_specs=[
            pl.BlockSpec((c_pad.shape[0], d), lambda j: (0, 0)),
            pl.BlockSpec((d, tn), lambda j: (0, j)),
            pl.BlockSpec((1, tn), lambda j: (0, j)),
        ],
        out_specs=pl.BlockSpec((c_pad.shape[0], tn), lambda j: (0, j)),
        out_shape=jax.ShapeDtypeStruct((c_pad.shape[0], n), F32),
        compiler_params=_params("arbitrary"),
        name="ada_matvec",
    )(c_pad, w, b.reshape(1, n))


def _norm_mod_kernel(x_ref, g_ref, sh_ref, sc_ref, o_ref):
    x = x_ref[...]
    ms = jnp.mean(x * x, axis=-1, keepdims=True)
    y = x * lax.rsqrt(ms + EPS) * g_ref[...]
    o_ref[...] = (y * (1.0 + sc_ref[0]) + sh_ref[0]).astype(o_ref.dtype)


def _norm_mod(x2, gain, mod3, n_mod, shift_j, scale_j, seq, out_dtype):
    m, d = x2.shape
    tm = _tile(seq, 256)
    per_b = seq // tm
    return pl.pallas_call(
        _norm_mod_kernel,
        grid=(m // tm,),
        in_specs=[
            pl.BlockSpec((tm, d), lambda i: (i, 0)),
            pl.BlockSpec((1, d), lambda i: (0, 0)),
            pl.BlockSpec((1, 1, d), lambda i: ((i // per_b) * n_mod + shift_j, 0, 0)),
            pl.BlockSpec((1, 1, d), lambda i: ((i // per_b) * n_mod + scale_j, 0, 0)),
        ],
        out_specs=pl.BlockSpec((tm, d), lambda i: (i, 0)),
        out_shape=jax.ShapeDtypeStruct((m, d), out_dtype),
        compiler_params=_params("arbitrary"),
        name="norm_mod",
    )(x2, gain.reshape(1, d), mod3, mod3)


def _ffn_in_kernel(h_ref, wg_ref, wu_ref, o_ref):
    h = h_ref[...]
    g = jnp.dot(h, wg_ref[...], preferred_element_type=F32)
    u = jnp.dot(h, wu_ref[...], preferred_element_type=F32)
    o_ref[...] = (g * jax.nn.sigmoid(g) * u).astype(o_ref.dtype)


def _ffn_in(h, wg, wu, bm, bn):
    m, d = h.shape
    fp = wg.shape[1]
    return pl.pallas_call(
        _ffn_in_kernel,
        grid=(m // bm, fp // bn),
        in_specs=[
            pl.BlockSpec((bm, d), lambda i, j: (i, 0)),
            pl.BlockSpec((d, bn), lambda i, j: (0, j)),
            pl.BlockSpec((d, bn), lambda i, j: (0, j)),
        ],
        out_specs=pl.BlockSpec((bm, bn), lambda i, j: (i, j)),
        out_shape=jax.ShapeDtypeStruct((m, fp), BF16),
        compiler_params=_params("arbitrary", "arbitrary"),
        name="ffn_in",
    )(h, wg, wu)


def _res_matmul_kernel(a_ref, w_ref, res_ref, gate_ref, o_ref, acc_ref, *, coef):
    k = pl.program_id(2)

    @pl.when(k == 0)
    def _():
        acc_ref[...] = jnp.zeros_like(acc_ref)

    acc_ref[...] += jnp.dot(a_ref[...], w_ref[...], preferred_element_type=F32)

    @pl.when(k == pl.num_programs(2) - 1)
    def _():
        o_ref[...] = res_ref[...] + (coef * gate_ref[0]) * acc_ref[...]


def _res_matmul(a, w, res, mod3, gate_j, coef, seq, bm, bn, bk, name):
    m, kdim = a.shape
    n = w.shape[1]
    per_b = seq // bm
    return pl.pallas_call(
        functools.partial(_res_matmul_kernel, coef=coef),
        grid=(m // bm, n // bn, kdim // bk),
        in_specs=[
            pl.BlockSpec((bm, bk), lambda i, j, k: (i, k)),
            pl.BlockSpec((bk, bn), lambda i, j, k: (k, j)),
            pl.BlockSpec((bm, bn), lambda i, j, k: (i, j)),
            pl.BlockSpec((1, 1, bn), lambda i, j, k: ((i // per_b) * N_MOD + gate_j, 0, j)),
        ],
        out_specs=pl.BlockSpec((bm, bn), lambda i, j, k: (i, j)),
        out_shape=jax.ShapeDtypeStruct((m, n), F32),
        scratch_shapes=[pltpu.VMEM((bm, bn), F32)],
        compiler_params=_params("arbitrary", "arbitrary", "arbitrary"),
        name=name,
    )(a, w, res, mod3)


def _proj_kernel(h_ref, w_ref, cos_ref, sin_ref, bg_ref, o_ref, acc_ref, *, nb_f, nb_d, bn, qscale):
    j = pl.program_id(1)
    acc_ref[...] = jnp.dot(h_ref[...], w_ref[...], preferred_element_type=F32)
    e_fq = nb_f
    e_fk = 2 * nb_f
    e_dq = e_fk + nb_d
    e_dk = e_dq + nb_d

    def rope_store(scale):
        cosf = cos_ref[...]
        sinf = sin_ref[...]
        lane = lax.broadcasted_iota(jnp.int32, cosf.shape, 1)
        for c in range(bn // HEAD_DIM):
            cols = slice(c * HEAD_DIM, (c + 1) * HEAD_DIM)
            t = acc_ref[:, cols]
            partner = jnp.where(lane < ROPE_HALF,
                                pltpu.roll(t, HEAD_DIM - ROPE_HALF, 1),
                                pltpu.roll(t, ROPE_HALF, 1))
            r = t * cosf + partner * sinf
            if scale is not None:
                r = r * scale
            o_ref[:, cols] = r.astype(o_ref.dtype)

    @pl.when(j < e_fq)
    def _():
        o_ref[...] = (acc_ref[...] * qscale).astype(o_ref.dtype)

    @pl.when((j >= e_fq) & (j < e_fk))
    def _():
        o_ref[...] = acc_ref[...].astype(o_ref.dtype)

    @pl.when((j >= e_fk) & (j < e_dq))
    def _():
        rope_store(qscale)

    @pl.when((j >= e_dq) & (j < e_dk))
    def _():
        rope_store(None)

    @pl.when(j >= e_dk)
    def _():
        o_ref[...] = jax.nn.sigmoid(acc_ref[...] + bg_ref[...]).astype(o_ref.dtype)


def _proj(h, w, cosf, sinf, b_gate, fox_w, diff_w, bm, bn):
    m, d = h.shape
    n = w.shape[1]
    nb_f, nb_d = fox_w // bn, diff_w // bn
    e_dk = 2 * nb_f + 2 * nb_d
    qscale = HEAD_DIM ** -0.5 * LOG2E
    return pl.pallas_call(
        functools.partial(_proj_kernel, nb_f=nb_f, nb_d=nb_d, bn=bn, qscale=qscale),
        grid=(m // bm, n // bn),
        in_specs=[
            pl.BlockSpec((bm, d), lambda i, j: (i, 0)),
            pl.BlockSpec((d, bn), lambda i, j: (0, j)),
            pl.BlockSpec((bm, HEAD_DIM), lambda i, j: (i, 0)),
            pl.BlockSpec((bm, HEAD_DIM), lambda i, j: (i, 0)),
            pl.BlockSpec((1, bn), lambda i, j: (0, jnp.maximum(j - e_dk, 0))),
        ],
        out_specs=pl.BlockSpec((bm, bn), lambda i, j: (i, j)),
        out_shape=jax.ShapeDtypeStruct((m, n), BF16),
        scratch_shapes=[pltpu.VMEM((bm, bn), F32)],
        compiler_params=_params("arbitrary", "arbitrary"),
        name="proj_qkg",
    )(h, w, cosf, sinf, b_gate.reshape(1, -1))


def _vt_proj_kernel(wt_ref, h_ref, o_ref, *, tk):
    r = _nt_dot(wt_ref[...], h_ref[...])
    for t in range(o_ref.shape[0]):
        o_ref[t] = r[:, t * tk:(t + 1) * tk].astype(o_ref.dtype)


def _vt_proj(wt, h, bm, bn, tk):
    n, d = wt.shape
    m = h.shape[0]
    return pl.pallas_call(
        functools.partial(_vt_proj_kernel, tk=tk),
        grid=(m // bm, n // bn),
        in_specs=[
            pl.BlockSpec((bn, d), lambda i, j: (j, 0)),
            pl.BlockSpec((bm, d), lambda i, j: (i, 0)),
        ],
        out_specs=pl.BlockSpec((bm // tk, bn, tk), lambda i, j: (i, j, 0)),
        out_shape=jax.ShapeDtypeStruct((m // tk, n, tk), BF16),
        compiler_params=_params("arbitrary", "arbitrary"),
        name="proj_vt",
    )(wt, h)


def _forget_kernel(h_ref, w_ref, b_ref, e_ref, o_ref, carry_ref):
    @pl.when(pl.program_id(1) == 0)
    def _():
        carry_ref[...] = jnp.zeros_like(carry_ref)

    ff = jnp.dot(h_ref[...], w_ref[...], preferred_element_type=F32) + b_ref[...]
    logf = jnp.minimum(ff, 0.0) - jnp.log1p(jnp.exp(-jnp.abs(ff)))
    tm = logf.shape[0]
    row = lax.broadcasted_iota(jnp.int32, (tm, tm), 0)
    col = lax.broadcasted_iota(jnp.int32, (tm, tm), 1)
    tri = (row >= col).astype(F32)
    cum = jnp.dot(tri, logf, preferred_element_type=F32,
                  precision=lax.Precision.HIGHEST) + carry_ref[...]
    carry_ref[...] = cum[tm - 1:tm, :]

    rest = cum * (-LOG2E)
    pieces = []
    for _ in range(N_SPLIT):
        p = rest.astype(BF16)
        pieces.append(p)
        rest = rest - p.astype(F32)
    spread = jnp.dot(jnp.concatenate(pieces, axis=1), e_ref[...], preferred_element_type=F32)
    o_ref[...] = spread.astype(o_ref.dtype)


def _forget(h, w_ff, b_ff, n_heads, batch, seq):
    m, d = h.shape
    tm = _tile(seq, 256)
    per_b = seq // tm
    route = np.zeros((N_SPLIT * LANE, n_heads * HEAD_DIM), np.float32)
    for p in range(N_SPLIT):
        for hh in range(n_heads):
            route[p * LANE + hh, hh * HEAD_DIM + p] = 1.0
    return pl.pallas_call(
        _forget_kernel,
        grid=(batch, per_b),
        in_specs=[
            pl.BlockSpec((tm, d), lambda b, i: (b * per_b + i, 0)),
            pl.BlockSpec((d, LANE), lambda b, i: (0, 0)),
            pl.BlockSpec((1, LANE), lambda b, i: (0, 0)),
            pl.BlockSpec(route.shape, lambda b, i: (0, 0)),
        ],
        out_specs=pl.BlockSpec((tm, n_heads * HEAD_DIM), lambda b, i: (b * per_b + i, 0)),
        out_shape=jax.ShapeDtypeStruct((m, n_heads * HEAD_DIM), BF16),
        scratch_shapes=[pltpu.VMEM((1, LANE), F32)],
        compiler_params=_params("arbitrary", "arbitrary"),
        name="forget_cumsum",
    )(h, w_ff, b_ff, jnp.asarray(route, BF16))


def _online_softmax_step(st, vt, m, l, acc):
    m_new = jnp.maximum(m, jnp.max(st, axis=0, keepdims=True))
    alpha = jnp.exp2(m - m_new)
    p = jnp.exp2(st - m_new)
    l_new = alpha * l + jnp.sum(p, axis=0, keepdims=True)
    acc_new = alpha * acc + jnp.dot(vt, p.astype(vt.dtype), preferred_element_type=F32)
    return m_new, l_new, acc_new


def _softmax_init(tq, dv):
    return (jnp.full((1, tq), MASKED, F32), jnp.zeros((1, tq), F32), jnp.zeros((dv, tq), F32))


def _fox_kernel(q_ref, k_ref, kx_ref, vt_ref, o_ref, *, tq, nh):
    qi = pl.program_id(2)
    lane = lax.broadcasted_iota(jnp.int32, (tq, HEAD_DIM), 1)
    ones = (lane < N_SPLIT).astype(q_ref.dtype)

    def head(a):
        return slice(a * HEAD_DIM, (a + 1) * HEAD_DIM)

    def block(kj, carry, diagonal):
        ks = pl.multiple_of(kj * tq, tq)
        sts = []
        for a in range(nh):
            q_aug = jnp.concatenate([q_ref[:, head(a)], ones], axis=1)
            k_aug = jnp.concatenate([k_ref[pl.ds(ks, tq), head(a)],
                                     kx_ref[pl.ds(ks, tq), head(a)]], axis=1)
            st = _nt_dot(k_aug, q_aug)
            if diagonal:
                key = lax.broadcasted_iota(jnp.int32, st.shape, 0)
                qry = lax.broadcasted_iota(jnp.int32, st.shape, 1)
                st = jnp.where(key <= qry, st, MASKED)
            sts.append(st)
        out = ()
        for a in range(nh):
            out += _online_softmax_step(sts[a], vt_ref[kj, head(a), :], *carry[3 * a:3 * a + 3])
        return out

    carry = lax.fori_loop(0, qi, lambda kj, c: block(kj, c, False),
                          _softmax_init(tq, HEAD_DIM) * nh)
    carry = block(qi, carry, True)
    for a in range(nh):
        _, l, acc = carry[3 * a:3 * a + 3]
        o_ref[:, head(a)] = (acc / l).T.astype(o_ref.dtype)


def _fox(qk, kx, vt, batch, seq, n_heads, tq, nh):
    m = qk.shape[0]
    nq = seq // tq
    gw = nh * HEAD_DIM
    ng = n_heads // nh
    return pl.pallas_call(
        functools.partial(_fox_kernel, tq=tq, nh=nh),
        grid=(batch, ng, nq),
        in_specs=[
            pl.BlockSpec((tq, gw), lambda b, g, i: (b * nq + i, g)),
            pl.BlockSpec((seq, gw), lambda b, g, i: (b, ng + g)),
            pl.BlockSpec((seq, gw), lambda b, g, i: (b, g)),
            pl.BlockSpec((nq, gw, tq), lambda b, g, i: (b, g, 0)),
        ],
        out_specs=pl.BlockSpec((tq, gw), lambda b, g, i: (b * nq + i, g)),
        out_shape=jax.ShapeDtypeStruct((m, n_heads * HEAD_DIM), BF16),
        compiler_params=_params("arbitrary", "arbitrary", "arbitrary"),
        name="fox_attention",
    )(qk, qk, kx, vt)


def _diff_kernel(q_ref, k_ref, vt_ref, lam_ref, g_ref, o_ref, *, tq, nh, lambda_init):
    qi = pl.program_id(2)
    n_maps = 2 * nh

    def cols(c):
        return slice(c * HEAD_DIM, (c + 1) * HEAD_DIM)

    def block(kj, carry, diagonal):
        ks = pl.multiple_of(kj * tq, tq)
        sts = []
        for c in range(n_maps):
            st = _nt_dot(k_ref[pl.ds(ks, tq), cols(c)], q_ref[:, cols(c)])
            if diagonal:
                key = lax.broadcasted_iota(jnp.int32, st.shape, 0) // CHUNK
                qry = lax.broadcasted_iota(jnp.int32, st.shape, 1) // CHUNK
                st = jnp.where(key <= qry, st, MASKED)
            sts.append(st)
        out = ()
        for c in range(n_maps):
            vt = vt_ref[kj, (c // 2) * 2 * HEAD_DIM:(c // 2 + 1) * 2 * HEAD_DIM, :]
            out += _online_softmax_step(sts[c], vt, *carry[3 * c:3 * c + 3])
        return out

    carry = lax.fori_loop(0, qi, lambda kj, c: block(kj, c, False),
                          _softmax_init(tq, 2 * HEAD_DIM) * n_maps)
    carry = block(qi, carry, True)

    lp = lam_ref[...]
    lam = (jnp.exp(jnp.sum(lp[0:1] * lp[1:2], axis=-1, keepdims=True))
           - jnp.exp(jnp.sum(lp[2:3] * lp[3:4], axis=-1, keepdims=True)) + lambda_init)
    for a in range(nh):
        _, l1, a1, _, l2, a2 = carry[6 * a:6 * a + 6]
        y = a1 / l1 - lam * (a2 / l2)
        ms = jnp.mean(y * y, axis=0, keepdims=True)
        y = (y * lax.rsqrt(ms + EPS)).T * g_ref[...]
        o_ref[:, a * 2 * HEAD_DIM:(a + 1) * 2 * HEAD_DIM] = (y * (1.0 - lambda_init)).astype(o_ref.dtype)


def _diff(qk, vt, lam_p, subln, batch, seq, n_heads, q_col0, vt_row0, tq, nh, lambda_init):
    m = qk.shape[0]
    nq = seq // tq
    hw = 2 * HEAD_DIM
    gw = nh * hw
    ng = n_heads // nh
    cb = q_col0 // gw
    vb = vt_row0 // gw
    return pl.pallas_call(
        functools.partial(_diff_kernel, tq=tq, nh=nh, lambda_init=lambda_init),
        grid=(batch, ng, nq),
        in_specs=[
            pl.BlockSpec((tq, gw), lambda b, g, i: (b * nq + i, cb + g)),
            pl.BlockSpec((seq, gw), lambda b, g, i: (b, cb + ng + g)),
            pl.BlockSpec((nq, gw, tq), lambda b, g, i: (b, vb + g, 0)),
            pl.BlockSpec((4, HEAD_DIM), lambda b, g, i: (0, 0)),
            pl.BlockSpec((1, hw), lambda b, g, i: (0, 0)),
        ],
        out_specs=pl.BlockSpec((tq, gw), lambda b, g, i: (b * nq + i, g)),
        out_shape=jax.ShapeDtypeStruct((m, n_heads * hw), BF16),
        compiler_params=_params("arbitrary", "arbitrary", "arbitrary"),
        name="diff_attention",
    )(qk, qk, vt, lam_p, subln.reshape(1, hw))


def _merge_kernel(ya_ref, yb_ref, wa_ref, wb_ref, sa_ref, sb_ref, o_ref):
    pa = jnp.dot(ya_ref[...], wa_ref[...], preferred_element_type=F32)
    pb = jnp.dot(yb_ref[...], wb_ref[...], preferred_element_type=F32)
    o_ref[...] = (sa_ref[...].astype(F32) * pa + sb_ref[...].astype(F32) * pb).astype(o_ref.dtype)


def _merge(ya, yb, wa, wb, qk, gate_col0, bm, bn):
    m = ya.shape[0]
    d = wa.shape[1]
    ga0 = gate_col0 // bn
    gb0 = (gate_col0 + d) // bn
    return pl.pallas_call(
        _merge_kernel,
        grid=(m // bm, d // bn),
        in_specs=[
            pl.BlockSpec((bm, ya.shape[1]), lambda i, j: (i, 0)),
            pl.BlockSpec((bm, yb.shape[1]), lambda i, j: (i, 0)),
            pl.BlockSpec((wa.shape[0], bn), lambda i, j: (0, j)),
            pl.BlockSpec((wb.shape[0], bn), lambda i, j: (0, j)),
            pl.BlockSpec((bm, bn), lambda i, j: (i, ga0 + j)),
            pl.BlockSpec((bm, bn), lambda i, j: (i, gb0 + j)),
        ],
        out_specs=pl.BlockSpec((bm, bn), lambda i, j: (i, j)),
        out_shape=jax.ShapeDtypeStruct((m, d), BF16),
        compiler_params=_params("arbitrary", "arbitrary"),
        name="gated_merge",
    )(ya, yb, wa, wb, qk, qk)


def _ffn(x2, gain, w_in, w_out, mod3, j0, seq):
    m, d = x2.shape
    f = w_out.shape[0]
    bm = _tile(seq, 1024)
    fp = _round_up(f, 1024) if f >= 1024 else f
    bn_in = _tile(fp, 512)
    wg = jnp.pad(w_in[:, :f].astype(BF16), ((0, 0), (0, fp - f)))
    wu = jnp.pad(w_in[:, f:].astype(BF16), ((0, 0), (0, fp - f)))
    wo = jnp.pad(w_out.astype(BF16), ((0, fp - f), (0, 0)))
    h = _norm_mod(x2, gain, mod3, N_MOD, j0, j0 + 1, seq, BF16)
    a = _ffn_in(h, wg, wu, bm, bn_in)
    bk = fp // 4 if (fp // 4) % LANE == 0 else fp
    return _res_matmul(a, wo, x2, mod3, j0 + 2, 0.5, seq, bm, _tile(d, 1024), bk, "ffn_out")


def kernel(x, c, positions, ada_w, ada_b, norm_ffn1, ffn1_w_in, ffn1_w_out, norm_mix, w_in, b_forget, b_gate, diff_lambda, diff_subln, w_o_fox, w_o_diff, w_out, norm_ffn2, ffn2_w_in, ffn2_w_out, final_ada_w, final_ada_b, norm_final):
    batch, seq, d = x.shape
    depth = ada_w.shape[0]
    m = batch * seq
    n_fox = b_forget.shape[1]
    fox_w = n_fox * HEAD_DIM
    diff_w = w_o_diff.shape[1]
    n_diff = diff_w // (2 * HEAD_DIM)
    assert w_in.shape[2] == 3 * fox_w + n_fox + 3 * diff_w + 2 * d
    assert n_fox <= LANE and seq % CHUNK == 0

    c_pad = jnp.pad(c, ((0, -batch % 8), (0, 0)))
    x2 = x.reshape(m, d)

    inv_freq = ROPE_THETA ** (-jnp.arange(0, ROPE_DIM, 2, dtype=F32) / ROPE_DIM)
    ang = positions.astype(F32).reshape(m, 1) * inv_freq
    cos, sin = jnp.cos(ang), jnp.sin(ang)
    cosf = jnp.concatenate([cos, cos, jnp.ones((m, HEAD_DIM - ROPE_DIM), F32)], axis=-1)
    sinf = jnp.concatenate([-sin, sin, jnp.zeros((m, HEAD_DIM - ROPE_DIM), F32)], axis=-1)

    bm = _tile(seq, 1024)
    tq = _tile(seq, 512)
    for l in range(depth):
        lambda_init = 0.8 - 0.6 * math.exp(-0.3 * l)
        mod = _ada(c_pad, ada_w[l], ada_b[l])[:batch]
        mod3 = mod.reshape(batch * N_MOD, 1, d)

        x2 = _ffn(x2, norm_ffn1[l], ffn1_w_in[l], ffn1_w_out[l], mod3, 0, seq)

        h = _norm_mod(x2, norm_mix[l], mod3, N_MOD, 3, 4, seq, BF16)
        wl = w_in[l]
        o_fk, o_fv, o_ff = fox_w, 2 * fox_w, 3 * fox_w
        o_dq = o_ff + n_fox
        o_dk, o_dv, o_g = o_dq + diff_w, o_dq + 2 * diff_w, o_dq + 3 * diff_w
        w_qkg = jnp.concatenate([wl[:, :o_fv], wl[:, o_dq:o_dv], wl[:, o_g:]], axis=1).astype(BF16)
        w_vt = jnp.concatenate([wl[:, o_fv:o_ff], wl[:, o_dv:o_g]], axis=1).T.astype(BF16)
        w_ff = jnp.pad(wl[:, o_ff:o_dq].astype(BF16), ((0, 0), (0, LANE - n_fox)))
        b_ff = jnp.pad(b_forget[l].astype(F32), (0, LANE - n_fox)).reshape(1, LANE)

        bn_p = _tile(math.gcd(fox_w, diff_w), 1024)
        qk = _proj(h, w_qkg, cosf, sinf, b_gate[l], fox_w, diff_w, bm, bn_p)
        vt = _vt_proj(w_vt, h, bm, bn_p, tq)
        kx = _forget(h, w_ff, b_ff, n_fox, batch, seq)

        ya = _fox(qk, kx, vt, batch, seq, n_fox, tq, math.gcd(n_fox, FOX_HEADS_PER_STEP))
        yb = _diff(qk, vt, diff_lambda[l].astype(F32), diff_subln[l], batch, seq, n_diff,
                   2 * fox_w, fox_w, tq, math.gcd(n_diff, DIFF_HEADS_PER_STEP), lambda_init)
        merged = _merge(ya, yb, w_o_fox[l].astype(BF16), w_o_diff[l].astype(BF16), qk,
                        2 * fox_w + 2 * diff_w, bm, _tile(d, 512))
        x2 = _res_matmul(merged, w_out[l].astype(BF16), x2, mod3, 5, 1.0, seq, bm,
                         _tile(d, 512), d, "mixer_out")

        x2 = _ffn(x2, norm_ffn2[l], ffn2_w_in[l], ffn2_w_out[l], mod3, 6, seq)

    fmod = _ada(c_pad, final_ada_w, final_ada_b)[:batch]
    out = _norm_mod(x2, norm_final, fmod.reshape(batch * 2, 1, d), 2, 0, 1, seq, F32)
    return out.reshape(batch, seq, d)
```

```python
import functools
import math

import numpy as np
import jax
import jax.numpy as jnp
from jax import lax
from jax.experimental import pallas as pl
from jax.experimental.pallas import tpu as pltpu

HEAD_DIM = 128
CHUNK = 64
ROPE_THETA = 500000.0
ROPE_DIM = HEAD_DIM // 4
ROPE_HALF = ROPE_DIM // 2
EPS = 1e-6
N_MOD = 9
LOG2E = 1.4426950408889634
MASKED = -1e30
LANE = 128
VMEM_LIMIT = 56 * 1024 * 1024
CAST_BLOCK_ELEMS = 1024 * 1024
N_SPLIT = 3
FOX_HEADS_PER_STEP = 4
DIFF_HEADS_PER_STEP = 2

F32 = jnp.float32
BF16 = jnp.bfloat16


def _params(*sem):
    return pltpu.CompilerParams(dimension_semantics=sem, vmem_limit_bytes=VMEM_LIMIT)


def _tile(n, pref):
    t = min(n, pref)
    while n % t:
        t //= 2
    return t


def _round_up(n, m):
    return (n + m - 1) // m * m


def _nt_dot(a, b):
    return lax.dot_general(a, b, (((1,), (1,)), ((), ())), preferred_element_type=F32)


def _cast_pad_kernel(src_ref, o_ref, *, axis, n_valid):
    j = pl.program_id(axis)

    @pl.when(j < n_valid)
    def _():
        o_ref[...] = src_ref[...].astype(o_ref.dtype)

    @pl.when(j >= n_valid)
    def _():
        o_ref[...] = jnp.zeros_like(o_ref)


def _cast_pad(src, axis, start, n, n_out, blk):
    assert start % blk == 0 and n % blk == 0 and n_out % blk == 0
    other = src.shape[1 - axis]
    ob = _tile(other, max(CAST_BLOCK_ELEMS // blk, 8))
    b0, nv = start // blk, n // blk
    if axis == 1:
        block = (ob, blk)
        grid = (other // ob, n_out // blk)
        src_map = lambda i, j: (i, b0 + jnp.minimum(j, nv - 1))
        out_shape = (other, n_out)
    else:
        block = (blk, ob)
        grid = (n_out // blk, other // ob)
        src_map = lambda i, j: (b0 + jnp.minimum(i, nv - 1), j)
        out_shape = (n_out, other)
    return pl.pallas_call(
        functools.partial(_cast_pad_kernel, axis=axis, n_valid=nv),
        grid=grid,
        in_specs=[pl.BlockSpec(block, src_map)],
        out_specs=pl.BlockSpec(block, lambda i, j: (i, j)),
        out_shape=jax.ShapeDtypeStruct(out_shape, BF16),
        compiler_params=_params("arbitrary", "arbitrary"),
        name="cast_pad",
    )(src)


def _cast_shift_kernel(main_ref, extra_ref, o_ref, *, shift):
    cat = jnp.concatenate([main_ref[...], extra_ref[...]], axis=1)
    o_ref[...] = cat[:, shift:shift + o_ref.shape[1]].astype(o_ref.dtype)


def _cast_cols_unaligned(src, c0, n):
    rows = src.shape[0]
    base, shift = c0 // LANE * LANE, c0 % LANE
    cw = _tile(math.gcd(base, n), 2048)
    rk = _tile(rows, CAST_BLOCK_ELEMS // cw)
    per = cw // LANE
    return pl.pallas_call(
        functools.partial(_cast_shift_kernel, shift=shift),
        grid=(rows // rk, n // cw),
        in_specs=[
            pl.BlockSpec((rk, cw), lambda i, j: (i, base // cw + j)),
            pl.BlockSpec((rk, LANE), lambda i, j: (i, base // LANE + (j + 1) * per)),
        ],
        out_specs=pl.BlockSpec((rk, cw), lambda i, j: (i, j)),
        out_shape=jax.ShapeDtypeStruct((rows, n), BF16),
        compiler_params=_params("arbitrary", "arbitrary"),
        name="cast_shift",
    )(src, src)


def _ada_kernel(c_ref, w_ref, b_ref, o_ref):
    c = c_ref[...]
    sc = (c * jax.nn.sigmoid(c)).astype(BF16)
    o_ref[...] = jnp.dot(sc, w_ref[...].astype(BF16), preferred_element_type=F32) + b_ref[...]


def _ada(c_pad, w, b):
    d, n = w.shape
    tn = _tile(n, 512)
    return pl.pallas_call(
        _ada_kernel,
        grid=(n // tn,),
        in_specs=[
            pl.BlockSpec((c_pad.shape[0], d), lambda j: (0, 0)),
            pl.BlockSpec((d, tn), lambda j: (0, j)),
            pl.BlockSpec((1, tn), lambda j: (0, j)),
        ],
        out_specs=pl.BlockSpec((c_pad.shape[0], tn), lambda j: (0, j)),
        out_shape=jax.ShapeDtypeStruct((c_pad.shape[0], n), F32),
        compiler_params=_params("arbitrary"),
        name="ada_matvec",
    )(c_pad, w, b.reshape(1, n))


def _norm_mod_kernel(x_ref, g_ref, sh_ref, sc_ref, o_ref):
    x = x_ref[...]
    ms = jnp.mean(x * x, axis=-1, keepdims=True)
    y = x * lax.rsqrt(ms + EPS) * g_ref[...]
    o_ref[...] = (y * (1.0 + sc_ref[0]) + sh_ref[0]).astype(o_ref.dtype)


def _norm_mod(x2, gain, mod3, n_mod, shift_j, scale_j, seq, out_dtype):
    m, d = x2.shape
    tm = _tile(seq, 256)
    per_b = seq // tm
    return pl.pallas_call(
        _norm_mod_kernel,
        grid=(m // tm,),
        in_specs=[
            pl.BlockSpec((tm, d), lambda i: (i, 0)),
            pl.BlockSpec((1, d), lambda i: (0, 0)),
            pl.BlockSpec((1, 1, d), lambda i: ((i // per_b) * n_mod + shift_j, 0, 0)),
            pl.BlockSpec((1, 1, d), lambda i: ((i // per_b) * n_mod + scale_j, 0, 0)),
        ],
        out_specs=pl.BlockSpec((tm, d), lambda i: (i, 0)),
        out_shape=jax.ShapeDtypeStruct((m, d), out_dtype),
        compiler_params=_params("arbitrary"),
        name="norm_mod",
    )(x2, gain.reshape(1, d), mod3, mod3)


def _ffn_in_kernel(h_ref, wg_ref, wu_ref, o_ref):
    h = h_ref[...]
    g = jnp.dot(h, wg_ref[...], preferred_element_type=F32)
    u = jnp.dot(h, wu_ref[...], preferred_element_type=F32)
    o_ref[...] = (g * jax.nn.sigmoid(g) * u).astype(o_ref.dtype)


def _ffn_in(h, wg, wu, bm, bn):
    m, d = h.shape
    fp = wg.shape[1]
    return pl.pallas_call(
        _ffn_in_kernel,
        grid=(m // bm, fp // bn),
        in_specs=[
            pl.BlockSpec((bm, d), lambda i, j: (i, 0)),
            pl.BlockSpec((d, bn), lambda i, j: (0, j)),
            pl.BlockSpec((d, bn), lambda i, j: (0, j)),
        ],
        out_specs=pl.BlockSpec((bm, bn), lambda i, j: (i, j)),
        out_shape=jax.ShapeDtypeStruct((m, fp), BF16),
        compiler_params=_params("arbitrary", "arbitrary"),
        name="ffn_in",
    )(h, wg, wu)


def _res_matmul_kernel(a_ref, w_ref, res_ref, gate_ref, o_ref, acc_ref, *, coef):
    k = pl.program_id(2)

    @pl.when(k == 0)
    def _():
        acc_ref[...] = jnp.zeros_like(acc_ref)

    acc_ref[...] += jnp.dot(a_ref[...], w_ref[...], preferred_element_type=F32)

    @pl.when(k == pl.num_programs(2) - 1)
    def _():
        o_ref[...] = res_ref[...] + (coef * gate_ref[0]) * acc_ref[...]


def _res_matmul(a, w, res, mod3, gate_j, coef, seq, bm, bn, bk, name):
    m, kdim = a.shape
    n = w.shape[1]
    per_b = seq // bm
    return pl.pallas_call(
        functools.partial(_res_matmul_kernel, coef=coef),
        grid=(m // bm, n // bn, kdim // bk),
        in_specs=[
            pl.BlockSpec((bm, bk), lambda i, j, k: (i, k)),
            pl.BlockSpec((bk, bn), lambda i, j, k: (k, j)),
            pl.BlockSpec((bm, bn), lambda i, j, k: (i, j)),
            pl.BlockSpec((1, 1, bn), lambda i, j, k: ((i // per_b) * N_MOD + gate_j, 0, j)),
        ],
        out_specs=pl.BlockSpec((bm, bn), lambda i, j, k: (i, j)),
        out_shape=jax.ShapeDtypeStruct((m, n), F32),
        scratch_shapes=[pltpu.VMEM((bm, bn), F32)],
        compiler_params=_params("arbitrary", "arbitrary", "arbitrary"),
        name=name,
    )(a, w, res, mod3)


QSCALE = HEAD_DIM ** -0.5 * LOG2E


def _proj_call(body, h, w, col0, n, bm, bn, extra_in, extra_specs, name):
    m, d = h.shape
    b0 = col0 // bn
    return pl.pallas_call(
        body,
        grid=(m // bm, n // bn),
        in_specs=[
            pl.BlockSpec((bm, d), lambda i, j: (i, 0)),
            pl.BlockSpec((d, bn), lambda i, j: (0, b0 + j)),
        ] + extra_specs,
        out_specs=pl.BlockSpec((bm, bn), lambda i, j: (i, j)),
        out_shape=jax.ShapeDtypeStruct((m, n), BF16),
        compiler_params=_params("arbitrary", "arbitrary"),
        name=name,
    )(h, w, *extra_in)


def _block_scale(n_scaled):
    return jnp.where(pl.program_id(1) < n_scaled, QSCALE, 1.0).astype(F32)


def _proj_plain_kernel(h_ref, w_ref, o_ref, *, n_scaled):
    acc = jnp.dot(h_ref[...], w_ref[...], preferred_element_type=F32)
    o_ref[...] = (acc * _block_scale(n_scaled)).astype(o_ref.dtype)


def _proj_rope_kernel(h_ref, w_ref, cos_ref, sin_ref, o_ref, *, n_scaled):
    acc = jnp.dot(h_ref[...], w_ref[...], preferred_element_type=F32)
    cosf = cos_ref[...] * _block_scale(n_scaled)
    sinf = sin_ref[...] * _block_scale(n_scaled)
    lane = lax.broadcasted_iota(jnp.int32, cosf.shape, 1)
    for c in range(acc.shape[1] // HEAD_DIM):
        cols = slice(c * HEAD_DIM, (c + 1) * HEAD_DIM)
        t = acc[:, cols]
        partner = jnp.where(lane < ROPE_HALF,
                            pltpu.roll(t, HEAD_DIM - ROPE_HALF, 1),
                            pltpu.roll(t, ROPE_HALF, 1))
        o_ref[:, cols] = (t * cosf + partner * sinf).astype(o_ref.dtype)


def _proj_gate_kernel(h_ref, w_ref, b_ref, o_ref):
    acc = jnp.dot(h_ref[...], w_ref[...], preferred_element_type=F32)
    o_ref[...] = jax.nn.sigmoid(acc + b_ref[...]).astype(o_ref.dtype)


def _proj_plain(h, w, col0, n, n_scaled, bm, bn):
    return _proj_call(functools.partial(_proj_plain_kernel, n_scaled=n_scaled),
                      h, w, col0, n, bm, bn, [], [], "proj_plain")


def _proj_rope(h, w, col0, n, n_scaled, cosf, sinf, bm, bn):
    table = pl.BlockSpec((bm, HEAD_DIM), lambda i, j: (i, 0))
    return _proj_call(functools.partial(_proj_rope_kernel, n_scaled=n_scaled),
                      h, w, col0, n, bm, bn, [cosf, sinf], [table, table], "proj_rope")


def _proj_gate(h, w, col0, n, bias, bm, bn):
    return _proj_call(_proj_gate_kernel, h, w, col0, n, bm, bn, [bias.reshape(1, n)],
                      [pl.BlockSpec((1, bn), lambda i, j: (0, j))], "proj_gate")


def _vt_proj_kernel(w_ref, h_ref, o_ref, wt_ref, *, tk):
    @pl.when(pl.program_id(1) == 0)
    def _():
        step = wt_ref.shape[0]
        for c in range(w_ref.shape[0] // step):
            rows = slice(c * step, (c + 1) * step)
            wt_ref[:, rows] = w_ref[rows, :].astype(F32).T.astype(wt_ref.dtype)

    r = _nt_dot(wt_ref[...], h_ref[...])
    for t in range(o_ref.shape[0]):
        o_ref[t] = r[:, t * tk:(t + 1) * tk].astype(o_ref.dtype)


def _vt_proj(h, w, col0, n, bm, bn, tk):
    m, d = h.shape
    b0 = col0 // bn
    return pl.pallas_call(
        functools.partial(_vt_proj_kernel, tk=tk),
        grid=(n // bn, m // bm),
        in_specs=[
            pl.BlockSpec((d, bn), lambda j, i: (0, b0 + j)),
            pl.BlockSpec((bm, d), lambda j, i: (i, 0)),
        ],
        out_specs=pl.BlockSpec((bm // tk, bn, tk), lambda j, i: (i, j, 0)),
        out_shape=jax.ShapeDtypeStruct((m // tk, n, tk), BF16),
        scratch_shapes=[pltpu.VMEM((bn, d), BF16)],
        compiler_params=_params("arbitrary", "arbitrary"),
        name="proj_vt",
    )(w, h)


def _forget_kernel(h_ref, w_ref, b_ref, e_ref, o_ref, carry_ref):
    @pl.when(pl.program_id(1) == 0)
    def _():
        carry_ref[...] = jnp.zeros_like(carry_ref)

    ff = jnp.dot(h_ref[...], w_ref[...], preferred_element_type=F32) + b_ref[...]
    logf = jnp.minimum(ff, 0.0) - jnp.log1p(jnp.exp(-jnp.abs(ff)))
    tm = logf.shape[0]
    row = lax.broadcasted_iota(jnp.int32, (tm, tm), 0)
    col = lax.broadcasted_iota(jnp.int32, (tm, tm), 1)
    tri = (row >= col).astype(F32)
    cum = jnp.dot(tri, logf, preferred_element_type=F32,
                  precision=lax.Precision.HIGHEST) + carry_ref[...]
    carry_ref[...] = cum[tm - 1:tm, :]

    rest = cum * (-LOG2E)
    pieces = []
    for _ in range(N_SPLIT):
        p = rest.astype(BF16)
        pieces.append(p)
        rest = rest - p.astype(F32)
    spread = jnp.dot(jnp.concatenate(pieces, axis=1), e_ref[...], preferred_element_type=F32)
    o_ref[...] = spread.astype(o_ref.dtype)


def _forget(h, w_ff, b_ff, n_heads, batch, seq):
    m, d = h.shape
    tm = _tile(seq, 256)
    per_b = seq // tm
    route = np.zeros((N_SPLIT * LANE, n_heads * HEAD_DIM), np.float32)
    for p in range(N_SPLIT):
        for hh in range(n_heads):
            route[p * LANE + hh, hh * HEAD_DIM + p] = 1.0
    return pl.pallas_call(
        _forget_kernel,
        grid=(batch, per_b),
        in_specs=[
            pl.BlockSpec((tm, d), lambda b, i: (b * per_b + i, 0)),
            pl.BlockSpec((d, LANE), lambda b, i: (0, 0)),
            pl.BlockSpec((1, LANE), lambda b, i: (0, 0)),
            pl.BlockSpec(route.shape, lambda b, i: (0, 0)),
        ],
        out_specs=pl.BlockSpec((tm, n_heads * HEAD_DIM), lambda b, i: (b * per_b + i, 0)),
        out_shape=jax.ShapeDtypeStruct((m, n_heads * HEAD_DIM), BF16),
        scratch_shapes=[pltpu.VMEM((1, LANE), F32)],
        compiler_params=_params("arbitrary", "arbitrary"),
        name="forget_cumsum",
    )(h, w_ff, b_ff, jnp.asarray(route, BF16))


def _online_softmax_step(st, vt, m, l, acc):
    m_new = jnp.maximum(m, jnp.max(st, axis=0, keepdims=True))
    alpha = jnp.exp2(m - m_new)
    p = jnp.exp2(st - m_new)
    l_new = alpha * l + jnp.sum(p, axis=0, keepdims=True)
    acc_new = alpha * acc + jnp.dot(vt, p.astype(vt.dtype), preferred_element_type=F32)
    return m_new, l_new, acc_new


def _softmax_init(tq, dv):
    return (jnp.full((1, tq), MASKED, F32), jnp.zeros((1, tq), F32), jnp.zeros((dv, tq), F32))


def _fox_kernel(q_ref, k_ref, kx_ref, vt_ref, o_ref, *, tq, nh):
    qi = pl.program_id(2)
    lane = lax.broadcasted_iota(jnp.int32, (tq, HEAD_DIM), 1)
    ones = (lane < N_SPLIT).astype(q_ref.dtype)

    def head(a):
        return slice(a * HEAD_DIM, (a + 1) * HEAD_DIM)

    def block(kj, carry, diagonal):
        ks = pl.multiple_of(kj * tq, tq)
        sts = []
        for a in range(nh):
            q_aug = jnp.concatenate([q_ref[:, head(a)], ones], axis=1)
            k_aug = jnp.concatenate([k_ref[pl.ds(ks, tq), head(a)],
                                     kx_ref[pl.ds(ks, tq), head(a)]], axis=1)
            st = _nt_dot(k_aug, q_aug)
            if diagonal:
                key = lax.broadcasted_iota(jnp.int32, st.shape, 0)
                qry = lax.broadcasted_iota(jnp.int32, st.shape, 1)
                st = jnp.where(key <= qry, st, MASKED)
            sts.append(st)
        out = ()
        for a in range(nh):
            out += _online_softmax_step(sts[a], vt_ref[kj, head(a), :], *carry[3 * a:3 * a + 3])
        return out

    carry = lax.fori_loop(0, qi, lambda kj, c: block(kj, c, False),
                          _softmax_init(tq, HEAD_DIM) * nh)
    carry = block(qi, carry, True)
    for a in range(nh):
        _, l, acc = carry[3 * a:3 * a + 3]
        o_ref[:, head(a)] = (acc / l).T.astype(o_ref.dtype)


def _fox(qk, kx, vt, batch, seq, n_heads, tq, nh):
    m = qk.shape[0]
    nq = seq // tq
    gw = nh * HEAD_DIM
    ng = n_heads // nh
    return pl.pallas_call(
        functools.partial(_fox_kernel, tq=tq, nh=nh),
        grid=(batch, ng, nq),
        in_specs=[
            pl.BlockSpec((tq, gw), lambda b, g, i: (b * nq + i, g)),
            pl.BlockSpec((seq, gw), lambda b, g, i: (b, ng + g)),
            pl.BlockSpec((seq, gw), lambda b, g, i: (b, g)),
            pl.BlockSpec((nq, gw, tq), lambda b, g, i: (b, g, 0)),
        ],
        out_specs=pl.BlockSpec((tq, gw), lambda b, g, i: (b * nq + i, g)),
        out_shape=jax.ShapeDtypeStruct((m, n_heads * HEAD_DIM), BF16),
        compiler_params=_params("arbitrary", "arbitrary", "arbitrary"),
        name="fox_attention",
    )(qk, qk, kx, vt)


def _diff_kernel(q_ref, k_ref, vt_ref, lam_ref, g_ref, o_ref, *, tq, nh, lambda_init):
    qi = pl.program_id(2)
    n_maps = 2 * nh

    def cols(c):
        return slice(c * HEAD_DIM, (c + 1) * HEAD_DIM)

    def block(kj, carry, diagonal):
        ks = pl.multiple_of(kj * tq, tq)
        sts = []
        for c in range(n_maps):
            st = _nt_dot(k_ref[pl.ds(ks, tq), cols(c)], q_ref[:, cols(c)])
            if diagonal:
                key = lax.broadcasted_iota(jnp.int32, st.shape, 0) // CHUNK
                qry = lax.broadcasted_iota(jnp.int32, st.shape, 1) // CHUNK
                st = jnp.where(key <= qry, st, MASKED)
            sts.append(st)
        out = ()
        for c in range(n_maps):
            vt = vt_ref[kj, (c // 2) * 2 * HEAD_DIM:(c // 2 + 1) * 2 * HEAD_DIM, :]
            out += _online_softmax_step(sts[c], vt, *carry[3 * c:3 * c + 3])
        return out

    carry = lax.fori_loop(0, qi, lambda kj, c: block(kj, c, False),
                          _softmax_init(tq, 2 * HEAD_DIM) * n_maps)
    carry = block(qi, carry, True)

    lp = lam_ref[...]
    lam = (jnp.exp(jnp.sum(lp[0:1] * lp[1:2], axis=-1, keepdims=True))
           - jnp.exp(jnp.sum(lp[2:3] * lp[3:4], axis=-1, keepdims=True)) + lambda_init)
    for a in range(nh):
        _, l1, a1, _, l2, a2 = carry[6 * a:6 * a + 6]
        y = a1 / l1 - lam * (a2 / l2)
        ms = jnp.mean(y * y, axis=0, keepdims=True)
        y = (y * lax.rsqrt(ms + EPS)).T * g_ref[...]
        o_ref[:, a * 2 * HEAD_DIM:(a + 1) * 2 * HEAD_DIM] = (y * (1.0 - lambda_init)).astype(o_ref.dtype)


def _diff(qk, vt, lam_p, subln, batch, seq, n_heads, tq, nh, lambda_init):
    m = qk.shape[0]
    nq = seq // tq
    hw = 2 * HEAD_DIM
    gw = nh * hw
    ng = n_heads // nh
    return pl.pallas_call(
        functools.partial(_diff_kernel, tq=tq, nh=nh, lambda_init=lambda_init),
        grid=(batch, ng, nq),
        in_specs=[
            pl.BlockSpec((tq, gw), lambda b, g, i: (b * nq + i, g)),
            pl.BlockSpec((seq, gw), lambda b, g, i: (b, ng + g)),
            pl.BlockSpec((nq, gw, tq), lambda b, g, i: (b, g, 0)),
            pl.BlockSpec((4, HEAD_DIM), lambda b, g, i: (0, 0)),
            pl.BlockSpec((1, hw), lambda b, g, i: (0, 0)),
        ],
        out_specs=pl.BlockSpec((tq, gw), lambda b, g, i: (b * nq + i, g)),
        out_shape=jax.ShapeDtypeStruct((m, n_heads * hw), BF16),
        compiler_params=_params("arbitrary", "arbitrary", "arbitrary"),
        name="diff_attention",
    )(qk, qk, vt, lam_p, subln.reshape(1, hw))


def _merge_kernel(ya_ref, yb_ref, wa_ref, wb_ref, sa_ref, sb_ref, o_ref):
    pa = jnp.dot(ya_ref[...], wa_ref[...], preferred_element_type=F32)
    pb = jnp.dot(yb_ref[...], wb_ref[...], preferred_element_type=F32)
    o_ref[...] = (sa_ref[...].astype(F32) * pa + sb_ref[...].astype(F32) * pb).astype(o_ref.dtype)


def _merge(ya, yb, wa, wb, gates, bm, bn):
    m = ya.shape[0]
    d = wa.shape[1]
    nb = d // bn
    return pl.pallas_call(
        _merge_kernel,
        grid=(m // bm, nb),
        in_specs=[
            pl.BlockSpec((bm, ya.shape[1]), lambda i, j: (i, 0)),
            pl.BlockSpec((bm, yb.shape[1]), lambda i, j: (i, 0)),
            pl.BlockSpec((wa.shape[0], bn), lambda i, j: (0, j)),
            pl.BlockSpec((wb.shape[0], bn), lambda i, j: (0, j)),
            pl.BlockSpec((bm, bn), lambda i, j: (i, j)),
            pl.BlockSpec((bm, bn), lambda i, j: (i, nb + j)),
        ],
        out_specs=pl.BlockSpec((bm, bn), lambda i, j: (i, j)),
        out_shape=jax.ShapeDtypeStruct((m, d), BF16),
        compiler_params=_params("arbitrary", "arbitrary"),
        name="gated_merge",
    )(ya, yb, wa, wb, gates, gates)


def _ffn(x2, gain, w_in, w_out, mod3, j0, seq):
    m, d = x2.shape
    f = w_out.shape[0]
    bm = _tile(seq, 1024)
    fp = _round_up(f, 1024) if f >= 1024 else f
    bn_in = _tile(fp, 512)
    cb = _tile(math.gcd(f, fp), 256)
    wg = _cast_pad(w_in, 1, 0, f, fp, cb)
    wu = _cast_pad(w_in, 1, f, f, fp, cb)
    wo = _cast_pad(w_out, 0, 0, f, fp, cb)
    h = _norm_mod(x2, gain, mod3, N_MOD, j0, j0 + 1, seq, BF16)
    a = _ffn_in(h, wg, wu, bm, bn_in)
    bk = fp // 4 if (fp // 4) % LANE == 0 else fp
    return _res_matmul(a, wo, x2, mod3, j0 + 2, 0.5, seq, bm, _tile(d, 1024), bk, "ffn_out")


def kernel(x, c, positions, ada_w, ada_b, norm_ffn1, ffn1_w_in, ffn1_w_out, norm_mix, w_in, b_forget, b_gate, diff_lambda, diff_subln, w_o_fox, w_o_diff, w_out, norm_ffn2, ffn2_w_in, ffn2_w_out, final_ada_w, final_ada_b, norm_final):
    batch, seq, d = x.shape
    depth = ada_w.shape[0]
    m = batch * seq
    n_fox = b_forget.shape[1]
    fox_w = n_fox * HEAD_DIM
    diff_w = w_o_diff.shape[1]
    n_diff = diff_w // (2 * HEAD_DIM)
    assert w_in.shape[2] == 3 * fox_w + n_fox + 3 * diff_w + 2 * d
    assert n_fox < LANE and seq % CHUNK == 0

    c_pad = jnp.pad(c, ((0, -batch % 8), (0, 0)))
    x2 = x.reshape(m, d)

    inv_freq = ROPE_THETA ** (-jnp.arange(0, ROPE_DIM, 2, dtype=F32) / ROPE_DIM)
    ang = positions.astype(F32).reshape(m, 1) * inv_freq
    cos, sin = jnp.cos(ang), jnp.sin(ang)
    cosf = jnp.concatenate([cos, cos, jnp.ones((m, HEAD_DIM - ROPE_DIM), F32)], axis=-1)
    sinf = jnp.concatenate([-sin, sin, jnp.zeros((m, HEAD_DIM - ROPE_DIM), F32)], axis=-1)

    bm = _tile(seq, 1024)
    tq = _tile(seq, 512)
    bn_p = _tile(math.gcd(fox_w, diff_w), 1024)
    bn_v = _tile(math.gcd(fox_w, diff_w), 512)
    for l in range(depth):
        lambda_init = 0.8 - 0.6 * math.exp(-0.3 * l)
        mod = _ada(c_pad, ada_w[l], ada_b[l])[:batch]
        mod3 = mod.reshape(batch * N_MOD, 1, d)

        x2 = _ffn(x2, norm_ffn1[l], ffn1_w_in[l], ffn1_w_out[l], mod3, 0, seq)

        h = _norm_mod(x2, norm_mix[l], mod3, N_MOD, 3, 4, seq, BF16)
        wl = w_in[l]
        o_ff = 3 * fox_w
        w_front = _cast_pad(wl, 1, 0, o_ff, o_ff, fox_w)
        w_rest = _cast_cols_unaligned(wl, o_ff + n_fox, 3 * diff_w + 2 * d)
        w_ff = jnp.pad(wl[:, o_ff:o_ff + n_fox].astype(BF16), ((0, 0), (0, LANE - n_fox)))
        b_ff = jnp.pad(b_forget[l].astype(F32), (0, LANE - n_fox)).reshape(1, LANE)

        qk_f = _proj_plain(h, w_front, 0, 2 * fox_w, fox_w // bn_p, bm, bn_p)
        qk_d = _proj_rope(h, w_rest, 0, 2 * diff_w, diff_w // bn_p, cosf, sinf, bm, bn_p)
        gates = _proj_gate(h, w_rest, 3 * diff_w, 2 * d, b_gate[l], bm, bn_p)
        vt_f = _vt_proj(h, w_front, 2 * fox_w, fox_w, bm, bn_v, tq)
        vt_d = _vt_proj(h, w_rest, 2 * diff_w, diff_w, bm, bn_v, tq)
        kx = _forget(h, w_ff, b_ff, n_fox, batch, seq)

        ya = _fox(qk_f, kx, vt_f, batch, seq, n_fox, tq, math.gcd(n_fox, FOX_HEADS_PER_STEP))
        yb = _diff(qk_d, vt_d, diff_lambda[l].astype(F32), diff_subln[l], batch, seq, n_diff,
                   tq, math.gcd(n_diff, DIFF_HEADS_PER_STEP), lambda_init)
        wa = _cast_pad(w_o_fox[l], 1, 0, d, d, _tile(d, 2048))
        wb = _cast_pad(w_o_diff[l], 1, 0, d, d, _tile(d, 2048))
        wo = _cast_pad(w_out[l], 1, 0, d, d, _tile(d, 2048))
        merged = _merge(ya, yb, wa, wb, gates, bm, _tile(d, 512))
        x2 = _res_matmul(merged, wo, x2, mod3, 5, 1.0, seq, bm, _tile(d, 512), d, "mixer_out")

        x2 = _ffn(x2, norm_ffn2[l], ffn2_w_in[l], ffn2_w_out[l], mod3, 6, seq)

    fmod = _ada(c_pad, final_ada_w, final_ada_b)[:batch]
    out = _norm_mod(x2, norm_final, fmod.reshape(batch * 2, 1, d), 2, 0, 1, seq, F32)
    return out.reshape(batch, seq, d)
```

```python
import functools
import math
from typing import NamedTuple

import numpy as np
import jax
import jax.numpy as jnp
from jax import lax
from jax.experimental import pallas as pl
from jax.experimental.pallas import tpu as pltpu

HEAD_DIM = 128
CHUNK = 64
ROPE_THETA = 500000.0
ROPE_DIM = HEAD_DIM // 4
ROPE_HALF = ROPE_DIM // 2
EPS = 1e-6
N_MOD = 9
LOG2E = 1.4426950408889634
MASKED = -1e30
LANE = 128
VMEM_LIMIT = 56 * 1024 * 1024
BF16_ROWS = 16
CAST_BLOCK_ELEMS = 1024 * 1024
HOSTED_STRIP_ELEMS = 1024 * 1024
N_SPLIT = 3
FOX_HEADS_PER_STEP = 4
DIFF_HEADS_PER_STEP = 2

F32 = jnp.float32
BF16 = jnp.bfloat16


def _params(*sem):
    return pltpu.CompilerParams(dimension_semantics=sem, vmem_limit_bytes=VMEM_LIMIT)


def _tile(n, pref):
    t = min(n, pref)
    while n % t:
        t //= 2
    return t


def _round_up(n, m):
    return (n + m - 1) // m * m


def _nt_dot(a, b):
    return lax.dot_general(a, b, (((1,), (1,)), ((), ())), preferred_element_type=F32)


class _StripCast(NamedTuple):
    src: jax.Array
    layer: int
    rows_out: int
    pieces: tuple


def _strip_rows(cast, lo, hi):
    g = math.gcd(cast.src.shape[1], cast.rows_out)
    for rs in range(_round_up(max(lo, 1), BF16_ROWS), min(hi, g) + 1, BF16_ROWS):
        if g % rs == 0:
            return rs
    return None


def _linear_step(grid, ids):
    t = ids[0]
    for g, i in zip(grid[1:], ids[1:]):
        t = t * g + i
    return t


def _strip_specs(cast, rs, grid):
    n_valid = cast.src.shape[1] // rs
    n_strips = cast.rows_out // rs
    cols = cast.src.shape[2]
    src_spec = pl.BlockSpec(
        (None, rs, cols),
        lambda *ids: (cast.layer, jnp.minimum(_linear_step(grid, ids), n_valid - 1), 0))
    out_specs = [pl.BlockSpec((rs, nco), lambda *ids: (jnp.minimum(_linear_step(grid, ids), n_strips - 1), 0))
                 for _, _, nco in cast.pieces]
    out_shapes = [jax.ShapeDtypeStruct((cast.rows_out, nco), BF16) for _, _, nco in cast.pieces]
    return src_spec, out_specs, out_shapes


def _strip_cast_body(cast, rs, grid, src_ref, dst_refs):
    t = _linear_step(grid, [pl.program_id(a) for a in range(len(grid))])
    n_valid = cast.src.shape[1] // rs
    strip = src_ref[...]
    for (c0, n, nco), dst in zip(cast.pieces, dst_refs):
        piece = strip[:, c0:c0 + n].astype(dst.dtype)
        if n_valid * rs < cast.rows_out:
            piece = jnp.where(t < n_valid, piece, jnp.zeros_like(piece))
        if nco > n:
            dst[:, :n] = piece
            dst[:, n:] = jnp.zeros((rs, nco - n), dst.dtype)
        else:
            dst[...] = piece


def _run_cast(cast):
    rs = _strip_rows(cast, 1, max(CAST_BLOCK_ELEMS // cast.src.shape[2], BF16_ROWS))
    rs = rs or _strip_rows(cast, 1, cast.rows_out)
    grid = (cast.rows_out // rs,)
    src_spec, out_specs, out_shapes = _strip_specs(cast, rs, grid)

    def body(src_ref, *dst_refs):
        _strip_cast_body(cast, rs, grid, src_ref, dst_refs)

    return pl.pallas_call(
        body, grid=grid, in_specs=[src_spec], out_specs=out_specs, out_shape=out_shapes,
        compiler_params=_params("arbitrary"), name="weight_cast",
    )(cast.src)


def _hosted_call(body, grid, in_specs, out_spec, out_shape, inputs, name, scratch=(), cast=None):
    params = _params(*(("arbitrary",) * len(grid)))
    steps = math.prod(grid)
    rs = None
    if cast is not None:
        rs = _strip_rows(cast, pl.cdiv(cast.rows_out, steps), cast.rows_out)
        if rs is not None and rs * cast.src.shape[2] > HOSTED_STRIP_ELEMS:
            rs = None
    if rs is None:
        out = pl.pallas_call(body, grid=grid, in_specs=in_specs, out_specs=out_spec,
                             out_shape=out_shape, scratch_shapes=list(scratch),
                             compiler_params=params, name=name)(*inputs)
        return out, (tuple(_run_cast(cast)) if cast is not None else ())

    src_spec, cast_specs, cast_shapes = _strip_specs(cast, rs, grid)
    n_in, n_cast = len(in_specs), len(cast.pieces)

    def hosted(*refs):
        ins, src_ref, o_ref = refs[:n_in], refs[n_in], refs[n_in + 1]
        dst_refs = refs[n_in + 2:n_in + 2 + n_cast]
        body(*ins, o_ref, *refs[n_in + 2 + n_cast:])
        _strip_cast_body(cast, rs, grid, src_ref, dst_refs)

    out = pl.pallas_call(
        hosted, grid=grid, in_specs=list(in_specs) + [src_spec],
        out_specs=[out_spec] + cast_specs, out_shape=[out_shape] + cast_shapes,
        scratch_shapes=list(scratch), compiler_params=params, name=name,
    )(*inputs, cast.src)
    return out[0], tuple(out[1:])


def _ada_kernel(c_ref, w_ref, b_ref, o_ref):
    c = c_ref[...]
    sc = (c * jax.nn.sigmoid(c)).astype(BF16)
    o_ref[...] = jnp.dot(sc, w_ref[...].astype(BF16), preferred_element_type=F32) + b_ref[...]


def _ada(c_pad, w, b):
    d, n = w.shape
    tn = _tile(n, 512)
    return pl.pallas_call(
        _ada_kernel,
        grid=(n // tn,),
        in_specs=[
            pl.BlockSpec((c_pad.shape[0], d), lambda j: (0, 0)),
            pl.BlockSpec((d, tn), lambda j: (0, j)),
            pl.BlockSpec((1, tn), lambda j: (0, j)),
        ],
        out_specs=pl.BlockSpec((c_pad.shape[0], tn), lambda j: (0, j)),
        out_shape=jax.ShapeDtypeStruct((c_pad.shape[0], n), F32),
        compiler_params=_params("arbitrary"),
        name="ada_matvec",
    )(c_pad, w, b.reshape(1, n))


def _norm_mod_kernel(x_ref, g_ref, sh_ref, sc_ref, o_ref):
    x = x_ref[...]
    ms = jnp.mean(x * x, axis=-1, keepdims=True)
    y = x * lax.rsqrt(ms + EPS) * g_ref[...]
    o_ref[...] = (y * (1.0 + sc_ref[0]) + sh_ref[0]).astype(o_ref.dtype)


def _norm_mod(x2, gain, mod3, n_mod, shift_j, scale_j, seq, out_dtype):
    m, d = x2.shape
    tm = _tile(seq, 256)
    per_b = seq // tm
    return pl.pallas_call(
        _norm_mod_kernel,
        grid=(m // tm,),
        in_specs=[
            pl.BlockSpec((tm, d), lambda i: (i, 0)),
            pl.BlockSpec((1, d), lambda i: (0, 0)),
            pl.BlockSpec((1, 1, d), lambda i: ((i // per_b) * n_mod + shift_j, 0, 0)),
            pl.BlockSpec((1, 1, d), lambda i: ((i // per_b) * n_mod + scale_j, 0, 0)),
        ],
        out_specs=pl.BlockSpec((tm, d), lambda i: (i, 0)),
        out_shape=jax.ShapeDtypeStruct((m, d), out_dtype),
        compiler_params=_params("arbitrary"),
        name="norm_mod",
    )(x2, gain.reshape(1, d), mod3, mod3)


def _ffn_in_kernel(h_ref, wg_ref, wu_ref, o_ref):
    h = h_ref[...]
    g = jnp.dot(h, wg_ref[...], preferred_element_type=F32)
    u = jnp.dot(h, wu_ref[...], preferred_element_type=F32)
    o_ref[...] = (g * jax.nn.sigmoid(g) * u).astype(o_ref.dtype)


def _ffn_in(h, wg, wu, bm, bn, cast):
    m, d = h.shape
    fp = wg.shape[1]
    return _hosted_call(
        _ffn_in_kernel,
        (m // bm, fp // bn),
        [
            pl.BlockSpec((bm, d), lambda i, j: (i, 0)),
            pl.BlockSpec((d, bn), lambda i, j: (0, j)),
            pl.BlockSpec((d, bn), lambda i, j: (0, j)),
        ],
        pl.BlockSpec((bm, bn), lambda i, j: (i, j)),
        jax.ShapeDtypeStruct((m, fp), BF16),
        (h, wg, wu), "ffn_in", cast=cast)


def _res_matmul_kernel(a_ref, w_ref, res_ref, gate_ref, o_ref, acc_ref, *, coef, nk):
    k = pl.program_id(2)

    def partial_product():
        return jnp.dot(a_ref[...], w_ref[...], preferred_element_type=F32)

    def finish(total):
        o_ref[...] = res_ref[...] + (coef * gate_ref[0]) * total

    if nk == 1:
        finish(partial_product())
        return

    @pl.when(k == 0)
    def _():
        acc_ref[...] = partial_product()

    @pl.when((k > 0) & (k < nk - 1))
    def _():
        acc_ref[...] += partial_product()

    @pl.when(k == nk - 1)
    def _():
        finish(acc_ref[...] + partial_product())


def _res_matmul(a, w, res, mod3, gate_j, coef, seq, bm, bn, bk, name, cast=None):
    m, kdim = a.shape
    n = w.shape[1]
    per_b = seq // bm
    return _hosted_call(
        functools.partial(_res_matmul_kernel, coef=coef, nk=kdim // bk),
        (m // bm, n // bn, kdim // bk),
        [
            pl.BlockSpec((bm, bk), lambda i, j, k: (i, k)),
            pl.BlockSpec((bk, bn), lambda i, j, k: (k, j)),
            pl.BlockSpec((bm, bn), lambda i, j, k: (i, j)),
            pl.BlockSpec((1, 1, bn), lambda i, j, k: ((i // per_b) * N_MOD + gate_j, 0, j)),
        ],
        pl.BlockSpec((bm, bn), lambda i, j, k: (i, j)),
        jax.ShapeDtypeStruct((m, n), F32),
        (a, w, res, mod3), name, scratch=[pltpu.VMEM((bm, bn), F32)], cast=cast)


QSCALE = HEAD_DIM ** -0.5 * LOG2E


def _proj_call(body, h, w, col0, n, bm, bn, extra_in, extra_specs, name, cast):
    m, d = h.shape
    b0 = col0 // bn
    return _hosted_call(
        body,
        (m // bm, n // bn),
        [
            pl.BlockSpec((bm, d), lambda i, j: (i, 0)),
            pl.BlockSpec((d, bn), lambda i, j: (0, b0 + j)),
        ] + extra_specs,
        pl.BlockSpec((bm, bn), lambda i, j: (i, j)),
        jax.ShapeDtypeStruct((m, n), BF16),
        (h, w, *extra_in), name, cast=cast)


def _block_scale(n_scaled):
    return jnp.where(pl.program_id(1) < n_scaled, QSCALE, 1.0).astype(F32)


def _proj_plain_kernel(h_ref, w_ref, o_ref, *, n_scaled):
    acc = jnp.dot(h_ref[...], w_ref[...], preferred_element_type=F32)
    o_ref[...] = (acc * _block_scale(n_scaled)).astype(o_ref.dtype)


def _proj_rope_kernel(h_ref, w_ref, cos_ref, sin_ref, o_ref, *, n_scaled):
    acc = jnp.dot(h_ref[...], w_ref[...], preferred_element_type=F32)
    cosf = cos_ref[...] * _block_scale(n_scaled)
    sinf = sin_ref[...] * _block_scale(n_scaled)
    lane = lax.broadcasted_iota(jnp.int32, cosf.shape, 1)
    for c in range(acc.shape[1] // HEAD_DIM):
        cols = slice(c * HEAD_DIM, (c + 1) * HEAD_DIM)
        t = acc[:, cols]
        partner = jnp.where(lane < ROPE_HALF,
                            pltpu.roll(t, HEAD_DIM - ROPE_HALF, 1),
                            pltpu.roll(t, ROPE_HALF, 1))
        o_ref[:, cols] = (t * cosf + partner * sinf).astype(o_ref.dtype)


def _proj_gate_kernel(h_ref, w_ref, b_ref, o_ref):
    acc = jnp.dot(h_ref[...], w_ref[...], preferred_element_type=F32)
    o_ref[...] = jax.nn.sigmoid(acc + b_ref[...]).astype(o_ref.dtype)


def _proj_plain(h, w, col0, n, n_scaled, bm, bn, cast):
    return _proj_call(functools.partial(_proj_plain_kernel, n_scaled=n_scaled),
                      h, w, col0, n, bm, bn, [], [], "proj_plain", cast)


def _proj_rope(h, w, col0, n, n_scaled, cosf, sinf, bm, bn, cast):
    table = pl.BlockSpec((bm, HEAD_DIM), lambda i, j: (i, 0))
    return _proj_call(functools.partial(_proj_rope_kernel, n_scaled=n_scaled),
                      h, w, col0, n, bm, bn, [cosf, sinf], [table, table], "proj_rope", cast)


def _proj_gate(h, w, col0, n, bias, bm, bn, cast):
    return _proj_call(_proj_gate_kernel, h, w, col0, n, bm, bn, [bias.reshape(1, n)],
                      [pl.BlockSpec((1, bn), lambda i, j: (0, j))], "proj_gate", cast)


def _vt_proj_kernel(w_ref, h_ref, o_ref, wt_ref, *, tk):
    @pl.when(pl.program_id(1) == 0)
    def _():
        step = wt_ref.shape[0]
        for c in range(w_ref.shape[0] // step):
            rows = slice(c * step, (c + 1) * step)
            wt_ref[:, rows] = w_ref[rows, :].astype(F32).T.astype(wt_ref.dtype)

    r = _nt_dot(wt_ref[...], h_ref[...])
    for t in range(o_ref.shape[0]):
        o_ref[t] = r[:, t * tk:(t + 1) * tk].astype(o_ref.dtype)


def _vt_proj(h, w, col0, n, bm, bn, tk, cast=None):
    m, d = h.shape
    b0 = col0 // bn
    return _hosted_call(
        functools.partial(_vt_proj_kernel, tk=tk),
        (n // bn, m // bm),
        [
            pl.BlockSpec((d, bn), lambda j, i: (0, b0 + j)),
            pl.BlockSpec((bm, d), lambda j, i: (i, 0)),
        ],
        pl.BlockSpec((bm // tk, bn, tk), lambda j, i: (i, j, 0)),
        jax.ShapeDtypeStruct((m // tk, n, tk), BF16),
        (w, h), "proj_vt", scratch=[pltpu.VMEM((bn, d), BF16)], cast=cast)


def _forget_kernel(h_ref, w_ref, b_ref, e_ref, o_ref, carry_ref):
    @pl.when(pl.program_id(1) == 0)
    def _():
        carry_ref[...] = jnp.zeros_like(carry_ref)

    ff = jnp.dot(h_ref[...], w_ref[...], preferred_element_type=F32) + b_ref[...]
    logf = jnp.minimum(ff, 0.0) - jnp.log1p(jnp.exp(-jnp.abs(ff)))
    tm = logf.shape[0]
    row = lax.broadcasted_iota(jnp.int32, (tm, tm), 0)
    col = lax.broadcasted_iota(jnp.int32, (tm, tm), 1)
    tri = (row >= col).astype(F32)
    cum = jnp.dot(tri, logf, preferred_element_type=F32,
                  precision=lax.Precision.HIGHEST) + carry_ref[...]
    carry_ref[...] = cum[tm - 1:tm, :]

    rest = cum * (-LOG2E)
    pieces = []
    for _ in range(N_SPLIT):
        p = rest.astype(BF16)
        pieces.append(p)
        rest = rest - p.astype(F32)
    spread = jnp.dot(jnp.concatenate(pieces, axis=1), e_ref[...], preferred_element_type=F32)
    o_ref[...] = spread.astype(o_ref.dtype)


def _forget(h, w_ff, b_ff, n_heads, batch, seq):
    m, d = h.shape
    tm = _tile(seq, 256)
    per_b = seq // tm
    route = np.zeros((N_SPLIT * LANE, n_heads * HEAD_DIM), np.float32)
    for p in range(N_SPLIT):
        for hh in range(n_heads):
            route[p * LANE + hh, hh * HEAD_DIM + p] = 1.0
    return pl.pallas_call(
        _forget_kernel,
        grid=(batch, per_b),
        in_specs=[
            pl.BlockSpec((tm, d), lambda b, i: (b * per_b + i, 0)),
            pl.BlockSpec((d, LANE), lambda b, i: (0, 0)),
            pl.BlockSpec((1, LANE), lambda b, i: (0, 0)),
            pl.BlockSpec(route.shape, lambda b, i: (0, 0)),
        ],
        out_specs=pl.BlockSpec((tm, n_heads * HEAD_DIM), lambda b, i: (b * per_b + i, 0)),
        out_shape=jax.ShapeDtypeStruct((m, n_heads * HEAD_DIM), BF16),
        scratch_shapes=[pltpu.VMEM((1, LANE), F32)],
        compiler_params=_params("arbitrary", "arbitrary"),
        name="forget_cumsum",
    )(h, w_ff, b_ff, jnp.asarray(route, BF16))


def _online_softmax_step(st, vt, m, l, acc):
    m_new = jnp.maximum(m, jnp.max(st, axis=0, keepdims=True))
    alpha = jnp.exp2(m - m_new)
    p = jnp.exp2(st - m_new)
    l_new = alpha * l + jnp.sum(p, axis=0, keepdims=True)
    acc_new = alpha * acc + jnp.dot(vt, p.astype(vt.dtype), preferred_element_type=F32)
    return m_new, l_new, acc_new


def _softmax_init(tq, dv):
    return (jnp.full((1, tq), MASKED, F32), jnp.zeros((1, tq), F32), jnp.zeros((dv, tq), F32))


def _fox_kernel(q_ref, k_ref, kx_ref, vt_ref, o_ref, *, tq, nh):
    qi = pl.program_id(2)
    lane = lax.broadcasted_iota(jnp.int32, (tq, HEAD_DIM), 1)
    ones = (lane < N_SPLIT).astype(q_ref.dtype)

    def head(a):
        return slice(a * HEAD_DIM, (a + 1) * HEAD_DIM)

    def block(kj, carry, diagonal):
        ks = pl.multiple_of(kj * tq, tq)
        sts = []
        for a in range(nh):
            q_aug = jnp.concatenate([q_ref[:, head(a)], ones], axis=1)
            k_aug = jnp.concatenate([k_ref[pl.ds(ks, tq), head(a)],
                                     kx_ref[pl.ds(ks, tq), head(a)]], axis=1)
            st = _nt_dot(k_aug, q_aug)
            if diagonal:
                key = lax.broadcasted_iota(jnp.int32, st.shape, 0)
                qry = lax.broadcasted_iota(jnp.int32, st.shape, 1)
                st = jnp.where(key <= qry, st, MASKED)
            sts.append(st)
        out = ()
        for a in range(nh):
            out += _online_softmax_step(sts[a], vt_ref[kj, head(a), :], *carry[3 * a:3 * a + 3])
        return out

    carry = lax.fori_loop(0, qi, lambda kj, c: block(kj, c, False),
                          _softmax_init(tq, HEAD_DIM) * nh)
    carry = block(qi, carry, True)
    for a in range(nh):
        _, l, acc = carry[3 * a:3 * a + 3]
        o_ref[:, head(a)] = (acc / l).T.astype(o_ref.dtype)


def _fox(qk, kx, vt, batch, seq, n_heads, tq, nh):
    m = qk.shape[0]
    nq = seq // tq
    gw = nh * HEAD_DIM
    ng = n_heads // nh
    return pl.pallas_call(
        functools.partial(_fox_kernel, tq=tq, nh=nh),
        grid=(batch, ng, nq),
        in_specs=[
            pl.BlockSpec((tq, gw), lambda b, g, i: (b * nq + i, g)),
            pl.BlockSpec((seq, gw), lambda b, g, i: (b, ng + g)),
            pl.BlockSpec((seq, gw), lambda b, g, i: (b, g)),
            pl.BlockSpec((nq, gw, tq), lambda b, g, i: (b, g, 0)),
        ],
        out_specs=pl.BlockSpec((tq, gw), lambda b, g, i: (b * nq + i, g)),
        out_shape=jax.ShapeDtypeStruct((m, n_heads * HEAD_DIM), BF16),
        compiler_params=_params("arbitrary", "arbitrary", "arbitrary"),
        name="fox_attention",
    )(qk, qk, kx, vt)


def _diff_kernel(q_ref, k_ref, vt_ref, lam_ref, g_ref, o_ref, *, tq, nh, lambda_init):
    qi = pl.program_id(2)
    n_maps = 2 * nh

    def cols(c):
        return slice(c * HEAD_DIM, (c + 1) * HEAD_DIM)

    def block(kj, carry, diagonal):
        ks = pl.multiple_of(kj * tq, tq)
        sts = []
        for c in range(n_maps):
            st = _nt_dot(k_ref[pl.ds(ks, tq), cols(c)], q_ref[:, cols(c)])
            if diagonal:
                key = lax.broadcasted_iota(jnp.int32, st.shape, 0) // CHUNK
                qry = lax.broadcasted_iota(jnp.int32, st.shape, 1) // CHUNK
                st = jnp.where(key <= qry, st, MASKED)
            sts.append(st)
        out = ()
        for c in range(n_maps):
            vt = vt_ref[kj, (c // 2) * 2 * HEAD_DIM:(c // 2 + 1) * 2 * HEAD_DIM, :]
            out += _online_softmax_step(sts[c], vt, *carry[3 * c:3 * c + 3])
        return out

    carry = lax.fori_loop(0, qi, lambda kj, c: block(kj, c, False),
                          _softmax_init(tq, 2 * HEAD_DIM) * n_maps)
    carry = block(qi, carry, True)

    lp = lam_ref[...]
    lam = (jnp.exp(jnp.sum(lp[0:1] * lp[1:2], axis=-1, keepdims=True))
           - jnp.exp(jnp.sum(lp[2:3] * lp[3:4], axis=-1, keepdims=True)) + lambda_init)
    for a in range(nh):
        _, l1, a1, _, l2, a2 = carry[6 * a:6 * a + 6]
        y = a1 / l1 - lam * (a2 / l2)
        ms = jnp.mean(y * y, axis=0, keepdims=True)
        y = (y * lax.rsqrt(ms + EPS)).T * g_ref[...]
        o_ref[:, a * 2 * HEAD_DIM:(a + 1) * 2 * HEAD_DIM] = (y * (1.0 - lambda_init)).astype(o_ref.dtype)


def _diff(qk, vt, lam_p, subln, batch, seq, n_heads, tq, nh, lambda_init):
    m = qk.shape[0]
    nq = seq // tq
    hw = 2 * HEAD_DIM
    gw = nh * hw
    ng = n_heads // nh
    return pl.pallas_call(
        functools.partial(_diff_kernel, tq=tq, nh=nh, lambda_init=lambda_init),
        grid=(batch, ng, nq),
        in_specs=[
            pl.BlockSpec((tq, gw), lambda b, g, i: (b * nq + i, g)),
            pl.BlockSpec((seq, gw), lambda b, g, i: (b, ng + g)),
            pl.BlockSpec((nq, gw, tq), lambda b, g, i: (b, g, 0)),
            pl.BlockSpec((4, HEAD_DIM), lambda b, g, i: (0, 0)),
            pl.BlockSpec((1, hw), lambda b, g, i: (0, 0)),
        ],
        out_specs=pl.BlockSpec((tq, gw), lambda b, g, i: (b * nq + i, g)),
        out_shape=jax.ShapeDtypeStruct((m, n_heads * hw), BF16),
        compiler_params=_params("arbitrary", "arbitrary", "arbitrary"),
        name="diff_attention",
    )(qk, qk, vt, lam_p, subln.reshape(1, hw))


def _merge_kernel(ya_ref, yb_ref, wa_ref, wb_ref, sa_ref, sb_ref, o_ref):
    pa = jnp.dot(ya_ref[...], wa_ref[...], preferred_element_type=F32)
    pb = jnp.dot(yb_ref[...], wb_ref[...], preferred_element_type=F32)
    o_ref[...] = (sa_ref[...].astype(F32) * pa + sb_ref[...].astype(F32) * pb).astype(o_ref.dtype)


def _merge(ya, yb, wa, wb, gates, bm, bn):
    m = ya.shape[0]
    d = wa.shape[1]
    nb = d // bn
    return pl.pallas_call(
        _merge_kernel,
        grid=(m // bm, nb),
        in_specs=[
            pl.BlockSpec((bm, ya.shape[1]), lambda i, j: (i, 0)),
            pl.BlockSpec((bm, yb.shape[1]), lambda i, j: (i, 0)),
            pl.BlockSpec((wa.shape[0], bn), lambda i, j: (0, j)),
            pl.BlockSpec((wb.shape[0], bn), lambda i, j: (0, j)),
            pl.BlockSpec((bm, bn), lambda i, j: (i, j)),
            pl.BlockSpec((bm, bn), lambda i, j: (i, nb + j)),
        ],
        out_specs=pl.BlockSpec((bm, bn), lambda i, j: (i, j)),
        out_shape=jax.ShapeDtypeStruct((m, d), BF16),
        compiler_params=_params("arbitrary", "arbitrary"),
        name="gated_merge",
    )(ya, yb, wa, wb, gates, gates)


def _ffn_width(f):
    return _round_up(f, 1024) if f >= 1024 else f


def _ffn_in_cast(w_in, layer):
    f = w_in.shape[2] // 2
    return _StripCast(w_in, layer, w_in.shape[1], ((0, f, _ffn_width(f)), (f, f, _ffn_width(f))))


def _ffn(x2, gain, wg, wu, w_out, layer, mod3, j0, seq, next_cast):
    m, d = x2.shape
    fp = wg.shape[1]
    bm = _tile(seq, 1024)
    h = _norm_mod(x2, gain, mod3, N_MOD, j0, j0 + 1, seq, BF16)
    a, (wo,) = _ffn_in(h, wg, wu, bm, _tile(fp, 512), _StripCast(w_out, layer, fp, ((0, d, d),)))
    bk = fp // 4 if (fp // 4) % LANE == 0 else fp
    return _res_matmul(a, wo, x2, mod3, j0 + 2, 0.5, seq, bm, _tile(d, 1024), bk, "ffn_out",
                       cast=next_cast)


def kernel(x, c, positions, ada_w, ada_b, norm_ffn1, ffn1_w_in, ffn1_w_out, norm_mix, w_in, b_forget, b_gate, diff_lambda, diff_subln, w_o_fox, w_o_diff, w_out, norm_ffn2, ffn2_w_in, ffn2_w_out, final_ada_w, final_ada_b, norm_final):
    batch, seq, d = x.shape
    depth = ada_w.shape[0]
    m = batch * seq
    n_fox = b_forget.shape[1]
    fox_w = n_fox * HEAD_DIM
    diff_w = w_o_diff.shape[1]
    n_diff = diff_w // (2 * HEAD_DIM)
    assert w_in.shape[2] == 3 * fox_w + n_fox + 3 * diff_w + 2 * d
    assert n_fox < LANE and seq % CHUNK == 0

    c_pad = jnp.pad(c, ((0, -batch % 8), (0, 0)))
    x2 = x.reshape(m, d)

    inv_freq = ROPE_THETA ** (-jnp.arange(0, ROPE_DIM, 2, dtype=F32) / ROPE_DIM)
    ang = positions.astype(F32).reshape(m, 1) * inv_freq
    cos, sin = jnp.cos(ang), jnp.sin(ang)
    cosf = jnp.concatenate([cos, cos, jnp.ones((m, HEAD_DIM - ROPE_DIM), F32)], axis=-1)
    sinf = jnp.concatenate([-sin, sin, jnp.zeros((m, HEAD_DIM - ROPE_DIM), F32)], axis=-1)

    bm = _tile(seq, 1024)
    tq = _tile(seq, 512)
    bn_p = _tile(math.gcd(fox_w, diff_w), 1024)
    bn_v = _tile(math.gcd(fox_w, diff_w), 512)
    for l in range(depth):
        lambda_init = 0.8 - 0.6 * math.exp(-0.3 * l)
        mod = _ada(c_pad, ada_w[l], ada_b[l])[:batch]
        mod3 = mod.reshape(batch * N_MOD, 1, d)

        wg, wu = _run_cast(_ffn_in_cast(ffn1_w_in, l))
        o_ff = 3 * fox_w
        n_rest = 3 * diff_w + 2 * d
        mixer_cast = _StripCast(w_in, l, d, ((0, o_ff, o_ff), (o_ff + n_fox, n_rest, n_rest)))
        x2, (w_front, w_rest) = _ffn(x2, norm_ffn1[l], wg, wu, ffn1_w_out, l, mod3, 0, seq, mixer_cast)

        h = _norm_mod(x2, norm_mix[l], mod3, N_MOD, 3, 4, seq, BF16)
        w_ff = jnp.pad(w_in[l, :, o_ff:o_ff + n_fox].astype(BF16), ((0, 0), (0, LANE - n_fox)))
        b_ff = jnp.pad(b_forget[l].astype(F32), (0, LANE - n_fox)).reshape(1, LANE)

        qk_f, (wa,) = _proj_plain(h, w_front, 0, 2 * fox_w, fox_w // bn_p, bm, bn_p,
                                  _StripCast(w_o_fox, l, fox_w, ((0, d, d),)))
        qk_d, (wb,) = _proj_rope(h, w_rest, 0, 2 * diff_w, diff_w // bn_p, cosf, sinf, bm, bn_p,
                                 _StripCast(w_o_diff, l, diff_w, ((0, d, d),)))
        gates, (wg, wu) = _proj_gate(h, w_rest, 3 * diff_w, 2 * d, b_gate[l], bm, bn_p,
                                     _ffn_in_cast(ffn2_w_in, l))
        vt_f, (wo,) = _vt_proj(h, w_front, 2 * fox_w, fox_w, bm, bn_v, tq,
                               _StripCast(w_out, l, d, ((0, d, d),)))
        vt_d, _ = _vt_proj(h, w_rest, 2 * diff_w, diff_w, bm, bn_v, tq)
        kx = _forget(h, w_ff, b_ff, n_fox, batch, seq)

        ya = _fox(qk_f, kx, vt_f, batch, seq, n_fox, tq, math.gcd(n_fox, FOX_HEADS_PER_STEP))
        yb = _diff(qk_d, vt_d, diff_lambda[l].astype(F32), diff_subln[l], batch, seq, n_diff,
                   tq, math.gcd(n_diff, DIFF_HEADS_PER_STEP), lambda_init)
        merged = _merge(ya, yb, wa, wb, gates, bm, _tile(d, 512))
        x2, _ = _res_matmul(merged, wo, x2, mod3, 5, 1.0, seq, bm, _tile(d, 512), d, "mixer_out")

        x2, _ = _ffn(x2, norm_ffn2[l], wg, wu, ffn2_w_out, l, mod3, 6, seq, None)

    fmod = _ada(c_pad, final_ada_w, final_ada_b)[:batch]
    out = _norm_mod(x2, norm_final, fmod.reshape(batch * 2, 1, d), 2, 0, 1, seq, F32)
    return out.reshape(batch, seq, d)
```

```python
import functools
import math
from typing import NamedTuple

import numpy as np
import jax
import jax.numpy as jnp
from jax import lax
from jax.experimental import pallas as pl
from jax.experimental.pallas import tpu as pltpu

HEAD_DIM = 128
CHUNK = 64
ROPE_THETA = 500000.0
ROPE_DIM = HEAD_DIM // 4
ROPE_HALF = ROPE_DIM // 2
EPS = 1e-6
N_MOD = 9
LOG2E = 1.4426950408889634
MASKED = -1e30
LANE = 128
VMEM_LIMIT = 56 * 1024 * 1024
BF16_ROWS = 16
CAST_BLOCK_ELEMS = 1024 * 1024
HOSTED_STRIP_ELEMS = 1024 * 1024
N_SPLIT = 3
FOX_HEADS_PER_STEP = 4
DIFF_HEADS_PER_STEP = 2

F32 = jnp.float32
BF16 = jnp.bfloat16


def _params(*sem):
    return pltpu.CompilerParams(dimension_semantics=sem, vmem_limit_bytes=VMEM_LIMIT)


def _tile(n, pref):
    t = min(n, pref)
    while n % t:
        t //= 2
    return t


def _round_up(n, m):
    return (n + m - 1) // m * m


def _nt_dot(a, b):
    return lax.dot_general(a, b, (((1,), (1,)), ((), ())), preferred_element_type=F32)


class _StripCast(NamedTuple):
    src: jax.Array
    layer: int
    row0: int
    rows: int
    rows_out: int
    pieces: tuple


def _whole(src, layer):
    _, r, c = src.shape
    return _StripCast(src, layer, 0, r, r, ((0, c, c),))


def _strip_rows(cast, lo, hi):
    g = math.gcd(math.gcd(cast.row0, cast.rows), cast.rows_out)
    for rs in range(_round_up(max(lo, 1), BF16_ROWS), min(hi, g) + 1, BF16_ROWS):
        if g % rs == 0:
            return rs
    return None


def _linear_step(grid, ids):
    t = ids[0]
    for g, i in zip(grid[1:], ids[1:]):
        t = t * g + i
    return t


def _strip_specs(cast, rs, grid):
    first, n_valid, n_strips = cast.row0 // rs, cast.rows // rs, cast.rows_out // rs
    cols = cast.src.shape[2]
    src_spec = pl.BlockSpec(
        (None, rs, cols),
        lambda *ids: (cast.layer, first + jnp.minimum(_linear_step(grid, ids), n_valid - 1), 0))
    out_specs = [pl.BlockSpec((rs, nco), lambda *ids: (jnp.minimum(_linear_step(grid, ids), n_strips - 1), 0))
                 for _, _, nco in cast.pieces]
    out_shapes = [jax.ShapeDtypeStruct((cast.rows_out, nco), BF16) for _, _, nco in cast.pieces]
    return src_spec, out_specs, out_shapes


def _strip_cast_body(cast, rs, grid, src_ref, dst_refs):
    t = _linear_step(grid, [pl.program_id(a) for a in range(len(grid))])
    strip = src_ref[...]
    for (c0, n, nco), dst in zip(cast.pieces, dst_refs):
        piece = strip[:, c0:c0 + n].astype(dst.dtype)
        if cast.rows < cast.rows_out:
            piece = jnp.where(t < cast.rows // rs, piece, jnp.zeros_like(piece))
        if nco > n:
            dst[:, :n] = piece
            dst[:, n:] = jnp.zeros((rs, nco - n), dst.dtype)
        else:
            dst[...] = piece


def _run_cast(cast):
    rs = _strip_rows(cast, 1, max(CAST_BLOCK_ELEMS // cast.src.shape[2], BF16_ROWS))
    rs = rs or _strip_rows(cast, 1, cast.rows_out)
    if rs is None:
        w = cast.src[cast.layer, cast.row0:cast.row0 + cast.rows]
        return tuple(jnp.pad(w[:, c0:c0 + n].astype(BF16), ((0, cast.rows_out - cast.rows), (0, nco - n)))
                     for c0, n, nco in cast.pieces)
    grid = (cast.rows_out // rs,)
    src_spec, out_specs, out_shapes = _strip_specs(cast, rs, grid)

    def body(src_ref, *dst_refs):
        _strip_cast_body(cast, rs, grid, src_ref, dst_refs)

    return tuple(pl.pallas_call(
        body, grid=grid, in_specs=[src_spec], out_specs=out_specs, out_shape=out_shapes,
        compiler_params=_params("arbitrary"), name="weight_cast",
    )(cast.src))


def _hosted_call(body, grid, in_specs, out_spec, out_shape, inputs, name, scratch=(), casts=()):
    steps = math.prod(grid)
    hosted, strip_rows = [], []
    for cast in casts:
        rs = _strip_rows(cast, pl.cdiv(cast.rows_out, steps), cast.rows_out)
        if rs is not None and rs * cast.src.shape[2] <= HOSTED_STRIP_ELEMS:
            hosted.append(cast)
            strip_rows.append(rs)
    specs = [_strip_specs(cast, rs, grid) for cast, rs in zip(hosted, strip_rows)]
    n_in, n_src = len(in_specs), len(hosted)

    def kernel_body(*refs):
        ins, src_refs, o_ref = refs[:n_in], refs[n_in:n_in + n_src], refs[n_in + n_src]
        pos = n_in + n_src + 1
        dst_groups = []
        for cast in hosted:
            dst_groups.append(refs[pos:pos + len(cast.pieces)])
            pos += len(cast.pieces)
        body(*ins, o_ref, *refs[pos:])
        for cast, rs, src_ref, dst_refs in zip(hosted, strip_rows, src_refs, dst_groups):
            _strip_cast_body(cast, rs, grid, src_ref, dst_refs)

    outs = pl.pallas_call(
        kernel_body, grid=grid,
        in_specs=list(in_specs) + [s[0] for s in specs],
        out_specs=[out_spec] + [o for s in specs for o in s[1]],
        out_shape=[out_shape] + [o for s in specs for o in s[2]],
        scratch_shapes=list(scratch),
        compiler_params=_params(*(("arbitrary",) * len(grid))), name=name,
    )(*inputs, *[cast.src for cast in hosted])

    results, pos = [], 1
    for cast in casts:
        if any(cast is hc for hc in hosted):
            results.append(tuple(outs[pos:pos + len(cast.pieces)]))
            pos += len(cast.pieces)
        else:
            results.append(_run_cast(cast))
    return outs[0], tuple(results)


def _ada_kernel(c_ref, w_ref, b_ref, o_ref):
    c = c_ref[...]
    sc = (c * jax.nn.sigmoid(c)).astype(BF16)
    o_ref[...] = jnp.dot(sc, w_ref[...].astype(BF16), preferred_element_type=F32) + b_ref[...]


def _ada(c_pad, w, b):
    d, n = w.shape
    tn = _tile(n, 512)
    return pl.pallas_call(
        _ada_kernel,
        grid=(n // tn,),
        in_specs=[
            pl.BlockSpec((c_pad.shape[0], d), lambda j: (0, 0)),
            pl.BlockSpec((d, tn), lambda j: (0, j)),
            pl.BlockSpec((1, tn), lambda j: (0, j)),
        ],
        out_specs=pl.BlockSpec((c_pad.shape[0], tn), lambda j: (0, j)),
        out_shape=jax.ShapeDtypeStruct((c_pad.shape[0], n), F32),
        compiler_params=_params("arbitrary"),
        name="ada_matvec",
    )(c_pad, w, b.reshape(1, n))


def _norm_mod_kernel(x_ref, g_ref, sh_ref, sc_ref, o_ref):
    x = x_ref[...]
    ms = jnp.mean(x * x, axis=-1, keepdims=True)
    y = x * lax.rsqrt(ms + EPS) * g_ref[...]
    o_ref[...] = (y * (1.0 + sc_ref[0]) + sh_ref[0]).astype(o_ref.dtype)


def _norm_mod(x2, gain, mod3, n_mod, shift_j, scale_j, seq, out_dtype):
    m, d = x2.shape
    tm = _tile(seq, 256)
    per_b = seq // tm
    return pl.pallas_call(
        _norm_mod_kernel,
        grid=(m // tm,),
        in_specs=[
            pl.BlockSpec((tm, d), lambda i: (i, 0)),
            pl.BlockSpec((1, d), lambda i: (0, 0)),
            pl.BlockSpec((1, 1, d), lambda i: ((i // per_b) * n_mod + shift_j, 0, 0)),
            pl.BlockSpec((1, 1, d), lambda i: ((i // per_b) * n_mod + scale_j, 0, 0)),
        ],
        out_specs=pl.BlockSpec((tm, d), lambda i: (i, 0)),
        out_shape=jax.ShapeDtypeStruct((m, d), out_dtype),
        compiler_params=_params("arbitrary"),
        name="norm_mod",
    )(x2, gain.reshape(1, d), mod3, mod3)


def _ffn_in_kernel(h_ref, wg_ref, wu_ref, o_ref):
    h = h_ref[...]
    g = jnp.dot(h, wg_ref[...], preferred_element_type=F32)
    u = jnp.dot(h, wu_ref[...], preferred_element_type=F32)
    o_ref[...] = (g * jax.nn.sigmoid(g) * u).astype(o_ref.dtype)


def _ffn_in(h, wg, wu, bm, bn, casts):
    m, d = h.shape
    fp = wg.shape[1]
    return _hosted_call(
        _ffn_in_kernel,
        (m // bm, fp // bn),
        [
            pl.BlockSpec((bm, d), lambda i, j: (i, 0)),
            pl.BlockSpec((d, bn), lambda i, j: (0, j)),
            pl.BlockSpec((d, bn), lambda i, j: (0, j)),
        ],
        pl.BlockSpec((bm, bn), lambda i, j: (i, j)),
        jax.ShapeDtypeStruct((m, fp), BF16),
        (h, wg, wu), "ffn_in", casts=casts)


def _res_matmul_kernel(a_ref, w_ref, res_ref, gate_ref, o_ref, acc_ref, *, coef, nk):
    k = pl.program_id(2)

    def partial_product():
        return jnp.dot(a_ref[...], w_ref[...], preferred_element_type=F32)

    def finish(total):
        o_ref[...] = res_ref[...] + (coef * gate_ref[0]) * total

    if nk == 1:
        finish(partial_product())
        return

    @pl.when(k == 0)
    def _():
        acc_ref[...] = partial_product()

    @pl.when((k > 0) & (k < nk - 1))
    def _():
        acc_ref[...] += partial_product()

    @pl.when(k == nk - 1)
    def _():
        finish(acc_ref[...] + partial_product())


def _res_matmul(a, w, res, mod3, gate_j, coef, seq, bm, bn, bk, name, casts=()):
    m, kdim = a.shape
    n = w.shape[1]
    per_b = seq // bm
    return _hosted_call(
        functools.partial(_res_matmul_kernel, coef=coef, nk=kdim // bk),
        (m // bm, n // bn, kdim // bk),
        [
            pl.BlockSpec((bm, bk), lambda i, j, k: (i, k)),
            pl.BlockSpec((bk, bn), lambda i, j, k: (k, j)),
            pl.BlockSpec((bm, bn), lambda i, j, k: (i, j)),
            pl.BlockSpec((1, 1, bn), lambda i, j, k: ((i // per_b) * N_MOD + gate_j, 0, j)),
        ],
        pl.BlockSpec((bm, bn), lambda i, j, k: (i, j)),
        jax.ShapeDtypeStruct((m, n), F32),
        (a, w, res, mod3), name, scratch=[pltpu.VMEM((bm, bn), F32)], casts=casts)


QSCALE = HEAD_DIM ** -0.5 * LOG2E


def _proj_call(body, h, wt, row0, n, bm, bn, extra_in, extra_specs, name, casts):
    m, d = h.shape
    b0 = row0 // bn
    return _hosted_call(
        body,
        (m // bm, n // bn),
        [
            pl.BlockSpec((bm, d), lambda i, j: (i, 0)),
            pl.BlockSpec((bn, d), lambda i, j: (b0 + j, 0)),
        ] + extra_specs,
        pl.BlockSpec((bm, bn), lambda i, j: (i, j)),
        jax.ShapeDtypeStruct((m, n), BF16),
        (h, wt, *extra_in), name, casts=casts)


def _block_scale(n_scaled):
    return jnp.where(pl.program_id(1) < n_scaled, QSCALE, 1.0).astype(F32)


def _proj_plain_kernel(h_ref, wt_ref, o_ref, *, n_scaled):
    acc = _nt_dot(h_ref[...], wt_ref[...])
    o_ref[...] = (acc * _block_scale(n_scaled)).astype(o_ref.dtype)


def _proj_rope_kernel(h_ref, wt_ref, cos_ref, sin_ref, o_ref, *, n_scaled):
    acc = _nt_dot(h_ref[...], wt_ref[...])
    cosf = cos_ref[...] * _block_scale(n_scaled)
    sinf = sin_ref[...] * _block_scale(n_scaled)
    lane = lax.broadcasted_iota(jnp.int32, cosf.shape, 1)
    for c in range(acc.shape[1] // HEAD_DIM):
        cols = slice(c * HEAD_DIM, (c + 1) * HEAD_DIM)
        t = acc[:, cols]
        partner = jnp.where(lane < ROPE_HALF,
                            pltpu.roll(t, HEAD_DIM - ROPE_HALF, 1),
                            pltpu.roll(t, ROPE_HALF, 1))
        o_ref[:, cols] = (t * cosf + partner * sinf).astype(o_ref.dtype)


def _proj_gate_kernel(h_ref, wt_ref, b_ref, o_ref):
    acc = _nt_dot(h_ref[...], wt_ref[...])
    o_ref[...] = jax.nn.sigmoid(acc + b_ref[...]).astype(o_ref.dtype)


def _proj_plain(h, wt, row0, n, n_scaled, bm, bn, casts):
    return _proj_call(functools.partial(_proj_plain_kernel, n_scaled=n_scaled),
                      h, wt, row0, n, bm, bn, [], [], "proj_plain", casts)


def _proj_rope(h, wt, row0, n, n_scaled, cosf, sinf, bm, bn, casts):
    table = pl.BlockSpec((bm, HEAD_DIM), lambda i, j: (i, 0))
    return _proj_call(functools.partial(_proj_rope_kernel, n_scaled=n_scaled),
                      h, wt, row0, n, bm, bn, [cosf, sinf], [table, table], "proj_rope", casts)


def _proj_gate(h, wt, row0, n, bias, bm, bn, casts):
    return _proj_call(_proj_gate_kernel, h, wt, row0, n, bm, bn, [bias.reshape(1, n)],
                      [pl.BlockSpec((1, bn), lambda i, j: (0, j))], "proj_gate", casts)


def _vt_proj_kernel(wt_ref, h_ref, o_ref, *, tk):
    r = _nt_dot(wt_ref[...], h_ref[...])
    for t in range(o_ref.shape[0]):
        o_ref[t] = r[:, t * tk:(t + 1) * tk].astype(o_ref.dtype)


def _vt_proj(h, wt, row0, n, bm, bn, tk, casts=()):
    m, d = h.shape
    b0 = row0 // bn
    return _hosted_call(
        functools.partial(_vt_proj_kernel, tk=tk),
        (m // bm, n // bn),
        [
            pl.BlockSpec((bn, d), lambda i, j: (b0 + j, 0)),
            pl.BlockSpec((bm, d), lambda i, j: (i, 0)),
        ],
        pl.BlockSpec((bm // tk, bn, tk), lambda i, j: (i, j, 0)),
        jax.ShapeDtypeStruct((m // tk, n, tk), BF16),
        (wt, h), "proj_vt", casts=casts)


def _forget_kernel(h_ref, w_ref, b_ref, e_ref, o_ref, carry_ref):
    @pl.when(pl.program_id(1) == 0)
    def _():
        carry_ref[...] = jnp.zeros_like(carry_ref)

    ff = _nt_dot(h_ref[...], w_ref[...]) + b_ref[...]
    logf = jnp.minimum(ff, 0.0) - jnp.log1p(jnp.exp(-jnp.abs(ff)))
    tm = logf.shape[0]
    row = lax.broadcasted_iota(jnp.int32, (tm, tm), 0)
    col = lax.broadcasted_iota(jnp.int32, (tm, tm), 1)
    tri = (row >= col).astype(F32)
    cum = jnp.dot(tri, logf, preferred_element_type=F32,
                  precision=lax.Precision.HIGHEST) + carry_ref[...]
    carry_ref[...] = cum[tm - 1:tm, :]

    rest = cum * (-LOG2E)
    pieces = []
    for _ in range(N_SPLIT):
        p = rest.astype(BF16)
        pieces.append(p)
        rest = rest - p.astype(F32)
    spread = jnp.dot(jnp.concatenate(pieces, axis=1), e_ref[...], preferred_element_type=F32)
    o_ref[...] = spread.astype(o_ref.dtype)


def _forget(h, w_ff_t, b_ff, n_heads, batch, seq):
    m, d = h.shape
    tm = _tile(seq, 256)
    per_b = seq // tm
    route = np.zeros((N_SPLIT * LANE, n_heads * HEAD_DIM), np.float32)
    for p in range(N_SPLIT):
        for hh in range(n_heads):
            route[p * LANE + hh, hh * HEAD_DIM + p] = 1.0
    return pl.pallas_call(
        _forget_kernel,
        grid=(batch, per_b),
        in_specs=[
            pl.BlockSpec((tm, d), lambda b, i: (b * per_b + i, 0)),
            pl.BlockSpec((LANE, d), lambda b, i: (0, 0)),
            pl.BlockSpec((1, LANE), lambda b, i: (0, 0)),
            pl.BlockSpec(route.shape, lambda b, i: (0, 0)),
        ],
        out_specs=pl.BlockSpec((tm, n_heads * HEAD_DIM), lambda b, i: (b * per_b + i, 0)),
        out_shape=jax.ShapeDtypeStruct((m, n_heads * HEAD_DIM), BF16),
        scratch_shapes=[pltpu.VMEM((1, LANE), F32)],
        compiler_params=_params("arbitrary", "arbitrary"),
        name="forget_cumsum",
    )(h, w_ff_t, b_ff, jnp.asarray(route, BF16))


def _online_softmax_step(st, vt, m, l, acc):
    m_new = jnp.maximum(m, jnp.max(st, axis=0, keepdims=True))
    alpha = jnp.exp2(m - m_new)
    p = jnp.exp2(st - m_new)
    l_new = alpha * l + jnp.sum(p, axis=0, keepdims=True)
    acc_new = alpha * acc + jnp.dot(vt, p.astype(vt.dtype), preferred_element_type=F32)
    return m_new, l_new, acc_new


def _softmax_init(tq, dv):
    return (jnp.full((1, tq), MASKED, F32), jnp.zeros((1, tq), F32), jnp.zeros((dv, tq), F32))


def _fox_kernel(q_ref, k_ref, kx_ref, vt_ref, o_ref, *, tq, nh):
    qi = pl.program_id(2)
    lane = lax.broadcasted_iota(jnp.int32, (tq, HEAD_DIM), 1)
    ones = (lane < N_SPLIT).astype(q_ref.dtype)

    def head(a):
        return slice(a * HEAD_DIM, (a + 1) * HEAD_DIM)

    def block(kj, carry, diagonal):
        ks = pl.multiple_of(kj * tq, tq)
        sts = []
        for a in range(nh):
            q_aug = jnp.concatenate([q_ref[:, head(a)], ones], axis=1)
            k_aug = jnp.concatenate([k_ref[pl.ds(ks, tq), head(a)],
                                     kx_ref[pl.ds(ks, tq), head(a)]], axis=1)
            st = _nt_dot(k_aug, q_aug)
            if diagonal:
                key = lax.broadcasted_iota(jnp.int32, st.shape, 0)
                qry = lax.broadcasted_iota(jnp.int32, st.shape, 1)
                st = jnp.where(key <= qry, st, MASKED)
            sts.append(st)
        out = ()
        for a in range(nh):
            out += _online_softmax_step(sts[a], vt_ref[kj, head(a), :], *carry[3 * a:3 * a + 3])
        return out

    carry = lax.fori_loop(0, qi, lambda kj, c: block(kj, c, False),
                          _softmax_init(tq, HEAD_DIM) * nh)
    carry = block(qi, carry, True)
    for a in range(nh):
        _, l, acc = carry[3 * a:3 * a + 3]
        o_ref[:, head(a)] = (acc / l).T.astype(o_ref.dtype)


def _fox(qk, kx, vt, batch, seq, n_heads, tq, nh):
    m = qk.shape[0]
    nq = seq // tq
    gw = nh * HEAD_DIM
    ng = n_heads // nh
    return pl.pallas_call(
        functools.partial(_fox_kernel, tq=tq, nh=nh),
        grid=(batch, ng, nq),
        in_specs=[
            pl.BlockSpec((tq, gw), lambda b, g, i: (b * nq + i, g)),
            pl.BlockSpec((seq, gw), lambda b, g, i: (b, ng + g)),
            pl.BlockSpec((seq, gw), lambda b, g, i: (b, g)),
            pl.BlockSpec((nq, gw, tq), lambda b, g, i: (b, g, 0)),
        ],
        out_specs=pl.BlockSpec((tq, gw), lambda b, g, i: (b * nq + i, g)),
        out_shape=jax.ShapeDtypeStruct((m, n_heads * HEAD_DIM), BF16),
        compiler_params=_params("arbitrary", "arbitrary", "arbitrary"),
        name="fox_attention",
    )(qk, qk, kx, vt)


def _diff_kernel(q_ref, k_ref, vt_ref, lam_ref, g_ref, o_ref, *, tq, nh, lambda_init):
    qi = pl.program_id(2)
    n_maps = 2 * nh

    def cols(c):
        return slice(c * HEAD_DIM, (c + 1) * HEAD_DIM)

    def block(kj, carry, diagonal):
        ks = pl.multiple_of(kj * tq, tq)
        sts = []
        for c in range(n_maps):
            st = _nt_dot(k_ref[pl.ds(ks, tq), cols(c)], q_ref[:, cols(c)])
            if diagonal:
                key = lax.broadcasted_iota(jnp.int32, st.shape, 0) // CHUNK
                qry = lax.broadcasted_iota(jnp.int32, st.shape, 1) // CHUNK
                st = jnp.where(key <= qry, st, MASKED)
            sts.append(st)
        out = ()
        for c in range(n_maps):
            vt = vt_ref[kj, (c // 2) * 2 * HEAD_DIM:(c // 2 + 1) * 2 * HEAD_DIM, :]
            out += _online_softmax_step(sts[c], vt, *carry[3 * c:3 * c + 3])
        return out

    carry = lax.fori_loop(0, qi, lambda kj, c: block(kj, c, False),
                          _softmax_init(tq, 2 * HEAD_DIM) * n_maps)
    carry = block(qi, carry, True)

    lp = lam_ref[...]
    lam = (jnp.exp(jnp.sum(lp[0:1] * lp[1:2], axis=-1, keepdims=True))
           - jnp.exp(jnp.sum(lp[2:3] * lp[3:4], axis=-1, keepdims=True)) + lambda_init)
    for a in range(nh):
        _, l1, a1, _, l2, a2 = carry[6 * a:6 * a + 6]
        y = a1 / l1 - lam * (a2 / l2)
        ms = jnp.mean(y * y, axis=0, keepdims=True)
        y = (y * lax.rsqrt(ms + EPS)).T * g_ref[...]
        o_ref[:, a * 2 * HEAD_DIM:(a + 1) * 2 * HEAD_DIM] = (y * (1.0 - lambda_init)).astype(o_ref.dtype)


def _diff(qk, vt, lam_p, subln, batch, seq, n_heads, tq, nh, lambda_init):
    m = qk.shape[0]
    nq = seq // tq
    hw = 2 * HEAD_DIM
    gw = nh * hw
    ng = n_heads // nh
    return pl.pallas_call(
        functools.partial(_diff_kernel, tq=tq, nh=nh, lambda_init=lambda_init),
        grid=(batch, ng, nq),
        in_specs=[
            pl.BlockSpec((tq, gw), lambda b, g, i: (b * nq + i, g)),
            pl.BlockSpec((seq, gw), lambda b, g, i: (b, ng + g)),
            pl.BlockSpec((nq, gw, tq), lambda b, g, i: (b, g, 0)),
            pl.BlockSpec((4, HEAD_DIM), lambda b, g, i: (0, 0)),
            pl.BlockSpec((1, hw), lambda b, g, i: (0, 0)),
        ],
        out_specs=pl.BlockSpec((tq, gw), lambda b, g, i: (b * nq + i, g)),
        out_shape=jax.ShapeDtypeStruct((m, n_heads * hw), BF16),
        compiler_params=_params("arbitrary", "arbitrary", "arbitrary"),
        name="diff_attention",
    )(qk, qk, vt, lam_p, subln.reshape(1, hw))


def _merge_kernel(ya_ref, yb_ref, wa_ref, wb_ref, sa_ref, sb_ref, o_ref):
    pa = jnp.dot(ya_ref[...], wa_ref[...], preferred_element_type=F32)
    pb = jnp.dot(yb_ref[...], wb_ref[...], preferred_element_type=F32)
    o_ref[...] = (sa_ref[...].astype(F32) * pa + sb_ref[...].astype(F32) * pb).astype(o_ref.dtype)


def _merge(ya, yb, wa, wb, gates, bm, bn):
    m = ya.shape[0]
    d = wa.shape[1]
    nb = d // bn
    return pl.pallas_call(
        _merge_kernel,
        grid=(m // bm, nb),
        in_specs=[
            pl.BlockSpec((bm, ya.shape[1]), lambda i, j: (i, 0)),
            pl.BlockSpec((bm, yb.shape[1]), lambda i, j: (i, 0)),
            pl.BlockSpec((wa.shape[0], bn), lambda i, j: (0, j)),
            pl.BlockSpec((wb.shape[0], bn), lambda i, j: (0, j)),
            pl.BlockSpec((bm, bn), lambda i, j: (i, j)),
            pl.BlockSpec((bm, bn), lambda i, j: (i, nb + j)),
        ],
        out_specs=pl.BlockSpec((bm, bn), lambda i, j: (i, j)),
        out_shape=jax.ShapeDtypeStruct((m, d), BF16),
        compiler_params=_params("arbitrary", "arbitrary"),
        name="gated_merge",
    )(ya, yb, wa, wb, gates, gates)


def _ffn_width(f):
    return _round_up(f, 1024) if f >= 1024 else f


def _ffn_in_cast(w_in, layer):
    _, d, f2 = w_in.shape
    f = f2 // 2
    return _StripCast(w_in, layer, 0, d, d, ((0, f, _ffn_width(f)), (f, f, _ffn_width(f))))


def _ffn(x2, gain, wg, wu, w_out, layer, mod3, j0, seq, next_casts):
    m, d = x2.shape
    f, fp = w_out.shape[1], wg.shape[1]
    bm = _tile(seq, 1024)
    h = _norm_mod(x2, gain, mod3, N_MOD, j0, j0 + 1, seq, BF16)
    wo_cast = _StripCast(w_out, layer, 0, f, fp, ((0, d, d),))
    a, ((wo,),) = _ffn_in(h, wg, wu, bm, _tile(fp, 512), (wo_cast,))
    bk = fp // 4 if (fp // 4) % LANE == 0 else fp
    return _res_matmul(a, wo, x2, mod3, j0 + 2, 0.5, seq, bm, _tile(d, 1024), bk, "ffn_out",
                       casts=next_casts)


def kernel(x, c, positions, ada_w, ada_b, norm_ffn1, ffn1_w_in, ffn1_w_out, norm_mix, w_in, b_forget, b_gate, diff_lambda, diff_subln, w_o_fox, w_o_diff, w_out, norm_ffn2, ffn2_w_in, ffn2_w_out, final_ada_w, final_ada_b, norm_final):
    batch, seq, d = x.shape
    depth = ada_w.shape[0]
    m = batch * seq
    n_fox = b_forget.shape[1]
    fox_w = n_fox * HEAD_DIM
    diff_w = w_o_diff.shape[1]
    n_diff = diff_w // (2 * HEAD_DIM)
    o_ff = 3 * fox_w
    o_dq = o_ff + n_fox
    n_rest = 3 * diff_w + 2 * d
    assert w_in.shape[2] == o_dq + n_rest
    assert n_fox <= LANE and seq % CHUNK == 0

    c_pad = jnp.pad(c, ((0, -batch % 8), (0, 0)))
    x2 = x.reshape(m, d)
    w_in_t = jnp.swapaxes(w_in, 1, 2)

    inv_freq = ROPE_THETA ** (-jnp.arange(0, ROPE_DIM, 2, dtype=F32) / ROPE_DIM)
    ang = positions.astype(F32).reshape(m, 1) * inv_freq
    cos, sin = jnp.cos(ang), jnp.sin(ang)
    cosf = jnp.concatenate([cos, cos, jnp.ones((m, HEAD_DIM - ROPE_DIM), F32)], axis=-1)
    sinf = jnp.concatenate([-sin, sin, jnp.zeros((m, HEAD_DIM - ROPE_DIM), F32)], axis=-1)

    bm = _tile(seq, 1024)
    tq = _tile(seq, 512)
    bn_p = _tile(math.gcd(fox_w, diff_w), 1024)
    for l in range(depth):
        lambda_init = 0.8 - 0.6 * math.exp(-0.3 * l)
        mod = _ada(c_pad, ada_w[l], ada_b[l])[:batch]
        mod3 = mod.reshape(batch * N_MOD, 1, d)

        wg, wu = _run_cast(_ffn_in_cast(ffn1_w_in, l))
        mixer_casts = (_StripCast(w_in_t, l, 0, o_ff, o_ff, ((0, d, d),)),
                       _StripCast(w_in_t, l, o_dq, n_rest, n_rest, ((0, d, d),)))
        x2, ((w_front,), (w_rest,)) = _ffn(x2, norm_ffn1[l], wg, wu, ffn1_w_out, l, mod3, 0, seq,
                                           mixer_casts)

        h = _norm_mod(x2, norm_mix[l], mod3, N_MOD, 3, 4, seq, BF16)
        w_ff_t = jnp.pad(w_in_t[l, o_ff:o_dq, :].astype(BF16), ((0, LANE - n_fox), (0, 0)))
        b_ff = jnp.pad(b_forget[l].astype(F32), (0, LANE - n_fox)).reshape(1, LANE)

        qk_f, ((wa,),) = _proj_plain(h, w_front, 0, 2 * fox_w, fox_w // bn_p, bm, bn_p,
                                     (_whole(w_o_fox, l),))
        qk_d, ((wb,),) = _proj_rope(h, w_rest, 0, 2 * diff_w, diff_w // bn_p, cosf, sinf, bm, bn_p,
                                    (_whole(w_o_diff, l),))
        gates, ((wg, wu),) = _proj_gate(h, w_rest, 3 * diff_w, 2 * d, b_gate[l], bm, bn_p,
                                        (_ffn_in_cast(ffn2_w_in, l),))
        vt_f, ((wo,),) = _vt_proj(h, w_front, 2 * fox_w, fox_w, bm, bn_p, tq, (_whole(w_out, l),))
        vt_d, _ = _vt_proj(h, w_rest, 2 * diff_w, diff_w, bm, bn_p, tq)
        kx = _forget(h, w_ff_t, b_ff, n_fox, batch, seq)

        ya = _fox(qk_f, kx, vt_f, batch, seq, n_fox, tq, math.gcd(n_fox, FOX_HEADS_PER_STEP))
        yb = _diff(qk_d, vt_d, diff_lambda[l].astype(F32), diff_subln[l], batch, seq, n_diff,
                   tq, math.gcd(n_diff, DIFF_HEADS_PER_STEP), lambda_init)
        merged = _merge(ya, yb, wa, wb, gates, bm, _tile(d, 512))
        x2, _ = _res_matmul(merged, wo, x2, mod3, 5, 1.0, seq, bm, _tile(d, 512), d, "mixer_out")

        x2, _ = _ffn(x2, norm_ffn2[l], wg, wu, ffn2_w_out, l, mod3, 6, seq, ())

    fmod = _ada(c_pad, final_ada_w, final_ada_b)[:batch]
    out = _norm_mod(x2, norm_final, fmod.reshape(batch * 2, 1, d), 2, 0, 1, seq, F32)
    return out.reshape(batch, seq, d)
```

```python
import functools
import math
from typing import NamedTuple

import numpy as np
import jax
import jax.numpy as jnp
from jax import lax
from jax.experimental import pallas as pl
from jax.experimental.pallas import tpu as pltpu

HEAD_DIM = 128
CHUNK = 64
ROPE_THETA = 500000.0
ROPE_DIM = HEAD_DIM // 4
ROPE_HALF = ROPE_DIM // 2
EPS = 1e-6
N_MOD = 9
LOG2E = 1.4426950408889634
MASKED = -1e30
LANE = 128
VMEM_LIMIT = 56 * 1024 * 1024
BF16_ROWS = 16
CAST_BLOCK_ELEMS = 2 * 1024 * 1024
HOSTED_STRIP_ELEMS = 1024 * 1024
N_SPLIT = 3
FOX_HEADS_PER_STEP = 4
DIFF_HEADS_PER_STEP = 2

F32 = jnp.float32
BF16 = jnp.bfloat16


def _params(*sem):
    return pltpu.CompilerParams(dimension_semantics=sem, vmem_limit_bytes=VMEM_LIMIT)


def _tile(n, pref):
    t = min(n, pref)
    while n % t:
        t //= 2
    return t


def _round_up(n, m):
    return (n + m - 1) // m * m


def _sigmoid(x):
    return 0.5 * jnp.tanh(0.5 * x) + 0.5


def _nt_dot(a, b):
    return lax.dot_general(a, b, (((1,), (1,)), ((), ())), preferred_element_type=F32)


class _StripCast(NamedTuple):
    src: jax.Array
    layer: int
    row0: int
    rows: int
    rows_out: int
    pieces: tuple


def _whole(src, layer):
    _, r, c = src.shape
    return _StripCast(src, layer, 0, r, r, ((0, c, c),))


def _strip_rows(cast, lo, hi):
    g = math.gcd(math.gcd(cast.row0, cast.rows), cast.rows_out)
    for rs in range(_round_up(max(lo, 1), BF16_ROWS), min(hi, g) + 1, BF16_ROWS):
        if g % rs == 0:
            return rs
    return None


def _linear_step(grid, ids):
    t = ids[0]
    for g, i in zip(grid[1:], ids[1:]):
        t = t * g + i
    return t


def _strip_specs(cast, rs, grid):
    first, n_valid, n_strips = cast.row0 // rs, cast.rows // rs, cast.rows_out // rs
    cols = cast.src.shape[2]
    src_spec = pl.BlockSpec(
        (None, rs, cols),
        lambda *ids: (cast.layer, first + jnp.minimum(_linear_step(grid, ids), n_valid - 1), 0))
    out_specs = [pl.BlockSpec((rs, nco), lambda *ids: (jnp.minimum(_linear_step(grid, ids), n_strips - 1), 0))
                 for _, _, nco in cast.pieces]
    out_shapes = [jax.ShapeDtypeStruct((cast.rows_out, nco), BF16) for _, _, nco in cast.pieces]
    return src_spec, out_specs, out_shapes


def _strip_cast_body(cast, rs, grid, src_ref, dst_refs):
    t = _linear_step(grid, [pl.program_id(a) for a in range(len(grid))])
    strip = src_ref[...]
    for (c0, n, nco), dst in zip(cast.pieces, dst_refs):
        piece = strip[:, c0:c0 + n].astype(dst.dtype)
        if cast.rows < cast.rows_out:
            piece = jnp.where(t < cast.rows // rs, piece, jnp.zeros_like(piece))
        if nco > n:
            dst[:, :n] = piece
            dst[:, n:] = jnp.zeros((rs, nco - n), dst.dtype)
        else:
            dst[...] = piece


def _run_cast(cast):
    rs = _strip_rows(cast, 1, max(CAST_BLOCK_ELEMS // cast.src.shape[2], BF16_ROWS))
    rs = rs or _strip_rows(cast, 1, cast.rows_out)
    if rs is None:
        w = cast.src[cast.layer, cast.row0:cast.row0 + cast.rows]
        return tuple(jnp.pad(w[:, c0:c0 + n].astype(BF16), ((0, cast.rows_out - cast.rows), (0, nco - n)))
                     for c0, n, nco in cast.pieces)
    grid = (cast.rows_out // rs,)
    src_spec, out_specs, out_shapes = _strip_specs(cast, rs, grid)

    def body(src_ref, *dst_refs):
        _strip_cast_body(cast, rs, grid, src_ref, dst_refs)

    return tuple(pl.pallas_call(
        body, grid=grid, in_specs=[src_spec], out_specs=out_specs, out_shape=out_shapes,
        compiler_params=_params("arbitrary"), name="weight_cast",
    )(cast.src))


def _hosted_call(body, grid, in_specs, out_spec, out_shape, inputs, name, scratch=(), casts=()):
    steps = math.prod(grid)
    hosted, strip_rows = [], []
    for cast in casts:
        rs = _strip_rows(cast, pl.cdiv(cast.rows_out, steps), cast.rows_out)
        if rs is not None and rs * cast.src.shape[2] <= HOSTED_STRIP_ELEMS:
            hosted.append(cast)
            strip_rows.append(rs)
    specs = [_strip_specs(cast, rs, grid) for cast, rs in zip(hosted, strip_rows)]
    n_in, n_src = len(in_specs), len(hosted)

    def kernel_body(*refs):
        ins, src_refs, o_ref = refs[:n_in], refs[n_in:n_in + n_src], refs[n_in + n_src]
        pos = n_in + n_src + 1
        dst_groups = []
        for cast in hosted:
            dst_groups.append(refs[pos:pos + len(cast.pieces)])
            pos += len(cast.pieces)
        body(*ins, o_ref, *refs[pos:])
        for cast, rs, src_ref, dst_refs in zip(hosted, strip_rows, src_refs, dst_groups):
            _strip_cast_body(cast, rs, grid, src_ref, dst_refs)

    outs = pl.pallas_call(
        kernel_body, grid=grid,
        in_specs=list(in_specs) + [s[0] for s in specs],
        out_specs=[out_spec] + [o for s in specs for o in s[1]],
        out_shape=[out_shape] + [o for s in specs for o in s[2]],
        scratch_shapes=list(scratch),
        compiler_params=_params(*(("arbitrary",) * len(grid))), name=name,
    )(*inputs, *[cast.src for cast in hosted])

    results, pos = [], 1
    for cast in casts:
        if any(cast is hc for hc in hosted):
            results.append(tuple(outs[pos:pos + len(cast.pieces)]))
            pos += len(cast.pieces)
        else:
            results.append(_run_cast(cast))
    return outs[0], tuple(results)


def _ada_kernel(c_ref, w_ref, b_ref, o_ref):
    c = c_ref[...]
    sc = (c * jax.nn.sigmoid(c)).astype(BF16)
    o_ref[...] = jnp.dot(sc, w_ref[...].astype(BF16), preferred_element_type=F32) + b_ref[...]


def _ada(c_pad, w, b):
    d, n = w.shape
    tn = _tile(n, 512)
    return pl.pallas_call(
        _ada_kernel,
        grid=(n // tn,),
        in_specs=[
            pl.BlockSpec((c_pad.shape[0], d), lambda j: (0, 0)),
            pl.BlockSpec((d, tn), lambda j: (0, j)),
            pl.BlockSpec((1, tn), lambda j: (0, j)),
        ],
        out_specs=pl.BlockSpec((c_pad.shape[0], tn), lambda j: (0, j)),
        out_shape=jax.ShapeDtypeStruct((c_pad.shape[0], n), F32),
        compiler_params=_params("arbitrary"),
        name="ada_matvec",
    )(c_pad, w, b.reshape(1, n))


def _norm_mod_kernel(x_ref, g_ref, sh_ref, sc_ref, o_ref):
    x = x_ref[...]
    ms = jnp.mean(x * x, axis=-1, keepdims=True)
    y = x * lax.rsqrt(ms + EPS) * g_ref[...]
    o_ref[...] = (y * (1.0 + sc_ref[0]) + sh_ref[0]).astype(o_ref.dtype)


def _norm_mod(x2, gain, mod3, n_mod, shift_j, scale_j, seq, out_dtype):
    m, d = x2.shape
    tm = _tile(seq, 256)
    per_b = seq // tm
    return pl.pallas_call(
        _norm_mod_kernel,
        grid=(m // tm,),
        in_specs=[
            pl.BlockSpec((tm, d), lambda i: (i, 0)),
            pl.BlockSpec((1, d), lambda i: (0, 0)),
            pl.BlockSpec((1, 1, d), lambda i: ((i // per_b) * n_mod + shift_j, 0, 0)),
            pl.BlockSpec((1, 1, d), lambda i: ((i // per_b) * n_mod + scale_j, 0, 0)),
        ],
        out_specs=pl.BlockSpec((tm, d), lambda i: (i, 0)),
        out_shape=jax.ShapeDtypeStruct((m, d), out_dtype),
        compiler_params=_params("arbitrary"),
        name="norm_mod",
    )(x2, gain.reshape(1, d), mod3, mod3)


def _ffn_in_kernel(h_ref, wg_ref, wu_ref, o_ref):
    h = h_ref[...]
    g = jnp.dot(h, wg_ref[...], preferred_element_type=F32)
    u = jnp.dot(h, wu_ref[...], preferred_element_type=F32)
    o_ref[...] = (g * _sigmoid(g) * u).astype(o_ref.dtype)


def _ffn_in(h, wg, wu, bm, bn, casts):
    m, d = h.shape
    fp = wg.shape[1]
    return _hosted_call(
        _ffn_in_kernel,
        (m // bm, fp // bn),
        [
            pl.BlockSpec((bm, d), lambda i, j: (i, 0)),
            pl.BlockSpec((d, bn), lambda i, j: (0, j)),
            pl.BlockSpec((d, bn), lambda i, j: (0, j)),
        ],
        pl.BlockSpec((bm, bn), lambda i, j: (i, j)),
        jax.ShapeDtypeStruct((m, fp), BF16),
        (h, wg, wu), "ffn_in", casts=casts)


def _res_matmul_kernel(a_ref, w_ref, res_ref, gate_ref, o_ref, acc_ref, *, coef, nk):
    k = pl.program_id(2)

    def partial_product():
        return jnp.dot(a_ref[...], w_ref[...], preferred_element_type=F32)

    def finish(total):
        o_ref[...] = res_ref[...] + (coef * gate_ref[0]) * total

    if nk == 1:
        finish(partial_product())
        return

    @pl.when(k == 0)
    def _():
        acc_ref[...] = partial_product()

    @pl.when((k > 0) & (k < nk - 1))
    def _():
        acc_ref[...] += partial_product()

    @pl.when(k == nk - 1)
    def _():
        finish(acc_ref[...] + partial_product())


def _res_matmul(a, w, res, mod3, gate_j, coef, seq, bm, bn, bk, name, casts=()):
    m, kdim = a.shape
    n = w.shape[1]
    per_b = seq // bm
    return _hosted_call(
        functools.partial(_res_matmul_kernel, coef=coef, nk=kdim // bk),
        (m // bm, n // bn, kdim // bk),
        [
            pl.BlockSpec((bm, bk), lambda i, j, k: (i, k)),
            pl.BlockSpec((bk, bn), lambda i, j, k: (k, j)),
            pl.BlockSpec((bm, bn), lambda i, j, k: (i, j)),
            pl.BlockSpec((1, 1, bn), lambda i, j, k: ((i // per_b) * N_MOD + gate_j, 0, j)),
        ],
        pl.BlockSpec((bm, bn), lambda i, j, k: (i, j)),
        jax.ShapeDtypeStruct((m, n), F32),
        (a, w, res, mod3), name, scratch=[pltpu.VMEM((bm, bn), F32)], casts=casts)


QSCALE = HEAD_DIM ** -0.5 * LOG2E


def _proj_call(body, h, wt, row0, n, bm, bn, extra_in, extra_specs, name, casts):
    m, d = h.shape
    b0 = row0 // bn
    return _hosted_call(
        body,
        (m // bm, n // bn),
        [
            pl.BlockSpec((bm, d), lambda i, j: (i, 0)),
            pl.BlockSpec((bn, d), lambda i, j: (b0 + j, 0)),
        ] + extra_specs,
        pl.BlockSpec((bm, bn), lambda i, j: (i, j)),
        jax.ShapeDtypeStruct((m, n), BF16),
        (h, wt, *extra_in), name, casts=casts)


def _block_scale(n_scaled):
    return jnp.where(pl.program_id(1) < n_scaled, QSCALE, 1.0).astype(F32)


def _proj_plain_kernel(h_ref, wt_ref, o_ref, *, n_scaled):
    acc = _nt_dot(h_ref[...], wt_ref[...])
    o_ref[...] = (acc * _block_scale(n_scaled)).astype(o_ref.dtype)


def _proj_rope_kernel(h_ref, wt_ref, cos_ref, sin_ref, o_ref, *, n_scaled):
    acc = _nt_dot(h_ref[...], wt_ref[...])
    cosf = cos_ref[...] * _block_scale(n_scaled)
    sinf = sin_ref[...] * _block_scale(n_scaled)
    lane = lax.broadcasted_iota(jnp.int32, cosf.shape, 1)
    for c in range(acc.shape[1] // HEAD_DIM):
        cols = slice(c * HEAD_DIM, (c + 1) * HEAD_DIM)
        t = acc[:, cols]
        partner = jnp.where(lane < ROPE_HALF,
                            pltpu.roll(t, HEAD_DIM - ROPE_HALF, 1),
                            pltpu.roll(t, ROPE_HALF, 1))
        o_ref[:, cols] = (t * cosf + partner * sinf).astype(o_ref.dtype)


def _proj_gate_kernel(h_ref, wt_ref, b_ref, o_ref):
    acc = _nt_dot(h_ref[...], wt_ref[...])
    o_ref[...] = _sigmoid(acc + b_ref[...]).astype(o_ref.dtype)


def _proj_plain(h, wt, row0, n, n_scaled, bm, bn, casts):
    return _proj_call(functools.partial(_proj_plain_kernel, n_scaled=n_scaled),
                      h, wt, row0, n, bm, bn, [], [], "proj_plain", casts)


def _proj_rope(h, wt, row0, n, n_scaled, cosf, sinf, bm, bn, casts):
    table = pl.BlockSpec((bm, HEAD_DIM), lambda i, j: (i, 0))
    return _proj_call(functools.partial(_proj_rope_kernel, n_scaled=n_scaled),
                      h, wt, row0, n, bm, bn, [cosf, sinf], [table, table], "proj_rope", casts)


def _proj_gate(h, wt, row0, n, bias, bm, bn, casts):
    return _proj_call(_proj_gate_kernel, h, wt, row0, n, bm, bn, [bias.reshape(1, n)],
                      [pl.BlockSpec((1, bn), lambda i, j: (0, j))], "proj_gate", casts)


def _vt_proj_kernel(wt_ref, h_ref, o_ref, *, tk):
    r = _nt_dot(wt_ref[...], h_ref[...])
    for t in range(o_ref.shape[0]):
        o_ref[t] = r[:, t * tk:(t + 1) * tk].astype(o_ref.dtype)


def _vt_proj(h, wt, row0, n, bm, bn, tk, casts=()):
    m, d = h.shape
    b0 = row0 // bn
    return _hosted_call(
        functools.partial(_vt_proj_kernel, tk=tk),
        (m // bm, n // bn),
        [
            pl.BlockSpec((bn, d), lambda i, j: (b0 + j, 0)),
            pl.BlockSpec((bm, d), lambda i, j: (i, 0)),
        ],
        pl.BlockSpec((bm // tk, bn, tk), lambda i, j: (i, j, 0)),
        jax.ShapeDtypeStruct((m // tk, n, tk), BF16),
        (wt, h), "proj_vt", casts=casts)


def _forget_kernel(h_ref, w_ref, b_ref, e_ref, o_ref, carry_ref):
    @pl.when(pl.program_id(1) == 0)
    def _():
        carry_ref[...] = jnp.zeros_like(carry_ref)

    ff = _nt_dot(h_ref[...], w_ref[...]) + b_ref[...]
    logf = jnp.minimum(ff, 0.0) - jnp.log1p(jnp.exp(-jnp.abs(ff)))
    tm = logf.shape[0]
    row = lax.broadcasted_iota(jnp.int32, (tm, tm), 0)
    col = lax.broadcasted_iota(jnp.int32, (tm, tm), 1)
    tri = (row >= col).astype(F32)
    cum = jnp.dot(tri, logf, preferred_element_type=F32,
                  precision=lax.Precision.HIGHEST) + carry_ref[...]
    carry_ref[...] = cum[tm - 1:tm, :]

    rest = cum * (-LOG2E)
    pieces = []
    for _ in range(N_SPLIT):
        p = rest.astype(BF16)
        pieces.append(p)
        rest = rest - p.astype(F32)
    spread = jnp.dot(jnp.concatenate(pieces, axis=1), e_ref[...], preferred_element_type=F32)
    o_ref[...] = spread.astype(o_ref.dtype)


def _forget(h, w_ff_t, b_ff, n_heads, batch, seq):
    m, d = h.shape
    tm = _tile(seq, 512)
    per_b = seq // tm
    route = np.zeros((N_SPLIT * LANE, n_heads * HEAD_DIM), np.float32)
    for p in range(N_SPLIT):
        for hh in range(n_heads):
            route[p * LANE + hh, hh * HEAD_DIM + p] = 1.0
    return pl.pallas_call(
        _forget_kernel,
        grid=(batch, per_b),
        in_specs=[
            pl.BlockSpec((tm, d), lambda b, i: (b * per_b + i, 0)),
            pl.BlockSpec((LANE, d), lambda b, i: (0, 0)),
            pl.BlockSpec((1, LANE), lambda b, i: (0, 0)),
            pl.BlockSpec(route.shape, lambda b, i: (0, 0)),
        ],
        out_specs=pl.BlockSpec((tm, n_heads * HEAD_DIM), lambda b, i: (b * per_b + i, 0)),
        out_shape=jax.ShapeDtypeStruct((m, n_heads * HEAD_DIM), BF16),
        scratch_shapes=[pltpu.VMEM((1, LANE), F32)],
        compiler_params=_params("arbitrary", "arbitrary"),
        name="forget_cumsum",
    )(h, w_ff_t, b_ff, jnp.asarray(route, BF16))


def _online_softmax_step(st, vt, m, l, acc):
    m_new = jnp.maximum(m, jnp.max(st, axis=0, keepdims=True))
    alpha = jnp.exp2(m - m_new)
    p = jnp.exp2(st - m_new)
    l_new = alpha * l + jnp.sum(p, axis=0, keepdims=True)
    acc_new = alpha * acc + jnp.dot(vt, p.astype(vt.dtype), preferred_element_type=F32)
    return m_new, l_new, acc_new


def _flash_loop(qi, n_chains, tq, dv, qk_fn, vt_fn, visible_fn):
    def block(kj, state, diagonal):
        sts = []
        for c in range(n_chains):
            st = qk_fn(c, kj)
            if diagonal:
                st = jnp.where(visible_fn(), st, MASKED)
            sts.append(st)
        out = ()
        for c in range(n_chains):
            out += _online_softmax_step(sts[c], vt_fn(c, kj), *state[3 * c:3 * c + 3])
        return out

    state = (jnp.full((1, tq), MASKED, F32), jnp.zeros((1, tq), F32),
             jnp.zeros((dv, tq), F32)) * n_chains
    state = lax.fori_loop(0, qi, lambda kj, s: block(kj, s, False), state)
    return block(qi, state, True)


def _fox_kernel(q_ref, k_ref, kx_ref, vt_ref, o_ref, *, tq, nh):
    qi = pl.program_id(2)
    lane = lax.broadcasted_iota(jnp.int32, (tq, HEAD_DIM), 1)
    ones = (lane < N_SPLIT).astype(q_ref.dtype)

    def head(a):
        return slice(a * HEAD_DIM, (a + 1) * HEAD_DIM)

    def qk(a, kj):
        ks = pl.multiple_of(kj * tq, tq)
        q_aug = jnp.concatenate([q_ref[:, head(a)], ones], axis=1)
        k_aug = jnp.concatenate([k_ref[pl.ds(ks, tq), head(a)], kx_ref[pl.ds(ks, tq), head(a)]], axis=1)
        return _nt_dot(k_aug, q_aug)

    def visible():
        key = lax.broadcasted_iota(jnp.int32, (tq, tq), 0)
        qry = lax.broadcasted_iota(jnp.int32, (tq, tq), 1)
        return key <= qry

    state = _flash_loop(qi, nh, tq, HEAD_DIM, qk, lambda a, kj: vt_ref[kj, head(a), :], visible)
    for a in range(nh):
        _, l, acc = state[3 * a:3 * a + 3]
        o_ref[:, head(a)] = (acc / l).T.astype(o_ref.dtype)


def _fox(qk, kx, vt, batch, seq, n_heads, tq, nh):
    m = qk.shape[0]
    nq = seq // tq
    gw = nh * HEAD_DIM
    ng = n_heads // nh
    return pl.pallas_call(
        functools.partial(_fox_kernel, tq=tq, nh=nh),
        grid=(batch, ng, nq),
        in_specs=[
            pl.BlockSpec((tq, gw), lambda b, g, i: (b * nq + i, g)),
            pl.BlockSpec((seq, gw), lambda b, g, i: (b, ng + g)),
            pl.BlockSpec((seq, gw), lambda b, g, i: (b, g)),
            pl.BlockSpec((nq, gw, tq), lambda b, g, i: (b, g, 0)),
        ],
        out_specs=pl.BlockSpec((tq, gw), lambda b, g, i: (b * nq + i, g)),
        out_shape=jax.ShapeDtypeStruct((m, n_heads * HEAD_DIM), BF16),
        compiler_params=_params("arbitrary", "arbitrary", "arbitrary"),
        name="fox_attention",
    )(qk, qk, kx, vt)


def _diff_kernel(q_ref, k_ref, vt_ref, lam_ref, g_ref, o_ref, *, tq, nh, lambda_init):
    qi = pl.program_id(2)
    n_maps = 2 * nh
    hw = 2 * HEAD_DIM

    def cols(c):
        return slice(c * HEAD_DIM, (c + 1) * HEAD_DIM)

    def qk(c, kj):
        ks = pl.multiple_of(kj * tq, tq)
        return _nt_dot(k_ref[pl.ds(ks, tq), cols(c)], q_ref[:, cols(c)])

    def visible():
        key = lax.broadcasted_iota(jnp.int32, (tq, tq), 0) // CHUNK
        qry = lax.broadcasted_iota(jnp.int32, (tq, tq), 1) // CHUNK
        return key <= qry

    state = _flash_loop(qi, n_maps, tq, hw, qk,
                        lambda c, kj: vt_ref[kj, (c // 2) * hw:(c // 2 + 1) * hw, :], visible)

    lp = lam_ref[...]
    lam = (jnp.exp(jnp.sum(lp[0:1] * lp[1:2], axis=-1, keepdims=True))
           - jnp.exp(jnp.sum(lp[2:3] * lp[3:4], axis=-1, keepdims=True)) + lambda_init)
    for a in range(nh):
        _, l1, a1, _, l2, a2 = state[6 * a:6 * a + 6]
        y = a1 / l1 - lam * (a2 / l2)
        ms = jnp.mean(y * y, axis=0, keepdims=True)
        y = (y * lax.rsqrt(ms + EPS)).T * g_ref[...]
        o_ref[:, a * hw:(a + 1) * hw] = (y * (1.0 - lambda_init)).astype(o_ref.dtype)


def _diff(qk, vt, lam_p, subln, batch, seq, n_heads, tq, nh, lambda_init):
    m = qk.shape[0]
    nq = seq // tq
    hw = 2 * HEAD_DIM
    gw = nh * hw
    ng = n_heads // nh
    return pl.pallas_call(
        functools.partial(_diff_kernel, tq=tq, nh=nh, lambda_init=lambda_init),
        grid=(batch, ng, nq),
        in_specs=[
            pl.BlockSpec((tq, gw), lambda b, g, i: (b * nq + i, g)),
            pl.BlockSpec((seq, gw), lambda b, g, i: (b, ng + g)),
            pl.BlockSpec((nq, gw, tq), lambda b, g, i: (b, g, 0)),
            pl.BlockSpec((4, HEAD_DIM), lambda b, g, i: (0, 0)),
            pl.BlockSpec((1, hw), lambda b, g, i: (0, 0)),
        ],
        out_specs=pl.BlockSpec((tq, gw), lambda b, g, i: (b * nq + i, g)),
        out_shape=jax.ShapeDtypeStruct((m, n_heads * hw), BF16),
        compiler_params=_params("arbitrary", "arbitrary", "arbitrary"),
        name="diff_attention",
    )(qk, qk, vt, lam_p, subln.reshape(1, hw))


def _merge_kernel(ya_ref, yb_ref, wa_ref, wb_ref, sa_ref, sb_ref, o_ref):
    pa = jnp.dot(ya_ref[...], wa_ref[...], preferred_element_type=F32)
    pb = jnp.dot(yb_ref[...], wb_ref[...], preferred_element_type=F32)
    o_ref[...] = (sa_ref[...].astype(F32) * pa + sb_ref[...].astype(F32) * pb).astype(o_ref.dtype)


def _merge(ya, yb, wa, wb, gates, bm, bn):
    m = ya.shape[0]
    d = wa.shape[1]
    nb = d // bn
    return pl.pallas_call(
        _merge_kernel,
        grid=(m // bm, nb),
        in_specs=[
            pl.BlockSpec((bm, ya.shape[1]), lambda i, j: (i, 0)),
            pl.BlockSpec((bm, yb.shape[1]), lambda i, j: (i, 0)),
            pl.BlockSpec((wa.shape[0], bn), lambda i, j: (0, j)),
            pl.BlockSpec((wb.shape[0], bn), lambda i, j: (0, j)),
            pl.BlockSpec((bm, bn), lambda i, j: (i, j)),
            pl.BlockSpec((bm, bn), lambda i, j: (i, nb + j)),
        ],
        out_specs=pl.BlockSpec((bm, bn), lambda i, j: (i, j)),
        out_shape=jax.ShapeDtypeStruct((m, d), BF16),
        compiler_params=_params("arbitrary", "arbitrary"),
        name="gated_merge",
    )(ya, yb, wa, wb, gates, gates)


def _ffn_width(f):
    return _round_up(f, 1024) if f >= 1024 else f


def _ffn_in_cast(w_in, layer):
    _, d, f2 = w_in.shape
    f = f2 // 2
    return _StripCast(w_in, layer, 0, d, d, ((0, f, _ffn_width(f)), (f, f, _ffn_width(f))))


def _ffn(x2, gain, wg, wu, w_out, layer, mod3, j0, seq, next_casts):
    m, d = x2.shape
    f, fp = w_out.shape[1], wg.shape[1]
    bm = _tile(seq, 1024)
    h = _norm_mod(x2, gain, mod3, N_MOD, j0, j0 + 1, seq, BF16)
    wo_cast = _StripCast(w_out, layer, 0, f, fp, ((0, d, d),))
    a, ((wo,),) = _ffn_in(h, wg, wu, bm, _tile(fp, 512), (wo_cast,))
    bk = fp // 4 if (fp // 4) % LANE == 0 else fp
    return _res_matmul(a, wo, x2, mod3, j0 + 2, 0.5, seq, bm, _tile(d, 1024), bk, "ffn_out",
                       casts=next_casts)


def kernel(x, c, positions, ada_w, ada_b, norm_ffn1, ffn1_w_in, ffn1_w_out, norm_mix, w_in, b_forget, b_gate, diff_lambda, diff_subln, w_o_fox, w_o_diff, w_out, norm_ffn2, ffn2_w_in, ffn2_w_out, final_ada_w, final_ada_b, norm_final):
    batch, seq, d = x.shape
    depth = ada_w.shape[0]
    m = batch * seq
    n_fox = b_forget.shape[1]
    fox_w = n_fox * HEAD_DIM
    diff_w = w_o_diff.shape[1]
    n_diff = diff_w // (2 * HEAD_DIM)
    o_ff = 3 * fox_w
    o_dq = o_ff + n_fox
    n_rest = 3 * diff_w + 2 * d
    assert w_in.shape[2] == o_dq + n_rest
    assert n_fox <= LANE and seq % CHUNK == 0

    c_pad = jnp.pad(c, ((0, -batch % 8), (0, 0)))
    x2 = x.reshape(m, d)
    w_in_t = jnp.swapaxes(w_in, 1, 2)

    inv_freq = ROPE_THETA ** (-jnp.arange(0, ROPE_DIM, 2, dtype=F32) / ROPE_DIM)
    ang = positions.astype(F32).reshape(m, 1) * inv_freq
    cos, sin = jnp.cos(ang), jnp.sin(ang)
    cosf = jnp.concatenate([cos, cos, jnp.ones((m, HEAD_DIM - ROPE_DIM), F32)], axis=-1)
    sinf = jnp.concatenate([-sin, sin, jnp.zeros((m, HEAD_DIM - ROPE_DIM), F32)], axis=-1)

    bm = _tile(seq, 1024)
    tq = _tile(seq, 512)
    bn_p = _tile(math.gcd(fox_w, diff_w), 1024)
    for l in range(depth):
        lambda_init = 0.8 - 0.6 * math.exp(-0.3 * l)
        mod = _ada(c_pad, ada_w[l], ada_b[l])[:batch]
        mod3 = mod.reshape(batch * N_MOD, 1, d)

        wg, wu = _run_cast(_ffn_in_cast(ffn1_w_in, l))
        mixer_casts = (_StripCast(w_in_t, l, 0, o_ff, o_ff, ((0, d, d),)),
                       _StripCast(w_in_t, l, o_dq, n_rest, n_rest, ((0, d, d),)))
        x2, ((w_front,), (w_rest,)) = _ffn(x2, norm_ffn1[l], wg, wu, ffn1_w_out, l, mod3, 0, seq,
                                           mixer_casts)

        h = _norm_mod(x2, norm_mix[l], mod3, N_MOD, 3, 4, seq, BF16)
        w_ff_t = jnp.pad(w_in_t[l, o_ff:o_dq, :].astype(BF16), ((0, LANE - n_fox), (0, 0)))
        b_ff = jnp.pad(b_forget[l].astype(F32), (0, LANE - n_fox)).reshape(1, LANE)

        qk_f, ((wa,),) = _proj_plain(h, w_front, 0, 2 * fox_w, fox_w // bn_p, bm, bn_p,
                                     (_whole(w_o_fox, l),))
        qk_d, ((wb,),) = _proj_rope(h, w_rest, 0, 2 * diff_w, diff_w // bn_p, cosf, sinf, bm, bn_p,
                                    (_whole(w_o_diff, l),))
        gates, ((wg, wu),) = _proj_gate(h, w_rest, 3 * diff_w, 2 * d, b_gate[l], bm, bn_p,
                                        (_ffn_in_cast(ffn2_w_in, l),))
        vt_f, ((wo,),) = _vt_proj(h, w_front, 2 * fox_w, fox_w, bm, bn_p, tq, (_whole(w_out, l),))
        vt_d, _ = _vt_proj(h, w_rest, 2 * diff_w, diff_w, bm, bn_p, tq)
        kx = _forget(h, w_ff_t, b_ff, n_fox, batch, seq)

        ya = _fox(qk_f, kx, vt_f, batch, seq, n_fox, tq, math.gcd(n_fox, FOX_HEADS_PER_STEP))
        yb = _diff(qk_d, vt_d, diff_lambda[l].astype(F32), diff_subln[l], batch, seq, n_diff,
                   tq, math.gcd(n_diff, DIFF_HEADS_PER_STEP), lambda_init)
        merged = _merge(ya, yb, wa, wb, gates, bm, _tile(d, 1024))
        x2, _ = _res_matmul(merged, wo, x2, mod3, 5, 1.0, seq, bm, _tile(d, 1024), d, "mixer_out")

        x2, _ = _ffn(x2, norm_ffn2[l], wg, wu, ffn2_w_out, l, mod3, 6, seq, ())

    fmod = _ada(c_pad, final_ada_w, final_ada_b)[:batch]
    out = _norm_mod(x2, norm_final, fmod.reshape(batch * 2, 1, d), 2, 0, 1, seq, F32)
    return out.reshape(batch, seq, d)
```

```python
import functools
import math
from typing import NamedTuple

import numpy as np
import jax
import jax.numpy as jnp
from jax import lax
from jax.experimental import pallas as pl
from jax.experimental.pallas import tpu as pltpu

HEAD_DIM = 128
CHUNK = 64
ROPE_THETA = 500000.0
ROPE_DIM = HEAD_DIM // 4
ROPE_HALF = ROPE_DIM // 2
EPS = 1e-6
N_MOD = 9
LOG2E = 1.4426950408889634
MASKED = -1e30
LANE = 128
VMEM_LIMIT = 56 * 1024 * 1024
BF16_ROWS = 16
CAST_BLOCK_ELEMS = 2 * 1024 * 1024
HOSTED_STRIP_ELEMS = 1024 * 1024
N_SPLIT = 3
FOX_HEADS_PER_STEP = 4
DIFF_HEADS_PER_STEP = 2

F32 = jnp.float32
BF16 = jnp.bfloat16


def _params(*sem):
    return pltpu.CompilerParams(dimension_semantics=sem, vmem_limit_bytes=VMEM_LIMIT)


def _tile(n, pref):
    t = min(n, pref)
    while n % t:
        t //= 2
    return t


def _round_up(n, m):
    return (n + m - 1) // m * m


def _sigmoid(x):
    return 0.5 * jnp.tanh(0.5 * x) + 0.5


def _nt_dot(a, b):
    return lax.dot_general(a, b, (((1,), (1,)), ((), ())), preferred_element_type=F32)


class _StripCast(NamedTuple):
    src: jax.Array
    layer: int
    row0: int
    rows: int
    rows_out: int
    pieces: tuple


def _whole(src, layer):
    _, r, c = src.shape
    return _StripCast(src, layer, 0, r, r, ((0, c, c),))


def _strip_rows(cast, lo, hi):
    g = math.gcd(math.gcd(cast.row0, cast.rows), cast.rows_out)
    for rs in range(_round_up(max(lo, 1), BF16_ROWS), min(hi, g) + 1, BF16_ROWS):
        if g % rs == 0:
            return rs
    return None


def _linear_step(grid, ids):
    t = ids[0]
    for g, i in zip(grid[1:], ids[1:]):
        t = t * g + i
    return t


def _strip_specs(cast, rs, grid):
    first, n_valid, n_strips = cast.row0 // rs, cast.rows // rs, cast.rows_out // rs
    cols = cast.src.shape[2]
    src_spec = pl.BlockSpec(
        (None, rs, cols),
        lambda *ids: (cast.layer, first + jnp.minimum(_linear_step(grid, ids), n_valid - 1), 0))
    out_specs = [pl.BlockSpec((rs, nco), lambda *ids: (jnp.minimum(_linear_step(grid, ids), n_strips - 1), 0))
                 for _, _, nco in cast.pieces]
    out_shapes = [jax.ShapeDtypeStruct((cast.rows_out, nco), BF16) for _, _, nco in cast.pieces]
    return src_spec, out_specs, out_shapes


def _strip_cast_body(cast, rs, grid, src_ref, dst_refs):
    t = _linear_step(grid, [pl.program_id(a) for a in range(len(grid))])
    strip = src_ref[...]
    for (c0, n, nco), dst in zip(cast.pieces, dst_refs):
        piece = strip[:, c0:c0 + n].astype(dst.dtype)
        if cast.rows < cast.rows_out:
            piece = jnp.where(t < cast.rows // rs, piece, jnp.zeros_like(piece))
        if nco > n:
            dst[:, :n] = piece
            dst[:, n:] = jnp.zeros((rs, nco - n), dst.dtype)
        else:
            dst[...] = piece


def _run_cast(cast):
    rs = _strip_rows(cast, 1, max(CAST_BLOCK_ELEMS // cast.src.shape[2], BF16_ROWS))
    rs = rs or _strip_rows(cast, 1, cast.rows_out)
    if rs is None:
        w = cast.src[cast.layer, cast.row0:cast.row0 + cast.rows]
        return tuple(jnp.pad(w[:, c0:c0 + n].astype(BF16), ((0, cast.rows_out - cast.rows), (0, nco - n)))
                     for c0, n, nco in cast.pieces)
    grid = (cast.rows_out // rs,)
    src_spec, out_specs, out_shapes = _strip_specs(cast, rs, grid)

    def body(src_ref, *dst_refs):
        _strip_cast_body(cast, rs, grid, src_ref, dst_refs)

    return tuple(pl.pallas_call(
        body, grid=grid, in_specs=[src_spec], out_specs=out_specs, out_shape=out_shapes,
        compiler_params=_params("arbitrary"), name="weight_cast",
    )(cast.src))


def _hosted_call(body, grid, in_specs, out_spec, out_shape, inputs, name, scratch=(), casts=()):
    steps = math.prod(grid)
    hosted, strip_rows = [], []
    for cast in casts:
        rs = _strip_rows(cast, pl.cdiv(cast.rows_out, steps), cast.rows_out)
        if rs is not None and rs * cast.src.shape[2] <= HOSTED_STRIP_ELEMS:
            hosted.append(cast)
            strip_rows.append(rs)
    specs = [_strip_specs(cast, rs, grid) for cast, rs in zip(hosted, strip_rows)]
    n_in, n_src = len(in_specs), len(hosted)

    def kernel_body(*refs):
        ins, src_refs, o_ref = refs[:n_in], refs[n_in:n_in + n_src], refs[n_in + n_src]
        pos = n_in + n_src + 1
        dst_groups = []
        for cast in hosted:
            dst_groups.append(refs[pos:pos + len(cast.pieces)])
            pos += len(cast.pieces)
        body(*ins, o_ref, *refs[pos:])
        for cast, rs, src_ref, dst_refs in zip(hosted, strip_rows, src_refs, dst_groups):
            _strip_cast_body(cast, rs, grid, src_ref, dst_refs)

    outs = pl.pallas_call(
        kernel_body, grid=grid,
        in_specs=list(in_specs) + [s[0] for s in specs],
        out_specs=[out_spec] + [o for s in specs for o in s[1]],
        out_shape=[out_shape] + [o for s in specs for o in s[2]],
        scratch_shapes=list(scratch),
        compiler_params=_params(*(("arbitrary",) * len(grid))), name=name,
    )(*inputs, *[cast.src for cast in hosted])

    results, pos = [], 1
    for cast in casts:
        if any(cast is hc for hc in hosted):
            results.append(tuple(outs[pos:pos + len(cast.pieces)]))
            pos += len(cast.pieces)
        else:
            results.append(_run_cast(cast))
    return outs[0], tuple(results)


def _ada_kernel(c_ref, w_ref, b_ref, o_ref):
    c = c_ref[...]
    sc = (c * jax.nn.sigmoid(c)).astype(BF16)
    o_ref[...] = jnp.dot(sc, w_ref[...].astype(BF16), preferred_element_type=F32) + b_ref[...]


def _ada(c_pad, w, b):
    d, n = w.shape
    tn = _tile(n, 512)
    return pl.pallas_call(
        _ada_kernel,
        grid=(n // tn,),
        in_specs=[
            pl.BlockSpec((c_pad.shape[0], d), lambda j: (0, 0)),
            pl.BlockSpec((d, tn), lambda j: (0, j)),
            pl.BlockSpec((1, tn), lambda j: (0, j)),
        ],
        out_specs=pl.BlockSpec((c_pad.shape[0], tn), lambda j: (0, j)),
        out_shape=jax.ShapeDtypeStruct((c_pad.shape[0], n), F32),
        compiler_params=_params("arbitrary"),
        name="ada_matvec",
    )(c_pad, w, b.reshape(1, n))


def _norm_mod_kernel(x_ref, g_ref, sh_ref, sc_ref, o_ref):
    x = x_ref[...]
    ms = jnp.mean(x * x, axis=-1, keepdims=True)
    y = x * lax.rsqrt(ms + EPS) * g_ref[...]
    o_ref[...] = (y * (1.0 + sc_ref[0]) + sh_ref[0]).astype(o_ref.dtype)


def _norm_mod(x2, gain, mod3, n_mod, shift_j, scale_j, seq, out_dtype):
    m, d = x2.shape
    tm = _tile(seq, 512)
    per_b = seq // tm
    return pl.pallas_call(
        _norm_mod_kernel,
        grid=(m // tm,),
        in_specs=[
            pl.BlockSpec((tm, d), lambda i: (i, 0)),
            pl.BlockSpec((1, d), lambda i: (0, 0)),
            pl.BlockSpec((1, 1, d), lambda i: ((i // per_b) * n_mod + shift_j, 0, 0)),
            pl.BlockSpec((1, 1, d), lambda i: ((i // per_b) * n_mod + scale_j, 0, 0)),
        ],
        out_specs=pl.BlockSpec((tm, d), lambda i: (i, 0)),
        out_shape=jax.ShapeDtypeStruct((m, d), out_dtype),
        compiler_params=_params("arbitrary"),
        name="norm_mod",
    )(x2, gain.reshape(1, d), mod3, mod3)


def _ffn_in_kernel(h_ref, wg_ref, wu_ref, o_ref):
    h = h_ref[...]
    g = jnp.dot(h, wg_ref[...], preferred_element_type=F32)
    u = jnp.dot(h, wu_ref[...], preferred_element_type=F32)
    o_ref[...] = (g * _sigmoid(g) * u).astype(o_ref.dtype)


def _ffn_in(h, wg, wu, bm, bn, casts):
    m, d = h.shape
    fp = wg.shape[1]
    return _hosted_call(
        _ffn_in_kernel,
        (m // bm, fp // bn),
        [
            pl.BlockSpec((bm, d), lambda i, j: (i, 0)),
            pl.BlockSpec((d, bn), lambda i, j: (0, j)),
            pl.BlockSpec((d, bn), lambda i, j: (0, j)),
        ],
        pl.BlockSpec((bm, bn), lambda i, j: (i, j)),
        jax.ShapeDtypeStruct((m, fp), BF16),
        (h, wg, wu), "ffn_in", casts=casts)


def _res_matmul_kernel(a_ref, w_ref, res_ref, gate_ref, o_ref, acc_ref, *, coef, nk):
    k = pl.program_id(2)

    def partial_product():
        return jnp.dot(a_ref[...], w_ref[...], preferred_element_type=F32)

    def finish(total):
        o_ref[...] = res_ref[...] + (coef * gate_ref[0]) * total

    if nk == 1:
        finish(partial_product())
        return

    @pl.when(k == 0)
    def _():
        acc_ref[...] = partial_product()

    @pl.when((k > 0) & (k < nk - 1))
    def _():
        acc_ref[...] += partial_product()

    @pl.when(k == nk - 1)
    def _():
        finish(acc_ref[...] + partial_product())


def _res_matmul(a, w, res, mod3, gate_j, coef, seq, bm, bn, bk, name, casts=()):
    m, kdim = a.shape
    n = w.shape[1]
    per_b = seq // bm
    return _hosted_call(
        functools.partial(_res_matmul_kernel, coef=coef, nk=kdim // bk),
        (m // bm, n // bn, kdim // bk),
        [
            pl.BlockSpec((bm, bk), lambda i, j, k: (i, k)),
            pl.BlockSpec((bk, bn), lambda i, j, k: (k, j)),
            pl.BlockSpec((bm, bn), lambda i, j, k: (i, j)),
            pl.BlockSpec((1, 1, bn), lambda i, j, k: ((i // per_b) * N_MOD + gate_j, 0, j)),
        ],
        pl.BlockSpec((bm, bn), lambda i, j, k: (i, j)),
        jax.ShapeDtypeStruct((m, n), F32),
        (a, w, res, mod3), name, scratch=[pltpu.VMEM((bm, bn), F32)], casts=casts)


QSCALE = HEAD_DIM ** -0.5 * LOG2E


def _proj_call(body, h, wt, row0, n, bm, bn, extra_in, extra_specs, name, casts):
    m, d = h.shape
    b0 = row0 // bn
    return _hosted_call(
        body,
        (m // bm, n // bn),
        [
            pl.BlockSpec((bm, d), lambda i, j: (i, 0)),
            pl.BlockSpec((bn, d), lambda i, j: (b0 + j, 0)),
        ] + extra_specs,
        pl.BlockSpec((bm, bn), lambda i, j: (i, j)),
        jax.ShapeDtypeStruct((m, n), BF16),
        (h, wt, *extra_in), name, casts=casts)


def _block_scale(n_scaled):
    return jnp.where(pl.program_id(1) < n_scaled, QSCALE, 1.0).astype(F32)


def _proj_plain_kernel(h_ref, wt_ref, o_ref, *, n_scaled):
    acc = _nt_dot(h_ref[...], wt_ref[...])
    o_ref[...] = (acc * _block_scale(n_scaled)).astype(o_ref.dtype)


def _proj_rope_kernel(h_ref, wt_ref, cos_ref, sin_ref, o_ref, *, n_scaled):
    acc = _nt_dot(h_ref[...], wt_ref[...])
    cosf = cos_ref[...] * _block_scale(n_scaled)
    sinf = sin_ref[...] * _block_scale(n_scaled)
    lane = lax.broadcasted_iota(jnp.int32, cosf.shape, 1)
    for c in range(acc.shape[1] // HEAD_DIM):
        cols = slice(c * HEAD_DIM, (c + 1) * HEAD_DIM)
        t = acc[:, cols]
        partner = jnp.where(lane < ROPE_HALF,
                            pltpu.roll(t, HEAD_DIM - ROPE_HALF, 1),
                            pltpu.roll(t, ROPE_HALF, 1))
        o_ref[:, cols] = (t * cosf + partner * sinf).astype(o_ref.dtype)


def _proj_gate_kernel(h_ref, wt_ref, b_ref, o_ref):
    acc = _nt_dot(h_ref[...], wt_ref[...])
    o_ref[...] = _sigmoid(acc + b_ref[...]).astype(o_ref.dtype)


def _proj_plain(h, wt, row0, n, n_scaled, bm, bn, casts):
    return _proj_call(functools.partial(_proj_plain_kernel, n_scaled=n_scaled),
                      h, wt, row0, n, bm, bn, [], [], "proj_plain", casts)


def _proj_rope(h, wt, row0, n, n_scaled, cosf, sinf, bm, bn, casts):
    table = pl.BlockSpec((bm, HEAD_DIM), lambda i, j: (i, 0))
    return _proj_call(functools.partial(_proj_rope_kernel, n_scaled=n_scaled),
                      h, wt, row0, n, bm, bn, [cosf, sinf], [table, table], "proj_rope", casts)


def _proj_gate(h, wt, row0, n, bias, bm, bn, casts):
    return _proj_call(_proj_gate_kernel, h, wt, row0, n, bm, bn, [bias.reshape(1, n)],
                      [pl.BlockSpec((1, bn), lambda i, j: (0, j))], "proj_gate", casts)


def _vt_proj_kernel(wt_ref, h_ref, o_ref, *, tk):
    r = _nt_dot(wt_ref[...], h_ref[...])
    for t in range(o_ref.shape[0]):
        o_ref[t] = r[:, t * tk:(t + 1) * tk].astype(o_ref.dtype)


def _vt_proj(h, wt, row0, n, bm, bn, tk, casts=()):
    m, d = h.shape
    b0 = row0 // bn
    return _hosted_call(
        functools.partial(_vt_proj_kernel, tk=tk),
        (m // bm, n // bn),
        [
            pl.BlockSpec((bn, d), lambda i, j: (b0 + j, 0)),
            pl.BlockSpec((bm, d), lambda i, j: (i, 0)),
        ],
        pl.BlockSpec((bm // tk, bn, tk), lambda i, j: (i, j, 0)),
        jax.ShapeDtypeStruct((m // tk, n, tk), BF16),
        (wt, h), "proj_vt", casts=casts)


def _forget_kernel(h_ref, w_ref, b_ref, e_ref, o_ref, carry_ref):
    @pl.when(pl.program_id(1) == 0)
    def _():
        carry_ref[...] = jnp.zeros_like(carry_ref)

    ff = _nt_dot(h_ref[...], w_ref[...]) + b_ref[...]
    logf = jnp.minimum(ff, 0.0) - jnp.log1p(jnp.exp(-jnp.abs(ff)))
    tm = logf.shape[0]
    row = lax.broadcasted_iota(jnp.int32, (tm, tm), 0)
    col = lax.broadcasted_iota(jnp.int32, (tm, tm), 1)
    tri = (row >= col).astype(F32)
    cum = jnp.dot(tri, logf, preferred_element_type=F32,
                  precision=lax.Precision.HIGHEST) + carry_ref[...]
    carry_ref[...] = cum[tm - 1:tm, :]

    rest = cum * (-LOG2E)
    pieces = []
    for _ in range(N_SPLIT):
        p = rest.astype(BF16)
        pieces.append(p)
        rest = rest - p.astype(F32)
    spread = jnp.dot(jnp.concatenate(pieces, axis=1), e_ref[...], preferred_element_type=F32)
    o_ref[...] = spread.astype(o_ref.dtype)


def _forget(h, w_ff_t, b_ff, n_heads, batch, seq):
    m, d = h.shape
    tm = _tile(seq, 512)
    per_b = seq // tm
    route = np.zeros((N_SPLIT * LANE, n_heads * HEAD_DIM), np.float32)
    for p in range(N_SPLIT):
        for hh in range(n_heads):
            route[p * LANE + hh, hh * HEAD_DIM + p] = 1.0
    return pl.pallas_call(
        _forget_kernel,
        grid=(batch, per_b),
        in_specs=[
            pl.BlockSpec((tm, d), lambda b, i: (b * per_b + i, 0)),
            pl.BlockSpec((LANE, d), lambda b, i: (0, 0)),
            pl.BlockSpec((1, LANE), lambda b, i: (0, 0)),
            pl.BlockSpec(route.shape, lambda b, i: (0, 0)),
        ],
        out_specs=pl.BlockSpec((tm, n_heads * HEAD_DIM), lambda b, i: (b * per_b + i, 0)),
        out_shape=jax.ShapeDtypeStruct((m, n_heads * HEAD_DIM), BF16),
        scratch_shapes=[pltpu.VMEM((1, LANE), F32)],
        compiler_params=_params("arbitrary", "arbitrary"),
        name="forget_cumsum",
    )(h, w_ff_t, b_ff, jnp.asarray(route, BF16))


def _online_softmax_step(st, vt, m, l, acc):
    m_new = jnp.maximum(m, jnp.max(st, axis=0, keepdims=True))
    alpha = jnp.exp2(m - m_new)
    p = jnp.exp2(st - m_new)
    l_new = alpha * l + jnp.sum(p, axis=0, keepdims=True)
    acc_new = alpha * acc + jnp.dot(vt, p.astype(vt.dtype), preferred_element_type=F32)
    return m_new, l_new, acc_new


def _flash_scratch(n_chains, tq, dv):
    return ([pltpu.VMEM((2, tq, tq), F32) for _ in range(n_chains)]
            + [pltpu.VMEM((n_chains, 1, tq), F32), pltpu.VMEM((n_chains, 1, tq), F32),
               pltpu.VMEM((n_chains, dv, tq), F32)])


def _flash_loop(qi, n_chains, qk_fn, vt_fn, visible_fn, st_refs, m_ref, l_ref, acc_ref):
    def produce(kj, slot, diagonal):
        for c in range(n_chains):
            st = qk_fn(c, kj)
            if diagonal:
                st = jnp.where(visible_fn(), st, MASKED)
            st_refs[c][slot] = st

    def consume(c, kj, slot):
        m, l, acc = _online_softmax_step(st_refs[c][slot], vt_fn(c, kj), m_ref[c], l_ref[c], acc_ref[c])
        m_ref[c] = m
        l_ref[c] = l
        acc_ref[c] = acc

    def step(kj, slot, next_is_diagonal):
        for c in range(n_chains):
            st = qk_fn(c, kj + 1)
            if next_is_diagonal:
                st = jnp.where(visible_fn(), st, MASKED)
            st_refs[c][1 - slot] = st
            consume(c, kj, slot)

    def run_if(cond, fn):
        def body(_, carry):
            fn()
            return carry
        lax.fori_loop(0, cond.astype(jnp.int32), body, 0)

    tq = m_ref.shape[-1]
    m_ref[...] = jnp.full(m_ref.shape, MASKED, F32)
    l_ref[...] = jnp.zeros(l_ref.shape, F32)
    acc_ref[...] = jnp.zeros(acc_ref.shape, F32)

    run_if(qi == 0, lambda: produce(0, 0, True))
    run_if(qi > 0, lambda: produce(0, 0, False))

    n_pairs = jnp.maximum(qi - 1, 0) // 2
    kj0 = 2 * n_pairs

    def pair(i, carry):
        step(2 * i, 0, False)
        step(2 * i + 1, 1, False)
        return carry

    lax.fori_loop(0, n_pairs, pair, 0)
    two_left = (qi > 0) & (qi - kj0 == 2)
    one_left = (qi > 0) & (qi - kj0 == 1)
    run_if(two_left, lambda: step(kj0, 0, False))
    run_if(two_left, lambda: step(kj0 + 1, 1, True))
    run_if(one_left, lambda: step(kj0, 0, True))
    run_if(one_left, lambda: [consume(c, qi, 1) for c in range(n_chains)])
    run_if(jnp.logical_not(one_left), lambda: [consume(c, qi, 0) for c in range(n_chains)])


def _fox_kernel(q_ref, k_ref, kx_ref, vt_ref, o_ref, *scratch, tq, nh):
    qi = pl.program_id(2)
    st_refs, (m_ref, l_ref, acc_ref) = scratch[:nh], scratch[nh:]
    lane = lax.broadcasted_iota(jnp.int32, (tq, HEAD_DIM), 1)
    ones = (lane < N_SPLIT).astype(q_ref.dtype)

    def head(a):
        return slice(a * HEAD_DIM, (a + 1) * HEAD_DIM)

    def qk(a, kj):
        ks = pl.multiple_of(kj * tq, tq)
        q_aug = jnp.concatenate([q_ref[:, head(a)], ones], axis=1)
        k_aug = jnp.concatenate([k_ref[pl.ds(ks, tq), head(a)], kx_ref[pl.ds(ks, tq), head(a)]], axis=1)
        return _nt_dot(k_aug, q_aug)

    def visible():
        key = lax.broadcasted_iota(jnp.int32, (tq, tq), 0)
        qry = lax.broadcasted_iota(jnp.int32, (tq, tq), 1)
        return key <= qry

    _flash_loop(qi, nh, qk, lambda a, kj: vt_ref[kj, head(a), :], visible,
                st_refs, m_ref, l_ref, acc_ref)
    for a in range(nh):
        o_ref[:, head(a)] = (acc_ref[a] / l_ref[a]).T.astype(o_ref.dtype)


def _fox(qk, kx, vt, batch, seq, n_heads, tq, nh):
    m = qk.shape[0]
    nq = seq // tq
    gw = nh * HEAD_DIM
    ng = n_heads // nh
    return pl.pallas_call(
        functools.partial(_fox_kernel, tq=tq, nh=nh),
        grid=(batch, ng, nq),
        in_specs=[
            pl.BlockSpec((tq, gw), lambda b, g, i: (b * nq + i, g)),
            pl.BlockSpec((seq, gw), lambda b, g, i: (b, ng + g)),
            pl.BlockSpec((seq, gw), lambda b, g, i: (b, g)),
            pl.BlockSpec((nq, gw, tq), lambda b, g, i: (b, g, 0)),
        ],
        out_specs=pl.BlockSpec((tq, gw), lambda b, g, i: (b * nq + i, g)),
        out_shape=jax.ShapeDtypeStruct((m, n_heads * HEAD_DIM), BF16),
        scratch_shapes=_flash_scratch(nh, tq, HEAD_DIM),
        compiler_params=_params("arbitrary", "arbitrary", "arbitrary"),
        name="fox_attention",
    )(qk, qk, kx, vt)


def _diff_kernel(q_ref, k_ref, vt_ref, lam_ref, g_ref, o_ref, *scratch, tq, nh, lambda_init):
    qi = pl.program_id(2)
    n_maps = 2 * nh
    st_refs, (m_ref, l_ref, acc_ref) = scratch[:n_maps], scratch[n_maps:]
    hw = 2 * HEAD_DIM

    def cols(c):
        return slice(c * HEAD_DIM, (c + 1) * HEAD_DIM)

    def qk(c, kj):
        ks = pl.multiple_of(kj * tq, tq)
        return _nt_dot(k_ref[pl.ds(ks, tq), cols(c)], q_ref[:, cols(c)])

    def visible():
        key = lax.broadcasted_iota(jnp.int32, (tq, tq), 0) // CHUNK
        qry = lax.broadcasted_iota(jnp.int32, (tq, tq), 1) // CHUNK
        return key <= qry

    _flash_loop(qi, n_maps, qk, lambda c, kj: vt_ref[kj, (c // 2) * hw:(c // 2 + 1) * hw, :],
                visible, st_refs, m_ref, l_ref, acc_ref)

    lp = lam_ref[...]
    lam = (jnp.exp(jnp.sum(lp[0:1] * lp[1:2], axis=-1, keepdims=True))
           - jnp.exp(jnp.sum(lp[2:3] * lp[3:4], axis=-1, keepdims=True)) + lambda_init)
    for a in range(nh):
        y = acc_ref[2 * a] / l_ref[2 * a] - lam * (acc_ref[2 * a + 1] / l_ref[2 * a + 1])
        ms = jnp.mean(y * y, axis=0, keepdims=True)
        y = (y * lax.rsqrt(ms + EPS)).T * g_ref[...]
        o_ref[:, a * hw:(a + 1) * hw] = (y * (1.0 - lambda_init)).astype(o_ref.dtype)


def _diff(qk, vt, lam_p, subln, batch, seq, n_heads, tq, nh, lambda_init):
    m = qk.shape[0]
    nq = seq // tq
    hw = 2 * HEAD_DIM
    gw = nh * hw
    ng = n_heads // nh
    return pl.pallas_call(
        functools.partial(_diff_kernel, tq=tq, nh=nh, lambda_init=lambda_init),
        grid=(batch, ng, nq),
        in_specs=[
            pl.BlockSpec((tq, gw), lambda b, g, i: (b * nq + i, g)),
            pl.BlockSpec((seq, gw), lambda b, g, i: (b, ng + g)),
            pl.BlockSpec((nq, gw, tq), lambda b, g, i: (b, g, 0)),
            pl.BlockSpec((4, HEAD_DIM), lambda b, g, i: (0, 0)),
            pl.BlockSpec((1, hw), lambda b, g, i: (0, 0)),
        ],
        out_specs=pl.BlockSpec((tq, gw), lambda b, g, i: (b * nq + i, g)),
        out_shape=jax.ShapeDtypeStruct((m, n_heads * hw), BF16),
        scratch_shapes=_flash_scratch(2 * nh, tq, hw),
        compiler_params=_params("arbitrary", "arbitrary", "arbitrary"),
        name="diff_attention",
    )(qk, qk, vt, lam_p, subln.reshape(1, hw))


def _merge_kernel(ya_ref, yb_ref, wa_ref, wb_ref, sa_ref, sb_ref, o_ref):
    pa = jnp.dot(ya_ref[...], wa_ref[...], preferred_element_type=F32)
    pb = jnp.dot(yb_ref[...], wb_ref[...], preferred_element_type=F32)
    o_ref[...] = (sa_ref[...].astype(F32) * pa + sb_ref[...].astype(F32) * pb).astype(o_ref.dtype)


def _merge(ya, yb, wa, wb, gates, bm, bn):
    m = ya.shape[0]
    d = wa.shape[1]
    nb = d // bn
    return pl.pallas_call(
        _merge_kernel,
        grid=(m // bm, nb),
        in_specs=[
            pl.BlockSpec((bm, ya.shape[1]), lambda i, j: (i, 0)),
            pl.BlockSpec((bm, yb.shape[1]), lambda i, j: (i, 0)),
            pl.BlockSpec((wa.shape[0], bn), lambda i, j: (0, j)),
            pl.BlockSpec((wb.shape[0], bn), lambda i, j: (0, j)),
            pl.BlockSpec((bm, bn), lambda i, j: (i, j)),
            pl.BlockSpec((bm, bn), lambda i, j: (i, nb + j)),
        ],
        out_specs=pl.BlockSpec((bm, bn), lambda i, j: (i, j)),
        out_shape=jax.ShapeDtypeStruct((m, d), BF16),
        compiler_params=_params("arbitrary", "arbitrary"),
        name="gated_merge",
    )(ya, yb, wa, wb, gates, gates)


def _ffn_width(f):
    return _round_up(f, 1024) if f >= 1024 else f


def _ffn_in_cast(w_in, layer):
    _, d, f2 = w_in.shape
    f = f2 // 2
    return _StripCast(w_in, layer, 0, d, d, ((0, f, _ffn_width(f)), (f, f, _ffn_width(f))))


def _ffn(x2, gain, wg, wu, w_out, layer, mod3, j0, seq, next_casts):
    m, d = x2.shape
    f, fp = w_out.shape[1], wg.shape[1]
    bm = _tile(seq, 1024)
    h = _norm_mod(x2, gain, mod3, N_MOD, j0, j0 + 1, seq, BF16)
    wo_cast = _StripCast(w_out, layer, 0, f, fp, ((0, d, d),))
    a, ((wo,),) = _ffn_in(h, wg, wu, bm, _tile(fp, 512), (wo_cast,))
    bk = fp // 4 if (fp // 4) % LANE == 0 else fp
    return _res_matmul(a, wo, x2, mod3, j0 + 2, 0.5, seq, bm, _tile(d, 1024), bk, "ffn_out",
                       casts=next_casts)


def kernel(x, c, positions, ada_w, ada_b, norm_ffn1, ffn1_w_in, ffn1_w_out, norm_mix, w_in, b_forget, b_gate, diff_lambda, diff_subln, w_o_fox, w_o_diff, w_out, norm_ffn2, ffn2_w_in, ffn2_w_out, final_ada_w, final_ada_b, norm_final):
    batch, seq, d = x.shape
    depth = ada_w.shape[0]
    m = batch * seq
    n_fox = b_forget.shape[1]
    fox_w = n_fox * HEAD_DIM
    diff_w = w_o_diff.shape[1]
    n_diff = diff_w // (2 * HEAD_DIM)
    o_ff = 3 * fox_w
    o_dq = o_ff + n_fox
    n_rest = 3 * diff_w + 2 * d
    assert w_in.shape[2] == o_dq + n_rest
    assert n_fox <= LANE and seq % CHUNK == 0

    c_pad = jnp.pad(c, ((0, -batch % 8), (0, 0)))
    x2 = x.reshape(m, d)
    w_in_t = jnp.swapaxes(w_in, 1, 2)

    inv_freq = ROPE_THETA ** (-jnp.arange(0, ROPE_DIM, 2, dtype=F32) / ROPE_DIM)
    ang = positions.astype(F32).reshape(m, 1) * inv_freq
    cos, sin = jnp.cos(ang), jnp.sin(ang)
    cosf = jnp.concatenate([cos, cos, jnp.ones((m, HEAD_DIM - ROPE_DIM), F32)], axis=-1)
    sinf = jnp.concatenate([-sin, sin, jnp.zeros((m, HEAD_DIM - ROPE_DIM), F32)], axis=-1)

    bm = _tile(seq, 1024)
    tq = _tile(seq, 512)
    bn_p = _tile(math.gcd(fox_w, diff_w), 1024)
    for l in range(depth):
        lambda_init = 0.8 - 0.6 * math.exp(-0.3 * l)
        mod = _ada(c_pad, ada_w[l], ada_b[l])[:batch]
        mod3 = mod.reshape(batch * N_MOD, 1, d)

        wg, wu = _run_cast(_ffn_in_cast(ffn1_w_in, l))
        mixer_casts = (_StripCast(w_in_t, l, 0, o_ff, o_ff, ((0, d, d),)),
                       _StripCast(w_in_t, l, o_dq, n_rest, n_rest, ((0, d, d),)))
        x2, ((w_front,), (w_rest,)) = _ffn(x2, norm_ffn1[l], wg, wu, ffn1_w_out, l, mod3, 0, seq,
                                           mixer_casts)

        h = _norm_mod(x2, norm_mix[l], mod3, N_MOD, 3, 4, seq, BF16)
        w_ff_t = jnp.pad(w_in_t[l, o_ff:o_dq, :].astype(BF16), ((0, LANE - n_fox), (0, 0)))
        b_ff = jnp.pad(b_forget[l].astype(F32), (0, LANE - n_fox)).reshape(1, LANE)

        qk_f, ((wa,),) = _proj_plain(h, w_front, 0, 2 * fox_w, fox_w // bn_p, bm, bn_p,
                                     (_whole(w_o_fox, l),))
        qk_d, ((wb,),) = _proj_rope(h, w_rest, 0, 2 * diff_w, diff_w // bn_p, cosf, sinf, bm, bn_p,
                                    (_whole(w_o_diff, l),))
        gates, ((wg, wu),) = _proj_gate(h, w_rest, 3 * diff_w, 2 * d, b_gate[l], bm, bn_p,
                                        (_ffn_in_cast(ffn2_w_in, l),))
        vt_f, ((wo,),) = _vt_proj(h, w_front, 2 * fox_w, fox_w, bm, bn_p, tq, (_whole(w_out, l),))
        vt_d, _ = _vt_proj(h, w_rest, 2 * diff_w, diff_w, bm, bn_p, tq)
        kx = _forget(h, w_ff_t, b_ff, n_fox, batch, seq)

        ya = _fox(qk_f, kx, vt_f, batch, seq, n_fox, tq, math.gcd(n_fox, FOX_HEADS_PER_STEP))
        yb = _diff(qk_d, vt_d, diff_lambda[l].astype(F32), diff_subln[l], batch, seq, n_diff,
                   tq, math.gcd(n_diff, DIFF_HEADS_PER_STEP), lambda_init)
        merged = _merge(ya, yb, wa, wb, gates, bm, _tile(d, 1024))
        x2, _ = _res_matmul(merged, wo, x2, mod3, 5, 1.0, seq, bm, _tile(d, 1024), d, "mixer_out")

        x2, _ = _ffn(x2, norm_ffn2[l], wg, wu, ffn2_w_out, l, mod3, 6, seq, ())

    fmod = _ada(c_pad, final_ada_w, final_ada_b)[:batch]
    out = _norm_mod(x2, norm_final, fmod.reshape(batch * 2, 1, d), 2, 0, 1, seq, F32)
    return out.reshape(batch, seq, d)
```

```python
import functools
import math
from typing import NamedTuple

import numpy as np
import jax
import jax.numpy as jnp
from jax import lax
from jax.experimental import pallas as pl
from jax.experimental.pallas import tpu as pltpu

HEAD_DIM = 128
CHUNK = 64
ROPE_THETA = 500000.0
ROPE_DIM = HEAD_DIM // 4
ROPE_HALF = ROPE_DIM // 2
EPS = 1e-6
N_MOD = 9
LOG2E = 1.4426950408889634
MASKED = -1e30
LANE = 128
VMEM_LIMIT = 56 * 1024 * 1024
BF16_ROWS = 16
CAST_BLOCK_ELEMS = 2 * 1024 * 1024
HOSTED_STRIP_ELEMS = 1024 * 1024
N_SPLIT = 3
FOX_HEADS_PER_STEP = 4
DIFF_HEADS_PER_STEP = 2

F32 = jnp.float32
BF16 = jnp.bfloat16


def _params(*sem):
    return pltpu.CompilerParams(dimension_semantics=sem, vmem_limit_bytes=VMEM_LIMIT)


def _tile(n, pref):
    t = min(n, pref)
    while n % t:
        t //= 2
    return t


def _round_up(n, m):
    return (n + m - 1) // m * m


def _sigmoid(x):
    return 0.5 * jnp.tanh(0.5 * x) + 0.5


def _nt_dot(a, b):
    return lax.dot_general(a, b, (((1,), (1,)), ((), ())), preferred_element_type=F32)


class _StripCast(NamedTuple):
    src: jax.Array
    layer: int
    row0: int
    rows: int
    rows_out: int
    pieces: tuple


def _whole(src, layer):
    _, r, c = src.shape
    return _StripCast(src, layer, 0, r, r, ((0, c, c),))


def _strip_rows(cast, lo, hi):
    g = math.gcd(math.gcd(cast.row0, cast.rows), cast.rows_out)
    for rs in range(_round_up(max(lo, 1), BF16_ROWS), min(hi, g) + 1, BF16_ROWS):
        if g % rs == 0:
            return rs
    return None


def _linear_step(grid, ids):
    t = ids[0]
    for g, i in zip(grid[1:], ids[1:]):
        t = t * g + i
    return t


def _strip_specs(cast, rs, grid):
    first, n_valid, n_strips = cast.row0 // rs, cast.rows // rs, cast.rows_out // rs
    cols = cast.src.shape[2]
    src_spec = pl.BlockSpec(
        (None, rs, cols),
        lambda *ids: (cast.layer, first + jnp.minimum(_linear_step(grid, ids), n_valid - 1), 0))
    out_specs = [pl.BlockSpec((rs, nco), lambda *ids: (jnp.minimum(_linear_step(grid, ids), n_strips - 1), 0))
                 for _, _, nco in cast.pieces]
    out_shapes = [jax.ShapeDtypeStruct((cast.rows_out, nco), BF16) for _, _, nco in cast.pieces]
    return src_spec, out_specs, out_shapes


def _strip_cast_body(cast, rs, grid, src_ref, dst_refs):
    t = _linear_step(grid, [pl.program_id(a) for a in range(len(grid))])
    strip = src_ref[...]
    for (c0, n, nco), dst in zip(cast.pieces, dst_refs):
        piece = strip[:, c0:c0 + n].astype(dst.dtype)
        if cast.rows < cast.rows_out:
            piece = jnp.where(t < cast.rows // rs, piece, jnp.zeros_like(piece))
        if nco > n:
            dst[:, :n] = piece
            dst[:, n:] = jnp.zeros((rs, nco - n), dst.dtype)
        else:
            dst[...] = piece


def _run_cast(cast):
    rs = _strip_rows(cast, 1, max(CAST_BLOCK_ELEMS // cast.src.shape[2], BF16_ROWS))
    rs = rs or _strip_rows(cast, 1, cast.rows_out)
    if rs is None:
        w = cast.src[cast.layer, cast.row0:cast.row0 + cast.rows]
        return tuple(jnp.pad(w[:, c0:c0 + n].astype(BF16), ((0, cast.rows_out - cast.rows), (0, nco - n)))
                     for c0, n, nco in cast.pieces)
    grid = (cast.rows_out // rs,)
    src_spec, out_specs, out_shapes = _strip_specs(cast, rs, grid)

    def body(src_ref, *dst_refs):
        _strip_cast_body(cast, rs, grid, src_ref, dst_refs)

    return tuple(pl.pallas_call(
        body, grid=grid, in_specs=[src_spec], out_specs=out_specs, out_shape=out_shapes,
        compiler_params=_params("arbitrary"), name="weight_cast",
    )(cast.src))


def _hosted_call(body, grid, in_specs, out_spec, out_shape, inputs, name, scratch=(), casts=(),
                 in_place=None):
    steps = math.prod(grid)
    hosted, strip_rows = [], []
    for cast in casts:
        rs = _strip_rows(cast, pl.cdiv(cast.rows_out, steps), cast.rows_out)
        if rs is not None and rs * cast.src.shape[2] <= HOSTED_STRIP_ELEMS:
            hosted.append(cast)
            strip_rows.append(rs)
    specs = [_strip_specs(cast, rs, grid) for cast, rs in zip(hosted, strip_rows)]
    n_in, n_src = len(in_specs), len(hosted)
    kept = [] if in_place is None else [in_place]

    def kernel_body(*refs):
        ins, src_refs = refs[:n_in], refs[n_in:n_in + n_src]
        o_ref = refs[n_in + n_src + len(kept)]
        pos = n_in + n_src + len(kept) + 1
        dst_groups = []
        for cast in hosted:
            dst_groups.append(refs[pos:pos + len(cast.pieces)])
            pos += len(cast.pieces)
        body(*ins, o_ref, *refs[pos:])
        for cast, rs, src_ref, dst_refs in zip(hosted, strip_rows, src_refs, dst_groups):
            _strip_cast_body(cast, rs, grid, src_ref, dst_refs)

    outs = pl.pallas_call(
        kernel_body, grid=grid,
        in_specs=(list(in_specs) + [s[0] for s in specs]
                  + [pl.BlockSpec(memory_space=pl.ANY) for _ in kept]),
        out_specs=[out_spec] + [o for s in specs for o in s[1]],
        out_shape=[out_shape] + [o for s in specs for o in s[2]],
        scratch_shapes=list(scratch),
        input_output_aliases={n_in + n_src: 0} if kept else {},
        compiler_params=_params(*(("arbitrary",) * len(grid))), name=name,
    )(*inputs, *[cast.src for cast in hosted], *kept)

    results, pos = [], 1
    for cast in casts:
        if any(cast is hc for hc in hosted):
            results.append(tuple(outs[pos:pos + len(cast.pieces)]))
            pos += len(cast.pieces)
        else:
            results.append(_run_cast(cast))
    return outs[0], tuple(results)


def _ada_kernel(c_ref, w_ref, b_ref, o_ref):
    c = c_ref[...]
    sc = (c * jax.nn.sigmoid(c)).astype(BF16)
    o_ref[...] = jnp.dot(sc, w_ref[...].astype(BF16), preferred_element_type=F32) + b_ref[...]


def _ada(c_pad, w, b):
    d, n = w.shape
    tn = _tile(n, 512)
    return pl.pallas_call(
        _ada_kernel,
        grid=(n // tn,),
        in_specs=[
            pl.BlockSpec((c_pad.shape[0], d), lambda j: (0, 0)),
            pl.BlockSpec((d, tn), lambda j: (0, j)),
            pl.BlockSpec((1, tn), lambda j: (0, j)),
        ],
        out_specs=pl.BlockSpec((c_pad.shape[0], tn), lambda j: (0, j)),
        out_shape=jax.ShapeDtypeStruct((c_pad.shape[0], n), F32),
        compiler_params=_params("arbitrary"),
        name="ada_matvec",
    )(c_pad, w, b.reshape(1, n))


def _norm_mod_kernel(x_ref, g_ref, sh_ref, sc_ref, o_ref):
    x = x_ref[...]
    ms = jnp.mean(x * x, axis=-1, keepdims=True)
    y = x * lax.rsqrt(ms + EPS) * g_ref[...]
    o_ref[...] = (y * (1.0 + sc_ref[0]) + sh_ref[0]).astype(o_ref.dtype)


def _norm_mod(x2, gain, mod3, n_mod, shift_j, scale_j, seq, out_dtype):
    m, d = x2.shape
    tm = _tile(seq, 512)
    per_b = seq // tm
    return pl.pallas_call(
        _norm_mod_kernel,
        grid=(m // tm,),
        in_specs=[
            pl.BlockSpec((tm, d), lambda i: (i, 0)),
            pl.BlockSpec((1, d), lambda i: (0, 0)),
            pl.BlockSpec((1, 1, d), lambda i: ((i // per_b) * n_mod + shift_j, 0, 0)),
            pl.BlockSpec((1, 1, d), lambda i: ((i // per_b) * n_mod + scale_j, 0, 0)),
        ],
        out_specs=pl.BlockSpec((tm, d), lambda i: (i, 0)),
        out_shape=jax.ShapeDtypeStruct((m, d), out_dtype),
        compiler_params=_params("arbitrary"),
        name="norm_mod",
    )(x2, gain.reshape(1, d), mod3, mod3)


def _ffn_in_kernel(h_ref, wg_ref, wu_ref, o_ref):
    h = h_ref[...]
    g = jnp.dot(h, wg_ref[...], preferred_element_type=F32)
    u = jnp.dot(h, wu_ref[...], preferred_element_type=F32)
    o_ref[...] = (g * _sigmoid(g) * u).astype(o_ref.dtype)


def _ffn_in(h, wg, wu, bm, bn, casts, done=None):
    m, d = h.shape
    fp = wg.shape[1]
    first = 0 if done is None else 1
    if m // bm == first:
        return done, tuple(_run_cast(cast) for cast in casts)
    return _hosted_call(
        _ffn_in_kernel,
        (m // bm - first, fp // bn),
        [
            pl.BlockSpec((bm, d), lambda i, j: (i + first, 0)),
            pl.BlockSpec((d, bn), lambda i, j: (0, j)),
            pl.BlockSpec((d, bn), lambda i, j: (0, j)),
        ],
        pl.BlockSpec((bm, bn), lambda i, j: (i + first, j)),
        jax.ShapeDtypeStruct((m, fp), BF16),
        (h, wg, wu), "ffn_in", casts=casts, in_place=done)


def _ffn_in_first_kernel(h_ref, wg32_ref, wu32_ref, o_ref, wg_ref, wu_ref, *, n_valid):
    keep = pl.program_id(0) < n_valid
    wg = jnp.where(keep, wg32_ref[...], 0.0).astype(wg_ref.dtype)
    wu = jnp.where(keep, wu32_ref[...], 0.0).astype(wu_ref.dtype)
    wg_ref[...] = wg
    wu_ref[...] = wu
    h = h_ref[...]
    g = jnp.dot(h, wg, preferred_element_type=F32)
    u = jnp.dot(h, wu, preferred_element_type=F32)
    o_ref[...] = (g * _sigmoid(g) * u).astype(o_ref.dtype)


def _ffn_in_first(h, w_in, layer, bm):
    m, d = h.shape
    f = w_in.shape[2] // 2
    fp = _ffn_width(f)
    bn = _tile(math.gcd(f, fp), 256)
    nv = f // bn
    return pl.pallas_call(
        functools.partial(_ffn_in_first_kernel, n_valid=nv),
        grid=(fp // bn,),
        in_specs=[
            pl.BlockSpec((bm, d), lambda j: (0, 0)),
            pl.BlockSpec((None, d, bn), lambda j: (layer, 0, jnp.minimum(j, nv - 1))),
            pl.BlockSpec((None, d, bn), lambda j: (layer, 0, nv + jnp.minimum(j, nv - 1))),
        ],
        out_specs=[
            pl.BlockSpec((bm, bn), lambda j: (0, j)),
            pl.BlockSpec((d, bn), lambda j: (0, j)),
            pl.BlockSpec((d, bn), lambda j: (0, j)),
        ],
        out_shape=[jax.ShapeDtypeStruct((m, fp), BF16),
                   jax.ShapeDtypeStruct((d, fp), BF16), jax.ShapeDtypeStruct((d, fp), BF16)],
        compiler_params=_params("arbitrary"),
        name="ffn_in_first",
    )(h, w_in, w_in)


def _res_matmul_kernel(a_ref, w_ref, res_ref, gate_ref, o_ref, acc_ref, *, coef, nk):
    k = pl.program_id(2)

    def partial_product():
        return jnp.dot(a_ref[...], w_ref[...], preferred_element_type=F32)

    def finish(total):
        o_ref[...] = res_ref[...] + (coef * gate_ref[0]) * total

    if nk == 1:
        finish(partial_product())
        return

    @pl.when(k == 0)
    def _():
        acc_ref[...] = partial_product()

    @pl.when((k > 0) & (k < nk - 1))
    def _():
        acc_ref[...] += partial_product()

    @pl.when(k == nk - 1)
    def _():
        finish(acc_ref[...] + partial_product())


def _res_matmul(a, w, res, mod3, gate_j, coef, seq, bm, bn, bk, name, casts=()):
    m, kdim = a.shape
    n = w.shape[1]
    per_b = seq // bm
    return _hosted_call(
        functools.partial(_res_matmul_kernel, coef=coef, nk=kdim // bk),
        (m // bm, n // bn, kdim // bk),
        [
            pl.BlockSpec((bm, bk), lambda i, j, k: (i, k)),
            pl.BlockSpec((bk, bn), lambda i, j, k: (k, j)),
            pl.BlockSpec((bm, bn), lambda i, j, k: (i, j)),
            pl.BlockSpec((1, 1, bn), lambda i, j, k: ((i // per_b) * N_MOD + gate_j, 0, j)),
        ],
        pl.BlockSpec((bm, bn), lambda i, j, k: (i, j)),
        jax.ShapeDtypeStruct((m, n), F32),
        (a, w, res, mod3), name, scratch=[pltpu.VMEM((bm, bn), F32)], casts=casts)


QSCALE = HEAD_DIM ** -0.5 * LOG2E


def _proj_call(body, h, wt, row0, n, bm, bn, extra_in, extra_specs, name, casts):
    m, d = h.shape
    b0 = row0 // bn
    return _hosted_call(
        body,
        (m // bm, n // bn),
        [
            pl.BlockSpec((bm, d), lambda i, j: (i, 0)),
            pl.BlockSpec((bn, d), lambda i, j: (b0 + j, 0)),
        ] + extra_specs,
        pl.BlockSpec((bm, bn), lambda i, j: (i, j)),
        jax.ShapeDtypeStruct((m, n), BF16),
        (h, wt, *extra_in), name, casts=casts)


def _block_scale(n_scaled):
    return jnp.where(pl.program_id(1) < n_scaled, QSCALE, 1.0).astype(F32)


def _proj_plain_kernel(h_ref, wt_ref, o_ref, *, n_scaled):
    acc = _nt_dot(h_ref[...], wt_ref[...])
    o_ref[...] = (acc * _block_scale(n_scaled)).astype(o_ref.dtype)


def _proj_rope_kernel(h_ref, wt_ref, cos_ref, sin_ref, o_ref, *, n_scaled):
    acc = _nt_dot(h_ref[...], wt_ref[...])
    cosf = cos_ref[...] * _block_scale(n_scaled)
    sinf = sin_ref[...] * _block_scale(n_scaled)
    lane = lax.broadcasted_iota(jnp.int32, cosf.shape, 1)
    for c in range(acc.shape[1] // HEAD_DIM):
        cols = slice(c * HEAD_DIM, (c + 1) * HEAD_DIM)
        t = acc[:, cols]
        partner = jnp.where(lane < ROPE_HALF,
                            pltpu.roll(t, HEAD_DIM - ROPE_HALF, 1),
                            pltpu.roll(t, ROPE_HALF, 1))
        o_ref[:, cols] = (t * cosf + partner * sinf).astype(o_ref.dtype)


def _proj_gate_kernel(h_ref, wt_ref, b_ref, o_ref):
    acc = _nt_dot(h_ref[...], wt_ref[...])
    o_ref[...] = _sigmoid(acc + b_ref[...]).astype(o_ref.dtype)


def _proj_plain(h, wt, row0, n, n_scaled, bm, bn, casts):
    return _proj_call(functools.partial(_proj_plain_kernel, n_scaled=n_scaled),
                      h, wt, row0, n, bm, bn, [], [], "proj_plain", casts)


def _proj_rope(h, wt, row0, n, n_scaled, cosf, sinf, bm, bn, casts):
    table = pl.BlockSpec((bm, HEAD_DIM), lambda i, j: (i, 0))
    return _proj_call(functools.partial(_proj_rope_kernel, n_scaled=n_scaled),
                      h, wt, row0, n, bm, bn, [cosf, sinf], [table, table], "proj_rope", casts)


def _proj_gate(h, wt, row0, n, bias, bm, bn, casts):
    return _proj_call(_proj_gate_kernel, h, wt, row0, n, bm, bn, [bias.reshape(1, n)],
                      [pl.BlockSpec((1, bn), lambda i, j: (0, j))], "proj_gate", casts)


def _vt_proj_kernel(wt_ref, h_ref, o_ref, *, tk):
    r = _nt_dot(wt_ref[...], h_ref[...])
    for t in range(o_ref.shape[0]):
        o_ref[t] = r[:, t * tk:(t + 1) * tk].astype(o_ref.dtype)


def _vt_proj(h, wt, row0, n, bm, bn, tk, casts=()):
    m, d = h.shape
    b0 = row0 // bn
    return _hosted_call(
        functools.partial(_vt_proj_kernel, tk=tk),
        (m // bm, n // bn),
        [
            pl.BlockSpec((bn, d), lambda i, j: (b0 + j, 0)),
            pl.BlockSpec((bm, d), lambda i, j: (i, 0)),
        ],
        pl.BlockSpec((bm // tk, bn, tk), lambda i, j: (i, j, 0)),
        jax.ShapeDtypeStruct((m // tk, n, tk), BF16),
        (wt, h), "proj_vt", casts=casts)


def _forget_kernel(h_ref, w_ref, b_ref, e_ref, o_ref, carry_ref):
    @pl.when(pl.program_id(1) == 0)
    def _():
        carry_ref[...] = jnp.zeros_like(carry_ref)

    ff = _nt_dot(h_ref[...], w_ref[...]) + b_ref[...]
    logf = jnp.minimum(ff, 0.0) - jnp.log1p(jnp.exp(-jnp.abs(ff)))
    tm = logf.shape[0]
    row = lax.broadcasted_iota(jnp.int32, (tm, tm), 0)
    col = lax.broadcasted_iota(jnp.int32, (tm, tm), 1)
    tri = (row >= col).astype(F32)
    cum = jnp.dot(tri, logf, preferred_element_type=F32,
                  precision=lax.Precision.HIGHEST) + carry_ref[...]
    carry_ref[...] = cum[tm - 1:tm, :]

    rest = cum * (-LOG2E)
    pieces = []
    for _ in range(N_SPLIT):
        p = rest.astype(BF16)
        pieces.append(p)
        rest = rest - p.astype(F32)
    spread = jnp.dot(jnp.concatenate(pieces, axis=1), e_ref[...], preferred_element_type=F32)
    o_ref[...] = spread.astype(o_ref.dtype)


def _forget(h, w_ff_t, b_ff, n_heads, batch, seq):
    m, d = h.shape
    tm = _tile(seq, 512)
    per_b = seq // tm
    route = np.zeros((N_SPLIT * LANE, n_heads * HEAD_DIM), np.float32)
    for p in range(N_SPLIT):
        for hh in range(n_heads):
            route[p * LANE + hh, hh * HEAD_DIM + p] = 1.0
    return pl.pallas_call(
        _forget_kernel,
        grid=(batch, per_b),
        in_specs=[
            pl.BlockSpec((tm, d), lambda b, i: (b * per_b + i, 0)),
            pl.BlockSpec((LANE, d), lambda b, i: (0, 0)),
            pl.BlockSpec((1, LANE), lambda b, i: (0, 0)),
            pl.BlockSpec(route.shape, lambda b, i: (0, 0)),
        ],
        out_specs=pl.BlockSpec((tm, n_heads * HEAD_DIM), lambda b, i: (b * per_b + i, 0)),
        out_shape=jax.ShapeDtypeStruct((m, n_heads * HEAD_DIM), BF16),
        scratch_shapes=[pltpu.VMEM((1, LANE), F32)],
        compiler_params=_params("arbitrary", "arbitrary"),
        name="forget_cumsum",
    )(h, w_ff_t, b_ff, jnp.asarray(route, BF16))


def _online_softmax_step(st, vt, m, l, acc):
    m_new = jnp.maximum(m, jnp.max(st, axis=0, keepdims=True))
    alpha = jnp.exp2(m - m_new)
    p = jnp.exp2(st - m_new)
    l_new = alpha * l + jnp.sum(p, axis=0, keepdims=True)
    acc_new = alpha * acc + jnp.dot(vt, p.astype(vt.dtype), preferred_element_type=F32)
    return m_new, l_new, acc_new


def _flash_scratch(n_chains, tq, dv):
    return ([pltpu.VMEM((2, tq, tq), F32) for _ in range(n_chains)]
            + [pltpu.VMEM((n_chains, 1, tq), F32), pltpu.VMEM((n_chains, 1, tq), F32),
               pltpu.VMEM((n_chains, dv, tq), F32)])


def _flash_loop(qi, n_chains, qk_fn, vt_fn, visible_fn, st_refs, m_ref, l_ref, acc_ref):
    def produce(kj, slot, diagonal):
        for c in range(n_chains):
            st = qk_fn(c, kj)
            if diagonal:
                st = jnp.where(visible_fn(), st, MASKED)
            st_refs[c][slot] = st

    def consume(c, kj, slot):
        m, l, acc = _online_softmax_step(st_refs[c][slot], vt_fn(c, kj), m_ref[c], l_ref[c], acc_ref[c])
        m_ref[c] = m
        l_ref[c] = l
        acc_ref[c] = acc

    def step(kj, slot, next_is_diagonal):
        for c in range(n_chains):
            st = qk_fn(c, kj + 1)
            if next_is_diagonal:
                st = jnp.where(visible_fn(), st, MASKED)
            st_refs[c][1 - slot] = st
            consume(c, kj, slot)

    def run_if(cond, fn):
        def body(_, carry):
            fn()
            return carry
        lax.fori_loop(0, cond.astype(jnp.int32), body, 0)

    tq = m_ref.shape[-1]
    m_ref[...] = jnp.full(m_ref.shape, MASKED, F32)
    l_ref[...] = jnp.zeros(l_ref.shape, F32)
    acc_ref[...] = jnp.zeros(acc_ref.shape, F32)

    run_if(qi == 0, lambda: produce(0, 0, True))
    run_if(qi > 0, lambda: produce(0, 0, False))

    n_pairs = jnp.maximum(qi - 1, 0) // 2
    kj0 = 2 * n_pairs

    def pair(i, carry):
        step(2 * i, 0, False)
        step(2 * i + 1, 1, False)
        return carry

    lax.fori_loop(0, n_pairs, pair, 0)
    two_left = (qi > 0) & (qi - kj0 == 2)
    one_left = (qi > 0) & (qi - kj0 == 1)
    run_if(two_left, lambda: step(kj0, 0, False))
    run_if(two_left, lambda: step(kj0 + 1, 1, True))
    run_if(one_left, lambda: step(kj0, 0, True))
    run_if(one_left, lambda: [consume(c, qi, 1) for c in range(n_chains)])
    run_if(jnp.logical_not(one_left), lambda: [consume(c, qi, 0) for c in range(n_chains)])


def _fox_kernel(q_ref, k_ref, kx_ref, vt_ref, o_ref, *scratch, tq, nh):
    qi = pl.program_id(2)
    st_refs, (m_ref, l_ref, acc_ref) = scratch[:nh], scratch[nh:]
    lane = lax.broadcasted_iota(jnp.int32, (tq, HEAD_DIM), 1)
    ones = (lane < N_SPLIT).astype(q_ref.dtype)

    def head(a):
        return slice(a * HEAD_DIM, (a + 1) * HEAD_DIM)

    def qk(a, kj):
        ks = pl.multiple_of(kj * tq, tq)
        q_aug = jnp.concatenate([q_ref[:, head(a)], ones], axis=1)
        k_aug = jnp.concatenate([k_ref[pl.ds(ks, tq), head(a)], kx_ref[pl.ds(ks, tq), head(a)]], axis=1)
        return _nt_dot(k_aug, q_aug)

    def visible():
        key = lax.broadcasted_iota(jnp.int32, (tq, tq), 0)
        qry = lax.broadcasted_iota(jnp.int32, (tq, tq), 1)
        return key <= qry

    _flash_loop(qi, nh, qk, lambda a, kj: vt_ref[kj, head(a), :], visible,
                st_refs, m_ref, l_ref, acc_ref)
    for a in range(nh):
        o_ref[:, head(a)] = (acc_ref[a] / l_ref[a]).T.astype(o_ref.dtype)


def _fox(qk, kx, vt, batch, seq, n_heads, tq, nh):
    m = qk.shape[0]
    nq = seq // tq
    gw = nh * HEAD_DIM
    ng = n_heads // nh
    return pl.pallas_call(
        functools.partial(_fox_kernel, tq=tq, nh=nh),
        grid=(batch, ng, nq),
        in_specs=[
            pl.BlockSpec((tq, gw), lambda b, g, i: (b * nq + i, g)),
            pl.BlockSpec((seq, gw), lambda b, g, i: (b, ng + g)),
            pl.BlockSpec((seq, gw), lambda b, g, i: (b, g)),
            pl.BlockSpec((nq, gw, tq), lambda b, g, i: (b, g, 0)),
        ],
        out_specs=pl.BlockSpec((tq, gw), lambda b, g, i: (b * nq + i, g)),
        out_shape=jax.ShapeDtypeStruct((m, n_heads * HEAD_DIM), BF16),
        scratch_shapes=_flash_scratch(nh, tq, HEAD_DIM),
        compiler_params=_params("arbitrary", "arbitrary", "arbitrary"),
        name="fox_attention",
    )(qk, qk, kx, vt)


def _diff_kernel(q_ref, k_ref, vt_ref, lam_ref, g_ref, o_ref, *scratch, tq, nh, lambda_init):
    qi = pl.program_id(2)
    n_maps = 2 * nh
    st_refs, (m_ref, l_ref, acc_ref) = scratch[:n_maps], scratch[n_maps:]
    hw = 2 * HEAD_DIM

    def cols(c):
        return slice(c * HEAD_DIM, (c + 1) * HEAD_DIM)

    def qk(c, kj):
        ks = pl.multiple_of(kj * tq, tq)
        return _nt_dot(k_ref[pl.ds(ks, tq), cols(c)], q_ref[:, cols(c)])

    def visible():
        key = lax.broadcasted_iota(jnp.int32, (tq, tq), 0) // CHUNK
        qry = lax.broadcasted_iota(jnp.int32, (tq, tq), 1) // CHUNK
        return key <= qry

    _flash_loop(qi, n_maps, qk, lambda c, kj: vt_ref[kj, (c // 2) * hw:(c // 2 + 1) * hw, :],
                visible, st_refs, m_ref, l_ref, acc_ref)

    lp = lam_ref[...]
    lam = (jnp.exp(jnp.sum(lp[0:1] * lp[1:2], axis=-1, keepdims=True))
           - jnp.exp(jnp.sum(lp[2:3] * lp[3:4], axis=-1, keepdims=True)) + lambda_init)
    for a in range(nh):
        y = acc_ref[2 * a] / l_ref[2 * a] - lam * (acc_ref[2 * a + 1] / l_ref[2 * a + 1])
        ms = jnp.mean(y * y, axis=0, keepdims=True)
        y = (y * lax.rsqrt(ms + EPS)).T * g_ref[...]
        o_ref[:, a * hw:(a + 1) * hw] = (y * (1.0 - lambda_init)).astype(o_ref.dtype)


def _diff(qk, vt, lam_p, subln, batch, seq, n_heads, tq, nh, lambda_init):
    m = qk.shape[0]
    nq = seq // tq
    hw = 2 * HEAD_DIM
    gw = nh * hw
    ng = n_heads // nh
    return pl.pallas_call(
        functools.partial(_diff_kernel, tq=tq, nh=nh, lambda_init=lambda_init),
        grid=(batch, ng, nq),
        in_specs=[
            pl.BlockSpec((tq, gw), lambda b, g, i: (b * nq + i, g)),
            pl.BlockSpec((seq, gw), lambda b, g, i: (b, ng + g)),
            pl.BlockSpec((nq, gw, tq), lambda b, g, i: (b, g, 0)),
            pl.BlockSpec((4, HEAD_DIM), lambda b, g, i: (0, 0)),
            pl.BlockSpec((1, hw), lambda b, g, i: (0, 0)),
        ],
        out_specs=pl.BlockSpec((tq, gw), lambda b, g, i: (b * nq + i, g)),
        out_shape=jax.ShapeDtypeStruct((m, n_heads * hw), BF16),
        scratch_shapes=_flash_scratch(2 * nh, tq, hw),
        compiler_params=_params("arbitrary", "arbitrary", "arbitrary"),
        name="diff_attention",
    )(qk, qk, vt, lam_p, subln.reshape(1, hw))


def _merge_kernel(ya_ref, yb_ref, wa_ref, wb_ref, sa_ref, sb_ref, o_ref):
    pa = jnp.dot(ya_ref[...], wa_ref[...], preferred_element_type=F32)
    pb = jnp.dot(yb_ref[...], wb_ref[...], preferred_element_type=F32)
    o_ref[...] = (sa_ref[...].astype(F32) * pa + sb_ref[...].astype(F32) * pb).astype(o_ref.dtype)


def _merge(ya, yb, wa, wb, gates, bm, bn):
    m = ya.shape[0]
    d = wa.shape[1]
    nb = d // bn
    return pl.pallas_call(
        _merge_kernel,
        grid=(m // bm, nb),
        in_specs=[
            pl.BlockSpec((bm, ya.shape[1]), lambda i, j: (i, 0)),
            pl.BlockSpec((bm, yb.shape[1]), lambda i, j: (i, 0)),
            pl.BlockSpec((wa.shape[0], bn), lambda i, j: (0, j)),
            pl.BlockSpec((wb.shape[0], bn), lambda i, j: (0, j)),
            pl.BlockSpec((bm, bn), lambda i, j: (i, j)),
            pl.BlockSpec((bm, bn), lambda i, j: (i, nb + j)),
        ],
        out_specs=pl.BlockSpec((bm, bn), lambda i, j: (i, j)),
        out_shape=jax.ShapeDtypeStruct((m, d), BF16),
        compiler_params=_params("arbitrary", "arbitrary"),
        name="gated_merge",
    )(ya, yb, wa, wb, gates, gates)


def _ffn_width(f):
    return _round_up(f, 1024) if f >= 1024 else f


def _ffn_in_cast(w_in, layer):
    _, d, f2 = w_in.shape
    f = f2 // 2
    return _StripCast(w_in, layer, 0, d, d, ((0, f, _ffn_width(f)), (f, f, _ffn_width(f))))


def _ffn(x2, gain, w_in, pre_cast, w_out, layer, mod3, j0, seq, next_casts):
    m, d = x2.shape
    f = w_out.shape[1]
    fp = _ffn_width(f)
    bm = _tile(seq, 1024)
    h = _norm_mod(x2, gain, mod3, N_MOD, j0, j0 + 1, seq, BF16)
    wo_cast = _StripCast(w_out, layer, 0, f, fp, ((0, d, d),))
    if pre_cast is None:
        done, wg, wu = _ffn_in_first(h, w_in, layer, bm)
    else:
        done, (wg, wu) = None, pre_cast
    a, ((wo,),) = _ffn_in(h, wg, wu, bm, _tile(fp, 512), (wo_cast,), done)
    bk = fp // 4 if (fp // 4) % LANE == 0 else fp
    return _res_matmul(a, wo, x2, mod3, j0 + 2, 0.5, seq, bm, _tile(d, 1024), bk, "ffn_out",
                       casts=next_casts)


def kernel(x, c, positions, ada_w, ada_b, norm_ffn1, ffn1_w_in, ffn1_w_out, norm_mix, w_in, b_forget, b_gate, diff_lambda, diff_subln, w_o_fox, w_o_diff, w_out, norm_ffn2, ffn2_w_in, ffn2_w_out, final_ada_w, final_ada_b, norm_final):
    batch, seq, d = x.shape
    depth = ada_w.shape[0]
    m = batch * seq
    n_fox = b_forget.shape[1]
    fox_w = n_fox * HEAD_DIM
    diff_w = w_o_diff.shape[1]
    n_diff = diff_w // (2 * HEAD_DIM)
    o_ff = 3 * fox_w
    o_dq = o_ff + n_fox
    n_rest = 3 * diff_w + 2 * d
    assert w_in.shape[2] == o_dq + n_rest
    assert n_fox <= LANE and seq % CHUNK == 0

    c_pad = jnp.pad(c, ((0, -batch % 8), (0, 0)))
    x2 = x.reshape(m, d)
    w_in_t = jnp.swapaxes(w_in, 1, 2)

    inv_freq = ROPE_THETA ** (-jnp.arange(0, ROPE_DIM, 2, dtype=F32) / ROPE_DIM)
    ang = positions.astype(F32).reshape(m, 1) * inv_freq
    cos, sin = jnp.cos(ang), jnp.sin(ang)
    cosf = jnp.concatenate([cos, cos, jnp.ones((m, HEAD_DIM - ROPE_DIM), F32)], axis=-1)
    sinf = jnp.concatenate([-sin, sin, jnp.zeros((m, HEAD_DIM - ROPE_DIM), F32)], axis=-1)

    bm = _tile(seq, 1024)
    tq = _tile(seq, 512)
    bn_p = _tile(math.gcd(fox_w, diff_w), 1024)
    for l in range(depth):
        lambda_init = 0.8 - 0.6 * math.exp(-0.3 * l)
        mod = _ada(c_pad, ada_w[l], ada_b[l])[:batch]
        mod3 = mod.reshape(batch * N_MOD, 1, d)

        mixer_casts = (_StripCast(w_in_t, l, 0, o_ff, o_ff, ((0, d, d),)),
                       _StripCast(w_in_t, l, o_dq, n_rest, n_rest, ((0, d, d),)))
        x2, ((w_front,), (w_rest,)) = _ffn(x2, norm_ffn1[l], ffn1_w_in, None, ffn1_w_out, l, mod3,
                                           0, seq, mixer_casts)

        h = _norm_mod(x2, norm_mix[l], mod3, N_MOD, 3, 4, seq, BF16)
        w_ff_t = jnp.pad(w_in_t[l, o_ff:o_dq, :].astype(BF16), ((0, LANE - n_fox), (0, 0)))
        b_ff = jnp.pad(b_forget[l].astype(F32), (0, LANE - n_fox)).reshape(1, LANE)

        qk_f, ((wa,),) = _proj_plain(h, w_front, 0, 2 * fox_w, fox_w // bn_p, bm, bn_p,
                                     (_whole(w_o_fox, l),))
        qk_d, ((wb,),) = _proj_rope(h, w_rest, 0, 2 * diff_w, diff_w // bn_p, cosf, sinf, bm, bn_p,
                                    (_whole(w_o_diff, l),))
        gates, ((wg, wu),) = _proj_gate(h, w_rest, 3 * diff_w, 2 * d, b_gate[l], bm, bn_p,
                                        (_ffn_in_cast(ffn2_w_in, l),))
        vt_f, ((wo,),) = _vt_proj(h, w_front, 2 * fox_w, fox_w, bm, bn_p, tq, (_whole(w_out, l),))
        vt_d, _ = _vt_proj(h, w_rest, 2 * diff_w, diff_w, bm, bn_p, tq)
        kx = _forget(h, w_ff_t, b_ff, n_fox, batch, seq)

        ya = _fox(qk_f, kx, vt_f, batch, seq, n_fox, tq, math.gcd(n_fox, FOX_HEADS_PER_STEP))
        yb = _diff(qk_d, vt_d, diff_lambda[l].astype(F32), diff_subln[l], batch, seq, n_diff,
                   tq, math.gcd(n_diff, DIFF_HEADS_PER_STEP), lambda_init)
        merged = _merge(ya, yb, wa, wb, gates, bm, _tile(d, 1024))
        x2, _ = _res_matmul(merged, wo, x2, mod3, 5, 1.0, seq, bm, _tile(d, 1024), d, "mixer_out")

        x2, _ = _ffn(x2, norm_ffn2[l], ffn2_w_in, (wg, wu), ffn2_w_out, l, mod3, 6, seq, ())

    fmod = _ada(c_pad, final_ada_w, final_ada_b)[:batch]
    out = _norm_mod(x2, norm_final, fmod.reshape(batch * 2, 1, d), 2, 0, 1, seq, F32)
    return out.reshape(batch, seq, d)
```

```python
import functools
import math
from typing import NamedTuple

import numpy as np
import jax
import jax.numpy as jnp
from jax import lax
from jax.experimental import pallas as pl
from jax.experimental.pallas import tpu as pltpu

HEAD_DIM = 128
CHUNK = 64
ROPE_THETA = 500000.0
ROPE_DIM = HEAD_DIM // 4
ROPE_HALF = ROPE_DIM // 2
EPS = 1e-6
N_MOD = 9
LOG2E = 1.4426950408889634
MASKED = -1e30
LANE = 128
VMEM_LIMIT = 56 * 1024 * 1024
BF16_ROWS = 16
CAST_BLOCK_ELEMS = 2 * 1024 * 1024
HOSTED_STRIP_ELEMS = 1024 * 1024
N_SPLIT = 3
FORGET_SUBTILE = 512
FOX_HEADS_PER_STEP = 4
DIFF_HEADS_PER_STEP = 2

F32 = jnp.float32
BF16 = jnp.bfloat16


def _params(*sem):
    return pltpu.CompilerParams(dimension_semantics=sem, vmem_limit_bytes=VMEM_LIMIT)


def _tile(n, pref):
    t = min(n, pref)
    while n % t:
        t //= 2
    return t


def _round_up(n, m):
    return (n + m - 1) // m * m


def _sigmoid(x):
    return 0.5 * jnp.tanh(0.5 * x) + 0.5


def _nt_dot(a, b):
    return lax.dot_general(a, b, (((1,), (1,)), ((), ())), preferred_element_type=F32)


class _StripCast(NamedTuple):
    src: jax.Array
    layer: int
    row0: int
    rows: int
    rows_out: int
    pieces: tuple


def _whole(src, layer):
    _, r, c = src.shape
    return _StripCast(src, layer, 0, r, r, ((0, c, c),))


def _strip_rows(cast, lo, hi):
    g = math.gcd(math.gcd(cast.row0, cast.rows), cast.rows_out)
    for rs in range(_round_up(max(lo, 1), BF16_ROWS), min(hi, g) + 1, BF16_ROWS):
        if g % rs == 0:
            return rs
    return None


def _linear_step(grid, ids):
    t = ids[0]
    for g, i in zip(grid[1:], ids[1:]):
        t = t * g + i
    return t


def _strip_specs(cast, rs, grid):
    first, n_valid, n_strips = cast.row0 // rs, cast.rows // rs, cast.rows_out // rs
    cols = cast.src.shape[2]
    src_spec = pl.BlockSpec(
        (None, rs, cols),
        lambda *ids: (cast.layer, first + jnp.minimum(_linear_step(grid, ids), n_valid - 1), 0))
    out_specs = [pl.BlockSpec((rs, nco), lambda *ids: (jnp.minimum(_linear_step(grid, ids), n_strips - 1), 0))
                 for _, _, nco in cast.pieces]
    out_shapes = [jax.ShapeDtypeStruct((cast.rows_out, nco), BF16) for _, _, nco in cast.pieces]
    return src_spec, out_specs, out_shapes


def _strip_cast_body(cast, rs, grid, src_ref, dst_refs):
    t = _linear_step(grid, [pl.program_id(a) for a in range(len(grid))])
    strip = src_ref[...]
    for (c0, n, nco), dst in zip(cast.pieces, dst_refs):
        piece = strip[:, c0:c0 + n].astype(dst.dtype)
        if cast.rows < cast.rows_out:
            piece = jnp.where(t < cast.rows // rs, piece, jnp.zeros_like(piece))
        if nco > n:
            dst[:, :n] = piece
            dst[:, n:] = jnp.zeros((rs, nco - n), dst.dtype)
        else:
            dst[...] = piece


def _run_cast(cast):
    rs = _strip_rows(cast, 1, max(CAST_BLOCK_ELEMS // cast.src.shape[2], BF16_ROWS))
    rs = rs or _strip_rows(cast, 1, cast.rows_out)
    if rs is None:
        w = cast.src[cast.layer, cast.row0:cast.row0 + cast.rows]
        return tuple(jnp.pad(w[:, c0:c0 + n].astype(BF16), ((0, cast.rows_out - cast.rows), (0, nco - n)))
                     for c0, n, nco in cast.pieces)
    grid = (cast.rows_out // rs,)
    src_spec, out_specs, out_shapes = _strip_specs(cast, rs, grid)

    def body(src_ref, *dst_refs):
        _strip_cast_body(cast, rs, grid, src_ref, dst_refs)

    return tuple(pl.pallas_call(
        body, grid=grid, in_specs=[src_spec], out_specs=out_specs, out_shape=out_shapes,
        compiler_params=_params("arbitrary"), name="weight_cast",
    )(cast.src))


def _hosted_call(body, grid, in_specs, out_spec, out_shape, inputs, name, scratch=(), casts=(),
                 in_place=None):
    steps = math.prod(grid)
    hosted, strip_rows = [], []
    for cast in casts:
        rs = _strip_rows(cast, pl.cdiv(cast.rows_out, steps), cast.rows_out)
        if rs is not None and rs * cast.src.shape[2] <= HOSTED_STRIP_ELEMS:
            hosted.append(cast)
            strip_rows.append(rs)
    specs = [_strip_specs(cast, rs, grid) for cast, rs in zip(hosted, strip_rows)]
    n_in, n_src = len(in_specs), len(hosted)
    kept = [] if in_place is None else [in_place]

    def kernel_body(*refs):
        ins, src_refs = refs[:n_in], refs[n_in:n_in + n_src]
        o_ref = refs[n_in + n_src + len(kept)]
        pos = n_in + n_src + len(kept) + 1
        dst_groups = []
        for cast in hosted:
            dst_groups.append(refs[pos:pos + len(cast.pieces)])
            pos += len(cast.pieces)
        body(*ins, o_ref, *refs[pos:])
        for cast, rs, src_ref, dst_refs in zip(hosted, strip_rows, src_refs, dst_groups):
            _strip_cast_body(cast, rs, grid, src_ref, dst_refs)

    outs = pl.pallas_call(
        kernel_body, grid=grid,
        in_specs=(list(in_specs) + [s[0] for s in specs]
                  + [pl.BlockSpec(memory_space=pl.ANY) for _ in kept]),
        out_specs=[out_spec] + [o for s in specs for o in s[1]],
        out_shape=[out_shape] + [o for s in specs for o in s[2]],
        scratch_shapes=list(scratch),
        input_output_aliases={n_in + n_src: 0} if kept else {},
        compiler_params=_params(*(("arbitrary",) * len(grid))), name=name,
    )(*inputs, *[cast.src for cast in hosted], *kept)

    results, pos = [], 1
    for cast in casts:
        if any(cast is hc for hc in hosted):
            results.append(tuple(outs[pos:pos + len(cast.pieces)]))
            pos += len(cast.pieces)
        else:
            results.append(_run_cast(cast))
    return outs[0], tuple(results)


def _ada_kernel(c_ref, w_ref, b_ref, o_ref):
    c = c_ref[...]
    sc = (c * jax.nn.sigmoid(c)).astype(BF16)
    o_ref[...] = jnp.dot(sc, w_ref[...].astype(BF16), preferred_element_type=F32) + b_ref[...]


def _ada(c_pad, w, b):
    d, n = w.shape
    tn = _tile(n, 512)
    return pl.pallas_call(
        _ada_kernel,
        grid=(n // tn,),
        in_specs=[
            pl.BlockSpec((c_pad.shape[0], d), lambda j: (0, 0)),
            pl.BlockSpec((d, tn), lambda j: (0, j)),
            pl.BlockSpec((1, tn), lambda j: (0, j)),
        ],
        out_specs=pl.BlockSpec((c_pad.shape[0], tn), lambda j: (0, j)),
        out_shape=jax.ShapeDtypeStruct((c_pad.shape[0], n), F32),
        compiler_params=_params("arbitrary"),
        name="ada_matvec",
    )(c_pad, w, b.reshape(1, n))


def _norm_mod_kernel(x_ref, g_ref, sh_ref, sc_ref, o_ref):
    x = x_ref[...]
    ms = jnp.mean(x * x, axis=-1, keepdims=True)
    y = x * lax.rsqrt(ms + EPS) * g_ref[...]
    o_ref[...] = (y * (1.0 + sc_ref[0]) + sh_ref[0]).astype(o_ref.dtype)


def _norm_mod(x2, gain, mod3, n_mod, shift_j, scale_j, seq, out_dtype):
    m, d = x2.shape
    tm = _tile(seq, 512)
    per_b = seq // tm
    return pl.pallas_call(
        _norm_mod_kernel,
        grid=(m // tm,),
        in_specs=[
            pl.BlockSpec((tm, d), lambda i: (i, 0)),
            pl.BlockSpec((1, d), lambda i: (0, 0)),
            pl.BlockSpec((1, 1, d), lambda i: ((i // per_b) * n_mod + shift_j, 0, 0)),
            pl.BlockSpec((1, 1, d), lambda i: ((i // per_b) * n_mod + scale_j, 0, 0)),
        ],
        out_specs=pl.BlockSpec((tm, d), lambda i: (i, 0)),
        out_shape=jax.ShapeDtypeStruct((m, d), out_dtype),
        compiler_params=_params("arbitrary"),
        name="norm_mod",
    )(x2, gain.reshape(1, d), mod3, mod3)


def _ffn_in_kernel(h_ref, wg_ref, wu_ref, o_ref):
    h = h_ref[...]
    g = jnp.dot(h, wg_ref[...], preferred_element_type=F32)
    u = jnp.dot(h, wu_ref[...], preferred_element_type=F32)
    o_ref[...] = (g * _sigmoid(g) * u).astype(o_ref.dtype)


def _ffn_in(h, wg, wu, bm, bn, casts, done=None):
    m, d = h.shape
    fp = wg.shape[1]
    first = 0 if done is None else 1
    if m // bm == first:
        return done, tuple(_run_cast(cast) for cast in casts)
    return _hosted_call(
        _ffn_in_kernel,
        (m // bm - first, fp // bn),
        [
            pl.BlockSpec((bm, d), lambda i, j: (i + first, 0)),
            pl.BlockSpec((d, bn), lambda i, j: (0, j)),
            pl.BlockSpec((d, bn), lambda i, j: (0, j)),
        ],
        pl.BlockSpec((bm, bn), lambda i, j: (i + first, j)),
        jax.ShapeDtypeStruct((m, fp), BF16),
        (h, wg, wu), "ffn_in", casts=casts, in_place=done)


def _ffn_in_first_kernel(h_ref, wg32_ref, wu32_ref, o_ref, wg_ref, wu_ref, *, n_valid):
    keep = pl.program_id(0) < n_valid
    wg = jnp.where(keep, wg32_ref[...], 0.0).astype(wg_ref.dtype)
    wu = jnp.where(keep, wu32_ref[...], 0.0).astype(wu_ref.dtype)
    wg_ref[...] = wg
    wu_ref[...] = wu
    h = h_ref[...]
    g = jnp.dot(h, wg, preferred_element_type=F32)
    u = jnp.dot(h, wu, preferred_element_type=F32)
    o_ref[...] = (g * _sigmoid(g) * u).astype(o_ref.dtype)


def _ffn_in_first(h, w_in, layer, bm):
    m, d = h.shape
    f = w_in.shape[2] // 2
    fp = _ffn_width(f)
    bn = _tile(math.gcd(f, fp), 256)
    nv = f // bn
    return pl.pallas_call(
        functools.partial(_ffn_in_first_kernel, n_valid=nv),
        grid=(fp // bn,),
        in_specs=[
            pl.BlockSpec((bm, d), lambda j: (0, 0)),
            pl.BlockSpec((None, d, bn), lambda j: (layer, 0, jnp.minimum(j, nv - 1))),
            pl.BlockSpec((None, d, bn), lambda j: (layer, 0, nv + jnp.minimum(j, nv - 1))),
        ],
        out_specs=[
            pl.BlockSpec((bm, bn), lambda j: (0, j)),
            pl.BlockSpec((d, bn), lambda j: (0, j)),
            pl.BlockSpec((d, bn), lambda j: (0, j)),
        ],
        out_shape=[jax.ShapeDtypeStruct((m, fp), BF16),
                   jax.ShapeDtypeStruct((d, fp), BF16), jax.ShapeDtypeStruct((d, fp), BF16)],
        compiler_params=_params("arbitrary"),
        name="ffn_in_first",
    )(h, w_in, w_in)


def _res_matmul_kernel(a_ref, w_ref, res_ref, gate_ref, o_ref, acc_ref, *, coef, nk):
    k = pl.program_id(2)

    def partial_product():
        return jnp.dot(a_ref[...], w_ref[...], preferred_element_type=F32)

    def finish(total):
        o_ref[...] = res_ref[...] + (coef * gate_ref[0]) * total

    if nk == 1:
        finish(partial_product())
        return

    @pl.when(k == 0)
    def _():
        acc_ref[...] = partial_product()

    @pl.when((k > 0) & (k < nk - 1))
    def _():
        acc_ref[...] += partial_product()

    @pl.when(k == nk - 1)
    def _():
        finish(acc_ref[...] + partial_product())


def _res_matmul(a, w, res, mod3, gate_j, coef, seq, bm, bn, bk, name, casts=()):
    m, kdim = a.shape
    n = w.shape[1]
    per_b = seq // bm
    return _hosted_call(
        functools.partial(_res_matmul_kernel, coef=coef, nk=kdim // bk),
        (m // bm, n // bn, kdim // bk),
        [
            pl.BlockSpec((bm, bk), lambda i, j, k: (i, k)),
            pl.BlockSpec((bk, bn), lambda i, j, k: (k, j)),
            pl.BlockSpec((bm, bn), lambda i, j, k: (i, j)),
            pl.BlockSpec((1, 1, bn), lambda i, j, k: ((i // per_b) * N_MOD + gate_j, 0, j)),
        ],
        pl.BlockSpec((bm, bn), lambda i, j, k: (i, j)),
        jax.ShapeDtypeStruct((m, n), F32),
        (a, w, res, mod3), name, scratch=[pltpu.VMEM((bm, bn), F32)], casts=casts)


QSCALE = HEAD_DIM ** -0.5 * LOG2E


def _proj_call(body, h, wt, row0, n, bm, bn, extra_in, extra_specs, name, casts):
    m, d = h.shape
    b0 = row0 // bn
    return _hosted_call(
        body,
        (m // bm, n // bn),
        [
            pl.BlockSpec((bm, d), lambda i, j: (i, 0)),
            pl.BlockSpec((bn, d), lambda i, j: (b0 + j, 0)),
        ] + extra_specs,
        pl.BlockSpec((bm, bn), lambda i, j: (i, j)),
        jax.ShapeDtypeStruct((m, n), BF16),
        (h, wt, *extra_in), name, casts=casts)


def _block_scale(n_scaled):
    return jnp.where(pl.program_id(1) < n_scaled, QSCALE, 1.0).astype(F32)


def _proj_plain_kernel(h_ref, wt_ref, o_ref, *, n_scaled):
    acc = _nt_dot(h_ref[...], wt_ref[...])
    o_ref[...] = (acc * _block_scale(n_scaled)).astype(o_ref.dtype)


def _proj_rope_kernel(h_ref, wt_ref, cos_ref, sin_ref, o_ref, *, n_scaled):
    acc = _nt_dot(h_ref[...], wt_ref[...])
    cosf = cos_ref[...] * _block_scale(n_scaled)
    sinf = sin_ref[...] * _block_scale(n_scaled)
    lane = lax.broadcasted_iota(jnp.int32, cosf.shape, 1)
    for c in range(acc.shape[1] // HEAD_DIM):
        cols = slice(c * HEAD_DIM, (c + 1) * HEAD_DIM)
        t = acc[:, cols]
        partner = jnp.where(lane < ROPE_HALF,
                            pltpu.roll(t, HEAD_DIM - ROPE_HALF, 1),
                            pltpu.roll(t, ROPE_HALF, 1))
        o_ref[:, cols] = (t * cosf + partner * sinf).astype(o_ref.dtype)


def _proj_gate_kernel(h_ref, wt_ref, b_ref, o_ref):
    acc = _nt_dot(h_ref[...], wt_ref[...])
    o_ref[...] = _sigmoid(acc + b_ref[...]).astype(o_ref.dtype)


def _proj_plain(h, wt, row0, n, n_scaled, bm, bn, casts):
    return _proj_call(functools.partial(_proj_plain_kernel, n_scaled=n_scaled),
                      h, wt, row0, n, bm, bn, [], [], "proj_plain", casts)


def _proj_rope(h, wt, row0, n, n_scaled, cosf, sinf, bm, bn, casts):
    table = pl.BlockSpec((bm, HEAD_DIM), lambda i, j: (i, 0))
    return _proj_call(functools.partial(_proj_rope_kernel, n_scaled=n_scaled),
                      h, wt, row0, n, bm, bn, [cosf, sinf], [table, table], "proj_rope", casts)


def _proj_gate(h, wt, row0, n, bias, bm, bn, casts):
    return _proj_call(_proj_gate_kernel, h, wt, row0, n, bm, bn, [bias.reshape(1, n)],
                      [pl.BlockSpec((1, bn), lambda i, j: (0, j))], "proj_gate", casts)


def _vt_proj_kernel(wt_ref, h_ref, o_ref, *, tk):
    r = _nt_dot(wt_ref[...], h_ref[...])
    for t in range(o_ref.shape[0]):
        o_ref[t] = r[:, t * tk:(t + 1) * tk].astype(o_ref.dtype)


def _vt_proj(h, wt, row0, n, bm, bn, tk, casts=()):
    m, d = h.shape
    b0 = row0 // bn
    return _hosted_call(
        functools.partial(_vt_proj_kernel, tk=tk),
        (m // bm, n // bn),
        [
            pl.BlockSpec((bn, d), lambda i, j: (b0 + j, 0)),
            pl.BlockSpec((bm, d), lambda i, j: (i, 0)),
        ],
        pl.BlockSpec((bm // tk, bn, tk), lambda i, j: (i, j, 0)),
        jax.ShapeDtypeStruct((m // tk, n, tk), BF16),
        (wt, h), "proj_vt", casts=casts)


def _forget_kernel(h_ref, w_ref, b_ref, e_ref, o_ref, carry_ref):
    @pl.when(pl.program_id(1) == 0)
    def _():
        carry_ref[...] = jnp.zeros_like(carry_ref)

    ts = FORGET_SUBTILE if h_ref.shape[0] % FORGET_SUBTILE == 0 else h_ref.shape[0]
    row = lax.broadcasted_iota(jnp.int32, (ts, ts), 0)
    col = lax.broadcasted_iota(jnp.int32, (ts, ts), 1)
    tri = (row >= col).astype(F32)
    subs = [slice(s * ts, (s + 1) * ts) for s in range(h_ref.shape[0] // ts)]
    ffs = [_nt_dot(h_ref[rows, :], w_ref[...]) + b_ref[...] for rows in subs]
    logfs = [jnp.minimum(ff, 0.0) - jnp.log1p(jnp.exp(-jnp.abs(ff))) for ff in ffs]
    sums = [jnp.dot(tri, logf, preferred_element_type=F32, precision=lax.Precision.HIGHEST)
            for logf in logfs]
    carry = carry_ref[...]
    stacked = []
    for within in sums:
        cum = within + carry
        carry = cum[ts - 1:ts, :]
        rest = cum * (-LOG2E)
        pieces = []
        for _ in range(N_SPLIT):
            p = rest.astype(BF16)
            pieces.append(p)
            rest = rest - p.astype(F32)
        stacked.append(jnp.concatenate(pieces, axis=1))
    carry_ref[...] = carry
    for rows, pieces in zip(subs, stacked):
        spread = jnp.dot(pieces, e_ref[...], preferred_element_type=F32)
        o_ref[rows, :] = spread.astype(o_ref.dtype)


def _forget(h, w_ff_t, b_ff, n_heads, batch, seq):
    m, d = h.shape
    tm = _tile(seq, 2 * FORGET_SUBTILE)
    per_b = seq // tm
    route = np.zeros((N_SPLIT * LANE, n_heads * HEAD_DIM), np.float32)
    for p in range(N_SPLIT):
        for hh in range(n_heads):
            route[p * LANE + hh, hh * HEAD_DIM + p] = 1.0
    return pl.pallas_call(
        _forget_kernel,
        grid=(batch, per_b),
        in_specs=[
            pl.BlockSpec((tm, d), lambda b, i: (b * per_b + i, 0)),
            pl.BlockSpec((LANE, d), lambda b, i: (0, 0)),
            pl.BlockSpec((1, LANE), lambda b, i: (0, 0)),
            pl.BlockSpec(route.shape, lambda b, i: (0, 0)),
        ],
        out_specs=pl.BlockSpec((tm, n_heads * HEAD_DIM), lambda b, i: (b * per_b + i, 0)),
        out_shape=jax.ShapeDtypeStruct((m, n_heads * HEAD_DIM), BF16),
        scratch_shapes=[pltpu.VMEM((1, LANE), F32)],
        compiler_params=_params("arbitrary", "arbitrary"),
        name="forget_cumsum",
    )(h, w_ff_t, b_ff, jnp.asarray(route, BF16))


def _online_softmax_step(st, vt, m, l, acc):
    m_new = jnp.maximum(m, jnp.max(st, axis=0, keepdims=True))
    alpha = jnp.exp2(m - m_new)
    p = jnp.exp2(st - m_new)
    l_new = alpha * l + jnp.sum(p, axis=0, keepdims=True)
    acc_new = alpha * acc + jnp.dot(vt, p.astype(vt.dtype), preferred_element_type=F32)
    return m_new, l_new, acc_new


def _flash_scratch(n_chains, tq, dv):
    return ([pltpu.VMEM((2, tq, tq), F32) for _ in range(n_chains)]
            + [pltpu.VMEM((n_chains, 1, tq), F32), pltpu.VMEM((n_chains, 1, tq), F32),
               pltpu.VMEM((n_chains, dv, tq), F32)])


def _flash_loop(qi, n_chains, qk_fn, vt_fn, visible_fn, st_refs, m_ref, l_ref, acc_ref):
    def produce(kj, slot, diagonal):
        for c in range(n_chains):
            st = qk_fn(c, kj)
            if diagonal:
                st = jnp.where(visible_fn(), st, MASKED)
            st_refs[c][slot] = st

    def consume(c, kj, slot):
        m, l, acc = _online_softmax_step(st_refs[c][slot], vt_fn(c, kj), m_ref[c], l_ref[c], acc_ref[c])
        m_ref[c] = m
        l_ref[c] = l
        acc_ref[c] = acc

    def step(kj, slot, next_is_diagonal):
        for c in range(n_chains):
            st = qk_fn(c, kj + 1)
            if next_is_diagonal:
                st = jnp.where(visible_fn(), st, MASKED)
            st_refs[c][1 - slot] = st
            consume(c, kj, slot)

    def run_if(cond, fn):
        def body(_, carry):
            fn()
            return carry
        lax.fori_loop(0, cond.astype(jnp.int32), body, 0)

    tq = m_ref.shape[-1]
    m_ref[...] = jnp.full(m_ref.shape, MASKED, F32)
    l_ref[...] = jnp.zeros(l_ref.shape, F32)
    acc_ref[...] = jnp.zeros(acc_ref.shape, F32)

    run_if(qi == 0, lambda: produce(0, 0, True))
    run_if(qi > 0, lambda: produce(0, 0, False))

    n_pairs = jnp.maximum(qi - 1, 0) // 2
    kj0 = 2 * n_pairs

    def pair(i, carry):
        step(2 * i, 0, False)
        step(2 * i + 1, 1, False)
        return carry

    lax.fori_loop(0, n_pairs, pair, 0)
    two_left = (qi > 0) & (qi - kj0 == 2)
    one_left = (qi > 0) & (qi - kj0 == 1)
    run_if(two_left, lambda: step(kj0, 0, False))
    run_if(two_left, lambda: step(kj0 + 1, 1, True))
    run_if(one_left, lambda: step(kj0, 0, True))
    run_if(one_left, lambda: [consume(c, qi, 1) for c in range(n_chains)])
    run_if(jnp.logical_not(one_left), lambda: [consume(c, qi, 0) for c in range(n_chains)])


def _fox_kernel(q_ref, k_ref, kx_ref, vt_ref, o_ref, *scratch, tq, nh):
    qi = pl.program_id(2)
    st_refs, (m_ref, l_ref, acc_ref) = scratch[:nh], scratch[nh:]
    lane = lax.broadcasted_iota(jnp.int32, (tq, HEAD_DIM), 1)
    ones = (lane < N_SPLIT).astype(q_ref.dtype)

    def head(a):
        return slice(a * HEAD_DIM, (a + 1) * HEAD_DIM)

    def qk(a, kj):
        ks = pl.multiple_of(kj * tq, tq)
        q_aug = jnp.concatenate([q_ref[:, head(a)], ones], axis=1)
        k_aug = jnp.concatenate([k_ref[pl.ds(ks, tq), head(a)], kx_ref[pl.ds(ks, tq), head(a)]], axis=1)
        return _nt_dot(k_aug, q_aug)

    def visible():
        key = lax.broadcasted_iota(jnp.int32, (tq, tq), 0)
        qry = lax.broadcasted_iota(jnp.int32, (tq, tq), 1)
        return key <= qry

    _flash_loop(qi, nh, qk, lambda a, kj: vt_ref[kj, head(a), :], visible,
                st_refs, m_ref, l_ref, acc_ref)
    for a in range(nh):
        o_ref[:, head(a)] = (acc_ref[a] / l_ref[a]).T.astype(o_ref.dtype)


def _fox(qk, kx, vt, batch, seq, n_heads, tq, nh):
    m = qk.shape[0]
    nq = seq // tq
    gw = nh * HEAD_DIM
    ng = n_heads // nh
    return pl.pallas_call(
        functools.partial(_fox_kernel, tq=tq, nh=nh),
        grid=(batch, ng, nq),
        in_specs=[
            pl.BlockSpec((tq, gw), lambda b, g, i: (b * nq + i, g)),
            pl.BlockSpec((seq, gw), lambda b, g, i: (b, ng + g)),
            pl.BlockSpec((seq, gw), lambda b, g, i: (b, g)),
            pl.BlockSpec((nq, gw, tq), lambda b, g, i: (b, g, 0)),
        ],
        out_specs=pl.BlockSpec((tq, gw), lambda b, g, i: (b * nq + i, g)),
        out_shape=jax.ShapeDtypeStruct((m, n_heads * HEAD_DIM), BF16),
        scratch_shapes=_flash_scratch(nh, tq, HEAD_DIM),
        compiler_params=_params("arbitrary", "arbitrary", "arbitrary"),
        name="fox_attention",
    )(qk, qk, kx, vt)


def _diff_kernel(q_ref, k_ref, vt_ref, lam_ref, g_ref, o_ref, *scratch, tq, nh, lambda_init):
    qi = pl.program_id(2)
    n_maps = 2 * nh
    st_refs, (m_ref, l_ref, acc_ref) = scratch[:n_maps], scratch[n_maps:]
    hw = 2 * HEAD_DIM

    def cols(c):
        return slice(c * HEAD_DIM, (c + 1) * HEAD_DIM)

    def qk(c, kj):
        ks = pl.multiple_of(kj * tq, tq)
        return _nt_dot(k_ref[pl.ds(ks, tq), cols(c)], q_ref[:, cols(c)])

    def visible():
        key = lax.broadcasted_iota(jnp.int32, (tq, tq), 0) // CHUNK
        qry = lax.broadcasted_iota(jnp.int32, (tq, tq), 1) // CHUNK
        return key <= qry

    _flash_loop(qi, n_maps, qk, lambda c, kj: vt_ref[kj, (c // 2) * hw:(c // 2 + 1) * hw, :],
                visible, st_refs, m_ref, l_ref, acc_ref)

    lp = lam_ref[...]
    lam = (jnp.exp(jnp.sum(lp[0:1] * lp[1:2], axis=-1, keepdims=True))
           - jnp.exp(jnp.sum(lp[2:3] * lp[3:4], axis=-1, keepdims=True)) + lambda_init)
    for a in range(nh):
        y = acc_ref[2 * a] / l_ref[2 * a] - lam * (acc_ref[2 * a + 1] / l_ref[2 * a + 1])
        ms = jnp.mean(y * y, axis=0, keepdims=True)
        y = (y * lax.rsqrt(ms + EPS)).T * g_ref[...]
        o_ref[:, a * hw:(a + 1) * hw] = (y * (1.0 - lambda_init)).astype(o_ref.dtype)


def _diff(qk, vt, lam_p, subln, batch, seq, n_heads, tq, nh, lambda_init):
    m = qk.shape[0]
    nq = seq // tq
    hw = 2 * HEAD_DIM
    gw = nh * hw
    ng = n_heads // nh
    return pl.pallas_call(
        functools.partial(_diff_kernel, tq=tq, nh=nh, lambda_init=lambda_init),
        grid=(batch, ng, nq),
        in_specs=[
            pl.BlockSpec((tq, gw), lambda b, g, i: (b * nq + i, g)),
            pl.BlockSpec((seq, gw), lambda b, g, i: (b, ng + g)),
            pl.BlockSpec((nq, gw, tq), lambda b, g, i: (b, g, 0)),
            pl.BlockSpec((4, HEAD_DIM), lambda b, g, i: (0, 0)),
            pl.BlockSpec((1, hw), lambda b, g, i: (0, 0)),
        ],
        out_specs=pl.BlockSpec((tq, gw), lambda b, g, i: (b * nq + i, g)),
        out_shape=jax.ShapeDtypeStruct((m, n_heads * hw), BF16),
        scratch_shapes=_flash_scratch(2 * nh, tq, hw),
        compiler_params=_params("arbitrary", "arbitrary", "arbitrary"),
        name="diff_attention",
    )(qk, qk, vt, lam_p, subln.reshape(1, hw))


def _merge_kernel(ya_ref, yb_ref, wa_ref, wb_ref, sa_ref, sb_ref, o_ref):
    pa = jnp.dot(ya_ref[...], wa_ref[...], preferred_element_type=F32)
    pb = jnp.dot(yb_ref[...], wb_ref[...], preferred_element_type=F32)
    o_ref[...] = (sa_ref[...].astype(F32) * pa + sb_ref[...].astype(F32) * pb).astype(o_ref.dtype)


def _merge(ya, yb, wa, wb, gates, bm, bn):
    m = ya.shape[0]
    d = wa.shape[1]
    nb = d // bn
    return pl.pallas_call(
        _merge_kernel,
        grid=(m // bm, nb),
        in_specs=[
            pl.BlockSpec((bm, ya.shape[1]), lambda i, j: (i, 0)),
            pl.BlockSpec((bm, yb.shape[1]), lambda i, j: (i, 0)),
            pl.BlockSpec((wa.shape[0], bn), lambda i, j: (0, j)),
            pl.BlockSpec((wb.shape[0], bn), lambda i, j: (0, j)),
            pl.BlockSpec((bm, bn), lambda i, j: (i, j)),
            pl.BlockSpec((bm, bn), lambda i, j: (i, nb + j)),
        ],
        out_specs=pl.BlockSpec((bm, bn), lambda i, j: (i, j)),
        out_shape=jax.ShapeDtypeStruct((m, d), BF16),
        compiler_params=_params("arbitrary", "arbitrary"),
        name="gated_merge",
    )(ya, yb, wa, wb, gates, gates)


def _ffn_width(f):
    return _round_up(f, 1024) if f >= 1024 else f


def _ffn_in_cast(w_in, layer):
    _, d, f2 = w_in.shape
    f = f2 // 2
    return _StripCast(w_in, layer, 0, d, d, ((0, f, _ffn_width(f)), (f, f, _ffn_width(f))))


def _ffn(x2, gain, w_in, pre_cast, w_out, layer, mod3, j0, seq, next_casts):
    m, d = x2.shape
    f = w_out.shape[1]
    fp = _ffn_width(f)
    bm = _tile(seq, 1024)
    h = _norm_mod(x2, gain, mod3, N_MOD, j0, j0 + 1, seq, BF16)
    wo_cast = _StripCast(w_out, layer, 0, f, fp, ((0, d, d),))
    if pre_cast is None:
        done, wg, wu = _ffn_in_first(h, w_in, layer, bm)
    else:
        done, (wg, wu) = None, pre_cast
    a, ((wo,),) = _ffn_in(h, wg, wu, bm, _tile(fp, 512), (wo_cast,), done)
    bk = fp // 4 if (fp // 4) % LANE == 0 else fp
    return _res_matmul(a, wo, x2, mod3, j0 + 2, 0.5, seq, bm, _tile(d, 1024), bk, "ffn_out",
                       casts=next_casts)


def kernel(x, c, positions, ada_w, ada_b, norm_ffn1, ffn1_w_in, ffn1_w_out, norm_mix, w_in, b_forget, b_gate, diff_lambda, diff_subln, w_o_fox, w_o_diff, w_out, norm_ffn2, ffn2_w_in, ffn2_w_out, final_ada_w, final_ada_b, norm_final):
    batch, seq, d = x.shape
    depth = ada_w.shape[0]
    m = batch * seq
    n_fox = b_forget.shape[1]
    fox_w = n_fox * HEAD_DIM
    diff_w = w_o_diff.shape[1]
    n_diff = diff_w // (2 * HEAD_DIM)
    o_ff = 3 * fox_w
    o_dq = o_ff + n_fox
    n_rest = 3 * diff_w + 2 * d
    assert w_in.shape[2] == o_dq + n_rest
    assert n_fox <= LANE and seq % CHUNK == 0

    c_pad = jnp.pad(c, ((0, -batch % 8), (0, 0)))
    x2 = x.reshape(m, d)
    w_in_t = jnp.swapaxes(w_in, 1, 2)

    inv_freq = ROPE_THETA ** (-jnp.arange(0, ROPE_DIM, 2, dtype=F32) / ROPE_DIM)
    ang = positions.astype(F32).reshape(m, 1) * inv_freq
    cos, sin = jnp.cos(ang), jnp.sin(ang)
    cosf = jnp.concatenate([cos, cos, jnp.ones((m, HEAD_DIM - ROPE_DIM), F32)], axis=-1)
    sinf = jnp.concatenate([-sin, sin, jnp.zeros((m, HEAD_DIM - ROPE_DIM), F32)], axis=-1)

    bm = _tile(seq, 1024)
    tq = _tile(seq, 512)
    bn_p = _tile(math.gcd(fox_w, diff_w), 1024)
    for l in range(depth):
        lambda_init = 0.8 - 0.6 * math.exp(-0.3 * l)
        mod = _ada(c_pad, ada_w[l], ada_b[l])[:batch]
        mod3 = mod.reshape(batch * N_MOD, 1, d)

        mixer_casts = (_StripCast(w_in_t, l, 0, o_ff, o_ff, ((0, d, d),)),
                       _StripCast(w_in_t, l, o_dq, n_rest, n_rest, ((0, d, d),)))
        x2, ((w_front,), (w_rest,)) = _ffn(x2, norm_ffn1[l], ffn1_w_in, None, ffn1_w_out, l, mod3,
                                           0, seq, mixer_casts)

        h = _norm_mod(x2, norm_mix[l], mod3, N_MOD, 3, 4, seq, BF16)
        w_ff_t = jnp.pad(w_in_t[l, o_ff:o_dq, :].astype(BF16), ((0, LANE - n_fox), (0, 0)))
        b_ff = jnp.pad(b_forget[l].astype(F32), (0, LANE - n_fox)).reshape(1, LANE)

        qk_f, ((wa,),) = _proj_plain(h, w_front, 0, 2 * fox_w, fox_w // bn_p, bm, bn_p,
                                     (_whole(w_o_fox, l),))
        qk_d, ((wb,),) = _proj_rope(h, w_rest, 0, 2 * diff_w, diff_w // bn_p, cosf, sinf, bm, bn_p,
                                    (_whole(w_o_diff, l),))
        gates, ((wg, wu),) = _proj_gate(h, w_rest, 3 * diff_w, 2 * d, b_gate[l], bm, bn_p,
                                        (_ffn_in_cast(ffn2_w_in, l),))
        vt_f, ((wo,),) = _vt_proj(h, w_front, 2 * fox_w, fox_w, bm, bn_p, tq, (_whole(w_out, l),))
        vt_d, _ = _vt_proj(h, w_rest, 2 * diff_w, diff_w, bm, bn_p, tq)
        kx = _forget(h, w_ff_t, b_ff, n_fox, batch, seq)

        ya = _fox(qk_f, kx, vt_f, batch, seq, n_fox, tq, math.gcd(n_fox, FOX_HEADS_PER_STEP))
        yb = _diff(qk_d, vt_d, diff_lambda[l].astype(F32), diff_subln[l], batch, seq, n_diff,
                   tq, math.gcd(n_diff, DIFF_HEADS_PER_STEP), lambda_init)
        merged = _merge(ya, yb, wa, wb, gates, bm, _tile(d, 1024))
        x2, _ = _res_matmul(merged, wo, x2, mod3, 5, 1.0, seq, bm, _tile(d, 1024), d, "mixer_out")

        x2, _ = _ffn(x2, norm_ffn2[l], ffn2_w_in, (wg, wu), ffn2_w_out, l, mod3, 6, seq, ())

    fmod = _ada(c_pad, final_ada_w, final_ada_b)[:batch]
    out = _norm_mod(x2, norm_final, fmod.reshape(batch * 2, 1, d), 2, 0, 1, seq, F32)
    return out.reshape(batch, seq, d)
```

```python
import functools
import math
from typing import NamedTuple

import numpy as np
import jax
import jax.numpy as jnp
from jax import lax
from jax.experimental import pallas as pl
from jax.experimental.pallas import tpu as pltpu

HEAD_DIM = 128
CHUNK = 64
ROPE_THETA = 500000.0
ROPE_DIM = HEAD_DIM // 4
ROPE_HALF = ROPE_DIM // 2
EPS = 1e-6
N_MOD = 9
LOG2E = 1.4426950408889634
MASKED = -1e30
LANE = 128
VMEM_LIMIT = 56 * 1024 * 1024
BF16_ROWS = 16
CAST_BLOCK_ELEMS = 2 * 1024 * 1024
HOSTED_STRIP_ELEMS = 1024 * 1024
N_SPLIT = 3
FORGET_SUBTILE = 512
FOX_HEADS_PER_STEP = 4
DIFF_HEADS_PER_STEP = 2

F32 = jnp.float32
BF16 = jnp.bfloat16


def _params(*sem):
    return pltpu.CompilerParams(dimension_semantics=sem, vmem_limit_bytes=VMEM_LIMIT)


def _tile(n, pref):
    t = min(n, pref)
    while n % t:
        t //= 2
    return t


def _round_up(n, m):
    return (n + m - 1) // m * m


def _sigmoid(x):
    return 0.5 * jnp.tanh(0.5 * x) + 0.5


def _nt_dot(a, b):
    return lax.dot_general(a, b, (((1,), (1,)), ((), ())), preferred_element_type=F32)


class _StripCast(NamedTuple):
    src: jax.Array
    layer: int
    row0: int
    rows: int
    rows_out: int
    pieces: tuple


def _whole(src, layer):
    _, r, c = src.shape
    return _StripCast(src, layer, 0, r, r, ((0, c, c),))


def _strip_rows(cast, lo, hi):
    g = math.gcd(math.gcd(cast.row0, cast.rows), cast.rows_out)
    for rs in range(_round_up(max(lo, 1), BF16_ROWS), min(hi, g) + 1, BF16_ROWS):
        if g % rs == 0:
            return rs
    return None


def _linear_step(grid, ids):
    t = ids[0]
    for g, i in zip(grid[1:], ids[1:]):
        t = t * g + i
    return t


def _strip_specs(cast, rs, grid):
    first, n_valid, n_strips = cast.row0 // rs, cast.rows // rs, cast.rows_out // rs
    cols = cast.src.shape[2]
    src_spec = pl.BlockSpec(
        (None, rs, cols),
        lambda *ids: (cast.layer, first + jnp.minimum(_linear_step(grid, ids), n_valid - 1), 0))
    out_specs = [pl.BlockSpec((rs, nco), lambda *ids: (jnp.minimum(_linear_step(grid, ids), n_strips - 1), 0))
                 for _, _, nco in cast.pieces]
    out_shapes = [jax.ShapeDtypeStruct((cast.rows_out, nco), BF16) for _, _, nco in cast.pieces]
    return src_spec, out_specs, out_shapes


def _strip_cast_body(cast, rs, grid, src_ref, dst_refs):
    t = _linear_step(grid, [pl.program_id(a) for a in range(len(grid))])
    strip = src_ref[...]
    for (c0, n, nco), dst in zip(cast.pieces, dst_refs):
        piece = strip[:, c0:c0 + n].astype(dst.dtype)
        if cast.rows < cast.rows_out:
            piece = jnp.where(t < cast.rows // rs, piece, jnp.zeros_like(piece))
        if nco > n:
            dst[:, :n] = piece
            dst[:, n:] = jnp.zeros((rs, nco - n), dst.dtype)
        else:
            dst[...] = piece


def _run_cast(cast):
    rs = _strip_rows(cast, 1, max(CAST_BLOCK_ELEMS // cast.src.shape[2], BF16_ROWS))
    rs = rs or _strip_rows(cast, 1, cast.rows_out)
    if rs is None:
        w = cast.src[cast.layer, cast.row0:cast.row0 + cast.rows]
        return tuple(jnp.pad(w[:, c0:c0 + n].astype(BF16), ((0, cast.rows_out - cast.rows), (0, nco - n)))
                     for c0, n, nco in cast.pieces)
    grid = (cast.rows_out // rs,)
    src_spec, out_specs, out_shapes = _strip_specs(cast, rs, grid)

    def body(src_ref, *dst_refs):
        _strip_cast_body(cast, rs, grid, src_ref, dst_refs)

    return tuple(pl.pallas_call(
        body, grid=grid, in_specs=[src_spec], out_specs=out_specs, out_shape=out_shapes,
        compiler_params=_params("arbitrary"), name="weight_cast",
    )(cast.src))


def _hosted_call(body, grid, in_specs, out_spec, out_shape, inputs, name, scratch=(), casts=()):
    steps = math.prod(grid)
    hosted, strip_rows = [], []
    for cast in casts:
        rs = _strip_rows(cast, pl.cdiv(cast.rows_out, steps), cast.rows_out)
        if rs is not None and rs * cast.src.shape[2] <= HOSTED_STRIP_ELEMS:
            hosted.append(cast)
            strip_rows.append(rs)
    specs = [_strip_specs(cast, rs, grid) for cast, rs in zip(hosted, strip_rows)]
    n_in, n_src = len(in_specs), len(hosted)

    def kernel_body(*refs):
        ins, src_refs, o_ref = refs[:n_in], refs[n_in:n_in + n_src], refs[n_in + n_src]
        pos = n_in + n_src + 1
        dst_groups = []
        for cast in hosted:
            dst_groups.append(refs[pos:pos + len(cast.pieces)])
            pos += len(cast.pieces)
        body(*ins, o_ref, *refs[pos:])
        for cast, rs, src_ref, dst_refs in zip(hosted, strip_rows, src_refs, dst_groups):
            _strip_cast_body(cast, rs, grid, src_ref, dst_refs)

    outs = pl.pallas_call(
        kernel_body, grid=grid,
        in_specs=list(in_specs) + [s[0] for s in specs],
        out_specs=[out_spec] + [o for s in specs for o in s[1]],
        out_shape=[out_shape] + [o for s in specs for o in s[2]],
        scratch_shapes=list(scratch),
        compiler_params=_params(*(("arbitrary",) * len(grid))), name=name,
    )(*inputs, *[cast.src for cast in hosted])

    results, pos = [], 1
    for cast in casts:
        if any(cast is hc for hc in hosted):
            results.append(tuple(outs[pos:pos + len(cast.pieces)]))
            pos += len(cast.pieces)
        else:
            results.append(_run_cast(cast))
    return outs[0], tuple(results)


def _ada_kernel(c_ref, w_ref, b_ref, o_ref):
    c = c_ref[...]
    sc = (c * jax.nn.sigmoid(c)).astype(BF16)
    o_ref[...] = jnp.dot(sc, w_ref[...].astype(BF16), preferred_element_type=F32) + b_ref[...]


def _ada(c_pad, w, b):
    d, n = w.shape
    tn = _tile(n, 512)
    return pl.pallas_call(
        _ada_kernel,
        grid=(n // tn,),
        in_specs=[
            pl.BlockSpec((c_pad.shape[0], d), lambda j: (0, 0)),
            pl.BlockSpec((d, tn), lambda j: (0, j)),
            pl.BlockSpec((1, tn), lambda j: (0, j)),
        ],
        out_specs=pl.BlockSpec((c_pad.shape[0], tn), lambda j: (0, j)),
        out_shape=jax.ShapeDtypeStruct((c_pad.shape[0], n), F32),
        compiler_params=_params("arbitrary"),
        name="ada_matvec",
    )(c_pad, w, b.reshape(1, n))


def _norm_mod_kernel(x_ref, g_ref, sh_ref, sc_ref, o_ref):
    x = x_ref[...]
    ms = jnp.mean(x * x, axis=-1, keepdims=True)
    y = x * lax.rsqrt(ms + EPS) * g_ref[...]
    o_ref[...] = (y * (1.0 + sc_ref[0]) + sh_ref[0]).astype(o_ref.dtype)


def _norm_mod(x2, gain, mod3, n_mod, shift_j, scale_j, seq, out_dtype):
    m, d = x2.shape
    tm = _tile(seq, 512)
    per_b = seq // tm
    return pl.pallas_call(
        _norm_mod_kernel,
        grid=(m // tm,),
        in_specs=[
            pl.BlockSpec((tm, d), lambda i: (i, 0)),
            pl.BlockSpec((1, d), lambda i: (0, 0)),
            pl.BlockSpec((1, 1, d), lambda i: ((i // per_b) * n_mod + shift_j, 0, 0)),
            pl.BlockSpec((1, 1, d), lambda i: ((i // per_b) * n_mod + scale_j, 0, 0)),
        ],
        out_specs=pl.BlockSpec((tm, d), lambda i: (i, 0)),
        out_shape=jax.ShapeDtypeStruct((m, d), out_dtype),
        compiler_params=_params("arbitrary"),
        name="norm_mod",
    )(x2, gain.reshape(1, d), mod3, mod3)


def _ffn_in_kernel(h_ref, wg_ref, wu_ref, o_ref):
    h = h_ref[...]
    g = jnp.dot(h, wg_ref[...], preferred_element_type=F32)
    u = jnp.dot(h, wu_ref[...], preferred_element_type=F32)
    o_ref[...] = (g * _sigmoid(g) * u).astype(o_ref.dtype)


def _ffn_in_resume_kernel(h_ref, wg_ref, wu_ref, a0_ref, o_ref):
    @pl.when(pl.program_id(0) == 0)
    def _():
        o_ref[...] = a0_ref[...]

    @pl.when(pl.program_id(0) > 0)
    def _():
        _ffn_in_kernel(h_ref, wg_ref, wu_ref, o_ref)


def _ffn_in(h, wg, wu, bm, bn, casts, a0=None):
    m, d = h.shape
    fp = wg.shape[1]
    nj = fp // bn
    resume = a0 is not None
    first = 1 if resume and m > bm else 0
    in_specs = [
        pl.BlockSpec((bm, d), lambda i, j: (jnp.maximum(i, first), 0)),
        pl.BlockSpec((d, bn), lambda i, j: (0, j)),
        pl.BlockSpec((d, bn), lambda i, j: (0, j)),
    ]
    if resume:
        in_specs.append(pl.BlockSpec((bm, bn), lambda i, j: (0, jnp.where(i == 0, j, nj - 1))))
    return _hosted_call(
        _ffn_in_resume_kernel if resume else _ffn_in_kernel,
        (m // bm, nj), in_specs,
        pl.BlockSpec((bm, bn), lambda i, j: (i, j)),
        jax.ShapeDtypeStruct((m, fp), BF16),
        (h, wg, wu) + ((a0,) if resume else ()), "ffn_in", casts=casts)


def _ffn_in_first_kernel(h_ref, wg32_ref, wu32_ref, o_ref, wg_ref, wu_ref, *, n_valid):
    keep = pl.program_id(0) < n_valid
    wg = jnp.where(keep, wg32_ref[...], 0.0).astype(wg_ref.dtype)
    wu = jnp.where(keep, wu32_ref[...], 0.0).astype(wu_ref.dtype)
    wg_ref[...] = wg
    wu_ref[...] = wu
    h = h_ref[...]
    g = jnp.dot(h, wg, preferred_element_type=F32)
    u = jnp.dot(h, wu, preferred_element_type=F32)
    o_ref[...] = (g * _sigmoid(g) * u).astype(o_ref.dtype)


def _ffn_in_first(h, w_in, layer, bm):
    m, d = h.shape
    f = w_in.shape[2] // 2
    fp = _ffn_width(f)
    bn = _tile(math.gcd(f, fp), 256)
    nv = f // bn
    return pl.pallas_call(
        functools.partial(_ffn_in_first_kernel, n_valid=nv),
        grid=(fp // bn,),
        in_specs=[
            pl.BlockSpec((bm, d), lambda j: (0, 0)),
            pl.BlockSpec((None, d, bn), lambda j: (layer, 0, jnp.minimum(j, nv - 1))),
            pl.BlockSpec((None, d, bn), lambda j: (layer, 0, nv + jnp.minimum(j, nv - 1))),
        ],
        out_specs=[
            pl.BlockSpec((bm, bn), lambda j: (0, j)),
            pl.BlockSpec((d, bn), lambda j: (0, j)),
            pl.BlockSpec((d, bn), lambda j: (0, j)),
        ],
        out_shape=[jax.ShapeDtypeStruct((bm, fp), BF16),
                   jax.ShapeDtypeStruct((d, fp), BF16), jax.ShapeDtypeStruct((d, fp), BF16)],
        compiler_params=_params("arbitrary"),
        name="ffn_in_first",
    )(h, w_in, w_in)


def _res_matmul_kernel(a_ref, w_ref, res_ref, gate_ref, o_ref, acc_ref, *, coef, nk):
    k = pl.program_id(2)

    def partial_product():
        return jnp.dot(a_ref[...], w_ref[...], preferred_element_type=F32)

    def finish(total):
        o_ref[...] = res_ref[...] + (coef * gate_ref[0]) * total

    if nk == 1:
        finish(partial_product())
        return

    @pl.when(k == 0)
    def _():
        acc_ref[...] = partial_product()

    @pl.when((k > 0) & (k < nk - 1))
    def _():
        acc_ref[...] += partial_product()

    @pl.when(k == nk - 1)
    def _():
        finish(acc_ref[...] + partial_product())


def _res_matmul(a, w, res, mod3, gate_j, coef, seq, bm, bn, bk, name, casts=()):
    m, kdim = a.shape
    n = w.shape[1]
    per_b = seq // bm
    return _hosted_call(
        functools.partial(_res_matmul_kernel, coef=coef, nk=kdim // bk),
        (m // bm, n // bn, kdim // bk),
        [
            pl.BlockSpec((bm, bk), lambda i, j, k: (i, k)),
            pl.BlockSpec((bk, bn), lambda i, j, k: (k, j)),
            pl.BlockSpec((bm, bn), lambda i, j, k: (i, j)),
            pl.BlockSpec((1, 1, bn), lambda i, j, k: ((i // per_b) * N_MOD + gate_j, 0, j)),
        ],
        pl.BlockSpec((bm, bn), lambda i, j, k: (i, j)),
        jax.ShapeDtypeStruct((m, n), F32),
        (a, w, res, mod3), name, scratch=[pltpu.VMEM((bm, bn), F32)], casts=casts)


QSCALE = HEAD_DIM ** -0.5 * LOG2E


def _proj_call(body, h, wt, row0, n, bm, bn, extra_in, extra_specs, name, casts):
    m, d = h.shape
    b0 = row0 // bn
    return _hosted_call(
        body,
        (m // bm, n // bn),
        [
            pl.BlockSpec((bm, d), lambda i, j: (i, 0)),
            pl.BlockSpec((bn, d), lambda i, j: (b0 + j, 0)),
        ] + extra_specs,
        pl.BlockSpec((bm, bn), lambda i, j: (i, j)),
        jax.ShapeDtypeStruct((m, n), BF16),
        (h, wt, *extra_in), name, casts=casts)


def _block_scale(n_scaled):
    return jnp.where(pl.program_id(1) < n_scaled, QSCALE, 1.0).astype(F32)


def _proj_plain_kernel(h_ref, wt_ref, o_ref, *, n_scaled):
    acc = _nt_dot(h_ref[...], wt_ref[...])
    o_ref[...] = (acc * _block_scale(n_scaled)).astype(o_ref.dtype)


def _proj_rope_kernel(h_ref, wt_ref, cos_ref, sin_ref, o_ref, *, n_scaled):
    acc = _nt_dot(h_ref[...], wt_ref[...])
    cosf = cos_ref[...] * _block_scale(n_scaled)
    sinf = sin_ref[...] * _block_scale(n_scaled)
    lane = lax.broadcasted_iota(jnp.int32, cosf.shape, 1)
    for c in range(acc.shape[1] // HEAD_DIM):
        cols = slice(c * HEAD_DIM, (c + 1) * HEAD_DIM)
        t = acc[:, cols]
        partner = jnp.where(lane < ROPE_HALF,
                            pltpu.roll(t, HEAD_DIM - ROPE_HALF, 1),
                            pltpu.roll(t, ROPE_HALF, 1))
        o_ref[:, cols] = (t * cosf + partner * sinf).astype(o_ref.dtype)


def _proj_gate_kernel(h_ref, wt_ref, b_ref, o_ref):
    acc = _nt_dot(h_ref[...], wt_ref[...])
    o_ref[...] = _sigmoid(acc + b_ref[...]).astype(o_ref.dtype)


def _proj_plain(h, wt, row0, n, n_scaled, bm, bn, casts):
    return _proj_call(functools.partial(_proj_plain_kernel, n_scaled=n_scaled),
                      h, wt, row0, n, bm, bn, [], [], "proj_plain", casts)


def _proj_rope(h, wt, row0, n, n_scaled, cosf, sinf, bm, bn, casts):
    table = pl.BlockSpec((bm, HEAD_DIM), lambda i, j: (i, 0))
    return _proj_call(functools.partial(_proj_rope_kernel, n_scaled=n_scaled),
                      h, wt, row0, n, bm, bn, [cosf, sinf], [table, table], "proj_rope", casts)


def _proj_gate(h, wt, row0, n, bias, bm, bn, casts):
    return _proj_call(_proj_gate_kernel, h, wt, row0, n, bm, bn, [bias.reshape(1, n)],
                      [pl.BlockSpec((1, bn), lambda i, j: (0, j))], "proj_gate", casts)


def _vt_proj_kernel(wt_ref, h_ref, o_ref, *, tk):
    r = _nt_dot(wt_ref[...], h_ref[...])
    for t in range(o_ref.shape[0]):
        o_ref[t] = r[:, t * tk:(t + 1) * tk].astype(o_ref.dtype)


def _vt_proj(h, wt, row0, n, bm, bn, tk, casts=()):
    m, d = h.shape
    b0 = row0 // bn
    return _hosted_call(
        functools.partial(_vt_proj_kernel, tk=tk),
        (m // bm, n // bn),
        [
            pl.BlockSpec((bn, d), lambda i, j: (b0 + j, 0)),
            pl.BlockSpec((bm, d), lambda i, j: (i, 0)),
        ],
        pl.BlockSpec((bm // tk, bn, tk), lambda i, j: (i, j, 0)),
        jax.ShapeDtypeStruct((m // tk, n, tk), BF16),
        (wt, h), "proj_vt", casts=casts)


def _forget_kernel(h_ref, w_ref, b_ref, e_ref, o_ref, carry_ref):
    @pl.when(pl.program_id(1) == 0)
    def _():
        carry_ref[...] = jnp.zeros_like(carry_ref)

    ts = FORGET_SUBTILE if h_ref.shape[0] % FORGET_SUBTILE == 0 else h_ref.shape[0]
    row = lax.broadcasted_iota(jnp.int32, (ts, ts), 0)
    col = lax.broadcasted_iota(jnp.int32, (ts, ts), 1)
    tri = (row >= col).astype(F32)
    subs = [slice(s * ts, (s + 1) * ts) for s in range(h_ref.shape[0] // ts)]
    ffs = [_nt_dot(h_ref[rows, :], w_ref[...]) + b_ref[...] for rows in subs]
    logfs = [jnp.minimum(ff, 0.0) - jnp.log1p(jnp.exp(-jnp.abs(ff))) for ff in ffs]
    sums = [jnp.dot(tri, logf, preferred_element_type=F32, precision=lax.Precision.HIGHEST)
            for logf in logfs]
    carry = carry_ref[...]
    stacked = []
    for within in sums:
        cum = within + carry
        carry = cum[ts - 1:ts, :]
        rest = cum * (-LOG2E)
        pieces = []
        for _ in range(N_SPLIT):
            p = rest.astype(BF16)
            pieces.append(p)
            rest = rest - p.astype(F32)
        stacked.append(jnp.concatenate(pieces, axis=1))
    carry_ref[...] = carry
    for rows, pieces in zip(subs, stacked):
        spread = jnp.dot(pieces, e_ref[...], preferred_element_type=F32)
        o_ref[rows, :] = spread.astype(o_ref.dtype)


def _forget(h, w_ff_t, b_ff, n_heads, batch, seq):
    m, d = h.shape
    tm = _tile(seq, 2 * FORGET_SUBTILE)
    per_b = seq // tm
    route = np.zeros((N_SPLIT * LANE, n_heads * HEAD_DIM), np.float32)
    for p in range(N_SPLIT):
        for hh in range(n_heads):
            route[p * LANE + hh, hh * HEAD_DIM + p] = 1.0
    return pl.pallas_call(
        _forget_kernel,
        grid=(batch, per_b),
        in_specs=[
            pl.BlockSpec((tm, d), lambda b, i: (b * per_b + i, 0)),
            pl.BlockSpec((LANE, d), lambda b, i: (0, 0)),
            pl.BlockSpec((1, LANE), lambda b, i: (0, 0)),
            pl.BlockSpec(route.shape, lambda b, i: (0, 0)),
        ],
        out_specs=pl.BlockSpec((tm, n_heads * HEAD_DIM), lambda b, i: (b * per_b + i, 0)),
        out_shape=jax.ShapeDtypeStruct((m, n_heads * HEAD_DIM), BF16),
        scratch_shapes=[pltpu.VMEM((1, LANE), F32)],
        compiler_params=_params("arbitrary", "arbitrary"),
        name="forget_cumsum",
    )(h, w_ff_t, b_ff, jnp.asarray(route, BF16))


def _online_softmax_step(st, vt, m, l, acc):
    m_new = jnp.maximum(m, jnp.max(st, axis=0, keepdims=True))
    alpha = jnp.exp2(m - m_new)
    p = jnp.exp2(st - m_new)
    l_new = alpha * l + jnp.sum(p, axis=0, keepdims=True)
    acc_new = alpha * acc + jnp.dot(vt, p.astype(vt.dtype), preferred_element_type=F32)
    return m_new, l_new, acc_new


def _flash_scratch(n_chains, tq, dv):
    return ([pltpu.VMEM((2, tq, tq), F32) for _ in range(n_chains)]
            + [pltpu.VMEM((n_chains, 1, tq), F32), pltpu.VMEM((n_chains, 1, tq), F32),
               pltpu.VMEM((n_chains, dv, tq), F32)])


def _flash_loop(qi, n_chains, qk_fn, vt_fn, visible_fn, st_refs, m_ref, l_ref, acc_ref):
    def produce(kj, slot, diagonal):
        for c in range(n_chains):
            st = qk_fn(c, kj)
            if diagonal:
                st = jnp.where(visible_fn(), st, MASKED)
            st_refs[c][slot] = st

    def consume(c, kj, slot):
        m, l, acc = _online_softmax_step(st_refs[c][slot], vt_fn(c, kj), m_ref[c], l_ref[c], acc_ref[c])
        m_ref[c] = m
        l_ref[c] = l
        acc_ref[c] = acc

    def step(kj, slot, next_is_diagonal):
        for c in range(n_chains):
            st = qk_fn(c, kj + 1)
            if next_is_diagonal:
                st = jnp.where(visible_fn(), st, MASKED)
            st_refs[c][1 - slot] = st
            consume(c, kj, slot)

    def run_if(cond, fn):
        def body(_, carry):
            fn()
            return carry
        lax.fori_loop(0, cond.astype(jnp.int32), body, 0)

    tq = m_ref.shape[-1]
    m_ref[...] = jnp.full(m_ref.shape, MASKED, F32)
    l_ref[...] = jnp.zeros(l_ref.shape, F32)
    acc_ref[...] = jnp.zeros(acc_ref.shape, F32)

    run_if(qi == 0, lambda: produce(0, 0, True))
    run_if(qi > 0, lambda: produce(0, 0, False))

    n_pairs = jnp.maximum(qi - 1, 0) // 2
    kj0 = 2 * n_pairs

    def pair(i, carry):
        step(2 * i, 0, False)
        step(2 * i + 1, 1, False)
        return carry

    lax.fori_loop(0, n_pairs, pair, 0)
    two_left = (qi > 0) & (qi - kj0 == 2)
    one_left = (qi > 0) & (qi - kj0 == 1)
    run_if(two_left, lambda: step(kj0, 0, False))
    run_if(two_left, lambda: step(kj0 + 1, 1, True))
    run_if(one_left, lambda: step(kj0, 0, True))
    run_if(one_left, lambda: [consume(c, qi, 1) for c in range(n_chains)])
    run_if(jnp.logical_not(one_left), lambda: [consume(c, qi, 0) for c in range(n_chains)])


def _fox_kernel(q_ref, k_ref, kx_ref, vt_ref, o_ref, *scratch, tq, nh):
    qi = pl.program_id(2)
    st_refs, (m_ref, l_ref, acc_ref) = scratch[:nh], scratch[nh:]
    lane = lax.broadcasted_iota(jnp.int32, (tq, HEAD_DIM), 1)
    ones = (lane < N_SPLIT).astype(q_ref.dtype)

    def head(a):
        return slice(a * HEAD_DIM, (a + 1) * HEAD_DIM)

    def qk(a, kj):
        ks = pl.multiple_of(kj * tq, tq)
        q_aug = jnp.concatenate([q_ref[:, head(a)], ones], axis=1)
        k_aug = jnp.concatenate([k_ref[pl.ds(ks, tq), head(a)], kx_ref[pl.ds(ks, tq), head(a)]], axis=1)
        return _nt_dot(k_aug, q_aug)

    def visible():
        key = lax.broadcasted_iota(jnp.int32, (tq, tq), 0)
        qry = lax.broadcasted_iota(jnp.int32, (tq, tq), 1)
        return key <= qry

    _flash_loop(qi, nh, qk, lambda a, kj: vt_ref[kj, head(a), :], visible,
                st_refs, m_ref, l_ref, acc_ref)
    for a in range(nh):
        o_ref[:, head(a)] = (acc_ref[a] / l_ref[a]).T.astype(o_ref.dtype)


def _fox(qk, kx, vt, batch, seq, n_heads, tq, nh):
    m = qk.shape[0]
    nq = seq // tq
    gw = nh * HEAD_DIM
    ng = n_heads // nh
    return pl.pallas_call(
        functools.partial(_fox_kernel, tq=tq, nh=nh),
        grid=(batch, ng, nq),
        in_specs=[
            pl.BlockSpec((tq, gw), lambda b, g, i: (b * nq + i, g)),
            pl.BlockSpec((seq, gw), lambda b, g, i: (b, ng + g)),
            pl.BlockSpec((seq, gw), lambda b, g, i: (b, g)),
            pl.BlockSpec((nq, gw, tq), lambda b, g, i: (b, g, 0)),
        ],
        out_specs=pl.BlockSpec((tq, gw), lambda b, g, i: (b * nq + i, g)),
        out_shape=jax.ShapeDtypeStruct((m, n_heads * HEAD_DIM), BF16),
        scratch_shapes=_flash_scratch(nh, tq, HEAD_DIM),
        compiler_params=_params("arbitrary", "arbitrary", "arbitrary"),
        name="fox_attention",
    )(qk, qk, kx, vt)


def _diff_kernel(q_ref, k_ref, vt_ref, lam_ref, g_ref, o_ref, *scratch, tq, nh, lambda_init):
    qi = pl.program_id(2)
    n_maps = 2 * nh
    st_refs, (m_ref, l_ref, acc_ref) = scratch[:n_maps], scratch[n_maps:]
    hw = 2 * HEAD_DIM

    def cols(c):
        return slice(c * HEAD_DIM, (c + 1) * HEAD_DIM)

    def qk(c, kj):
        ks = pl.multiple_of(kj * tq, tq)
        return _nt_dot(k_ref[pl.ds(ks, tq), cols(c)], q_ref[:, cols(c)])

    def visible():
        key = lax.broadcasted_iota(jnp.int32, (tq, tq), 0) // CHUNK
        qry = lax.broadcasted_iota(jnp.int32, (tq, tq), 1) // CHUNK
        return key <= qry

    _flash_loop(qi, n_maps, qk, lambda c, kj: vt_ref[kj, (c // 2) * hw:(c // 2 + 1) * hw, :],
                visible, st_refs, m_ref, l_ref, acc_ref)

    lp = lam_ref[...]
    lam = (jnp.exp(jnp.sum(lp[0:1] * lp[1:2], axis=-1, keepdims=True))
           - jnp.exp(jnp.sum(lp[2:3] * lp[3:4], axis=-1, keepdims=True)) + lambda_init)
    for a in range(nh):
        y = acc_ref[2 * a] / l_ref[2 * a] - lam * (acc_ref[2 * a + 1] / l_ref[2 * a + 1])
        ms = jnp.mean(y * y, axis=0, keepdims=True)
        y = (y * lax.rsqrt(ms + EPS)).T * g_ref[...]
        o_ref[:, a * hw:(a + 1) * hw] = (y * (1.0 - lambda_init)).astype(o_ref.dtype)


def _diff(qk, vt, lam_p, subln, batch, seq, n_heads, tq, nh, lambda_init):
    m = qk.shape[0]
    nq = seq // tq
    hw = 2 * HEAD_DIM
    gw = nh * hw
    ng = n_heads // nh
    return pl.pallas_call(
        functools.partial(_diff_kernel, tq=tq, nh=nh, lambda_init=lambda_init),
        grid=(batch, ng, nq),
        in_specs=[
            pl.BlockSpec((tq, gw), lambda b, g, i: (b * nq + i, g)),
            pl.BlockSpec((seq, gw), lambda b, g, i: (b, ng + g)),
            pl.BlockSpec((nq, gw, tq), lambda b, g, i: (b, g, 0)),
            pl.BlockSpec((4, HEAD_DIM), lambda b, g, i: (0, 0)),
            pl.BlockSpec((1, hw), lambda b, g, i: (0, 0)),
        ],
        out_specs=pl.BlockSpec((tq, gw), lambda b, g, i: (b * nq + i, g)),
        out_shape=jax.ShapeDtypeStruct((m, n_heads * hw), BF16),
        scratch_shapes=_flash_scratch(2 * nh, tq, hw),
        compiler_params=_params("arbitrary", "arbitrary", "arbitrary"),
        name="diff_attention",
    )(qk, qk, vt, lam_p, subln.reshape(1, hw))


def _merge_kernel(ya_ref, yb_ref, wa_ref, wb_ref, sa_ref, sb_ref, o_ref):
    pa = jnp.dot(ya_ref[...], wa_ref[...], preferred_element_type=F32)
    pb = jnp.dot(yb_ref[...], wb_ref[...], preferred_element_type=F32)
    o_ref[...] = (sa_ref[...].astype(F32) * pa + sb_ref[...].astype(F32) * pb).astype(o_ref.dtype)


def _merge(ya, yb, wa, wb, gates, bm, bn):
    m = ya.shape[0]
    d = wa.shape[1]
    nb = d // bn
    return pl.pallas_call(
        _merge_kernel,
        grid=(m // bm, nb),
        in_specs=[
            pl.BlockSpec((bm, ya.shape[1]), lambda i, j: (i, 0)),
            pl.BlockSpec((bm, yb.shape[1]), lambda i, j: (i, 0)),
            pl.BlockSpec((wa.shape[0], bn), lambda i, j: (0, j)),
            pl.BlockSpec((wb.shape[0], bn), lambda i, j: (0, j)),
            pl.BlockSpec((bm, bn), lambda i, j: (i, j)),
            pl.BlockSpec((bm, bn), lambda i, j: (i, nb + j)),
        ],
        out_specs=pl.BlockSpec((bm, bn), lambda i, j: (i, j)),
        out_shape=jax.ShapeDtypeStruct((m, d), BF16),
        compiler_params=_params("arbitrary", "arbitrary"),
        name="gated_merge",
    )(ya, yb, wa, wb, gates, gates)


def _ffn_width(f):
    return _round_up(f, 1024) if f >= 1024 else f


def _ffn_in_cast(w_in, layer):
    _, d, f2 = w_in.shape
    f = f2 // 2
    return _StripCast(w_in, layer, 0, d, d, ((0, f, _ffn_width(f)), (f, f, _ffn_width(f))))


def _ffn(x2, gain, w_in, pre_cast, w_out, layer, mod3, j0, seq, next_casts):
    m, d = x2.shape
    f = w_out.shape[1]
    fp = _ffn_width(f)
    bm = _tile(seq, 1024)
    h = _norm_mod(x2, gain, mod3, N_MOD, j0, j0 + 1, seq, BF16)
    wo_cast = _StripCast(w_out, layer, 0, f, fp, ((0, d, d),))
    if pre_cast is None:
        a0, wg, wu = _ffn_in_first(h, w_in, layer, bm)
    else:
        a0, (wg, wu) = None, pre_cast
    a, ((wo,),) = _ffn_in(h, wg, wu, bm, _tile(fp, 512), (wo_cast,), a0)
    bk = fp // 4 if (fp // 4) % LANE == 0 else fp
    return _res_matmul(a, wo, x2, mod3, j0 + 2, 0.5, seq, bm, _tile(d, 1024), bk, "ffn_out",
                       casts=next_casts)


def kernel(x, c, positions, ada_w, ada_b, norm_ffn1, ffn1_w_in, ffn1_w_out, norm_mix, w_in, b_forget, b_gate, diff_lambda, diff_subln, w_o_fox, w_o_diff, w_out, norm_ffn2, ffn2_w_in, ffn2_w_out, final_ada_w, final_ada_b, norm_final):
    batch, seq, d = x.shape
    depth = ada_w.shape[0]
    m = batch * seq
    n_fox = b_forget.shape[1]
    fox_w = n_fox * HEAD_DIM
    diff_w = w_o_diff.shape[1]
    n_diff = diff_w // (2 * HEAD_DIM)
    o_ff = 3 * fox_w
    o_dq = o_ff + n_fox
    n_rest = 3 * diff_w + 2 * d
    assert w_in.shape[2] == o_dq + n_rest
    assert n_fox <= LANE and seq % CHUNK == 0

    c_pad = jnp.pad(c, ((0, -batch % 8), (0, 0)))
    x2 = x.reshape(m, d)
    w_in_t = jnp.swapaxes(w_in, 1, 2)

    inv_freq = ROPE_THETA ** (-jnp.arange(0, ROPE_DIM, 2, dtype=F32) / ROPE_DIM)
    ang = positions.astype(F32).reshape(m, 1) * inv_freq
    cos, sin = jnp.cos(ang), jnp.sin(ang)
    cosf = jnp.concatenate([cos, cos, jnp.ones((m, HEAD_DIM - ROPE_DIM), F32)], axis=-1)
    sinf = jnp.concatenate([-sin, sin, jnp.zeros((m, HEAD_DIM - ROPE_DIM), F32)], axis=-1)

    bm = _tile(seq, 1024)
    tq = _tile(seq, 512)
    bn_p = _tile(math.gcd(fox_w, diff_w), 1024)
    for l in range(depth):
        lambda_init = 0.8 - 0.6 * math.exp(-0.3 * l)
        mod = _ada(c_pad, ada_w[l], ada_b[l])[:batch]
        mod3 = mod.reshape(batch * N_MOD, 1, d)

        mixer_casts = (_StripCast(w_in_t, l, 0, o_ff, o_ff, ((0, d, d),)),
                       _StripCast(w_in_t, l, o_dq, n_rest, n_rest, ((0, d, d),)))
        x2, ((w_front,), (w_rest,)) = _ffn(x2, norm_ffn1[l], ffn1_w_in, None, ffn1_w_out, l, mod3,
                                           0, seq, mixer_casts)

        h = _norm_mod(x2, norm_mix[l], mod3, N_MOD, 3, 4, seq, BF16)
        w_ff_t = jnp.pad(w_in_t[l, o_ff:o_dq, :].astype(BF16), ((0, LANE - n_fox), (0, 0)))
        b_ff = jnp.pad(b_forget[l].astype(F32), (0, LANE - n_fox)).reshape(1, LANE)

        qk_f, ((wa,),) = _proj_plain(h, w_front, 0, 2 * fox_w, fox_w // bn_p, bm, bn_p,
                                     (_whole(w_o_fox, l),))
        qk_d, ((wb,),) = _proj_rope(h, w_rest, 0, 2 * diff_w, diff_w // bn_p, cosf, sinf, bm, bn_p,
                                    (_whole(w_o_diff, l),))
        gates, ((wg, wu),) = _proj_gate(h, w_rest, 3 * diff_w, 2 * d, b_gate[l], bm, bn_p,
                                        (_ffn_in_cast(ffn2_w_in, l),))
        vt_f, ((wo,),) = _vt_proj(h, w_front, 2 * fox_w, fox_w, bm, bn_p, tq, (_whole(w_out, l),))
        vt_d, _ = _vt_proj(h, w_rest, 2 * diff_w, diff_w, bm, bn_p, tq)
        kx = _forget(h, w_ff_t, b_ff, n_fox, batch, seq)

        ya = _fox(qk_f, kx, vt_f, batch, seq, n_fox, tq, math.gcd(n_fox, FOX_HEADS_PER_STEP))
        yb = _diff(qk_d, vt_d, diff_lambda[l].astype(F32), diff_subln[l], batch, seq, n_diff,
                   tq, math.gcd(n_diff, DIFF_HEADS_PER_STEP), lambda_init)
        merged = _merge(ya, yb, wa, wb, gates, bm, _tile(d, 1024))
        x2, _ = _res_matmul(merged, wo, x2, mod3, 5, 1.0, seq, bm, _tile(d, 1024), d, "mixer_out")

        x2, _ = _ffn(x2, norm_ffn2[l], ffn2_w_in, (wg, wu), ffn2_w_out, l, mod3, 6, seq, ())

    fmod = _ada(c_pad, final_ada_w, final_ada_b)[:batch]
    out = _norm_mod(x2, norm_final, fmod.reshape(batch * 2, 1, d), 2, 0, 1, seq, F32)
    return out.reshape(batch, seq, d)
```

```python
import functools
import math
from typing import NamedTuple

import numpy as np
import jax
import jax.numpy as jnp
from jax import lax
from jax.experimental import pallas as pl
from jax.experimental.pallas import tpu as pltpu

HEAD_DIM = 128
CHUNK = 64
ROPE_THETA = 500000.0
ROPE_DIM = HEAD_DIM // 4
ROPE_HALF = ROPE_DIM // 2
EPS = 1e-6
N_MOD = 9
LOG2E = 1.4426950408889634
MASKED = -1e30
LANE = 128
VMEM_LIMIT = 56 * 1024 * 1024
BF16_ROWS = 16
CAST_BLOCK_ELEMS = 2 * 1024 * 1024
HOSTED_STRIP_ELEMS = 1024 * 1024
ADA_HOSTED_COLS = 256
N_SPLIT = 3
FORGET_SUBTILE = 512
FOX_HEADS_PER_STEP = 4
DIFF_HEADS_PER_STEP = 2

F32 = jnp.float32
BF16 = jnp.bfloat16


def _params(*sem):
    return pltpu.CompilerParams(dimension_semantics=sem, vmem_limit_bytes=VMEM_LIMIT)


def _tile(n, pref):
    t = min(n, pref)
    while n % t:
        t //= 2
    return t


def _round_up(n, m):
    return (n + m - 1) // m * m


def _sigmoid(x):
    return 0.5 * jnp.tanh(0.5 * x) + 0.5


def _nt_dot(a, b):
    return lax.dot_general(a, b, (((1,), (1,)), ((), ())), preferred_element_type=F32)


class _StripCast(NamedTuple):
    src: jax.Array
    layer: int
    row0: int
    rows: int
    rows_out: int
    pieces: tuple


def _whole(src, layer):
    _, r, c = src.shape
    return _StripCast(src, layer, 0, r, r, ((0, c, c),))


def _strip_rows(cast, lo, hi):
    g = math.gcd(math.gcd(cast.row0, cast.rows), cast.rows_out)
    for rs in range(_round_up(max(lo, 1), BF16_ROWS), min(hi, g) + 1, BF16_ROWS):
        if g % rs == 0:
            return rs
    return None


def _linear_step(grid, ids):
    t = ids[0]
    for g, i in zip(grid[1:], ids[1:]):
        t = t * g + i
    return t


def _strip_specs(cast, rs, grid):
    first, n_valid, n_strips = cast.row0 // rs, cast.rows // rs, cast.rows_out // rs
    cols = cast.src.shape[2]
    src_spec = pl.BlockSpec(
        (None, rs, cols),
        lambda *ids: (cast.layer, first + jnp.minimum(_linear_step(grid, ids), n_valid - 1), 0))
    out_specs = [pl.BlockSpec((rs, nco), lambda *ids: (jnp.minimum(_linear_step(grid, ids), n_strips - 1), 0))
                 for _, _, nco in cast.pieces]
    out_shapes = [jax.ShapeDtypeStruct((cast.rows_out, nco), BF16) for _, _, nco in cast.pieces]
    return src_spec, out_specs, out_shapes


def _strip_cast_body(cast, rs, grid, src_ref, dst_refs):
    t = _linear_step(grid, [pl.program_id(a) for a in range(len(grid))])
    strip = src_ref[...]
    for (c0, n, nco), dst in zip(cast.pieces, dst_refs):
        piece = strip[:, c0:c0 + n].astype(dst.dtype)
        if cast.rows < cast.rows_out:
            piece = jnp.where(t < cast.rows // rs, piece, jnp.zeros_like(piece))
        if nco > n:
            dst[:, :n] = piece
            dst[:, n:] = jnp.zeros((rs, nco - n), dst.dtype)
        else:
            dst[...] = piece


def _run_cast(cast):
    rs = _strip_rows(cast, 1, max(CAST_BLOCK_ELEMS // cast.src.shape[2], BF16_ROWS))
    rs = rs or _strip_rows(cast, 1, cast.rows_out)
    if rs is None:
        w = cast.src[cast.layer, cast.row0:cast.row0 + cast.rows]
        return tuple(jnp.pad(w[:, c0:c0 + n].astype(BF16), ((0, cast.rows_out - cast.rows), (0, nco - n)))
                     for c0, n, nco in cast.pieces)
    grid = (cast.rows_out // rs,)
    src_spec, out_specs, out_shapes = _strip_specs(cast, rs, grid)

    def body(src_ref, *dst_refs):
        _strip_cast_body(cast, rs, grid, src_ref, dst_refs)

    return tuple(pl.pallas_call(
        body, grid=grid, in_specs=[src_spec], out_specs=out_specs, out_shape=out_shapes,
        compiler_params=_params("arbitrary"), name="weight_cast",
    )(cast.src))


def _hosted_call(body, grid, in_specs, out_spec, out_shape, inputs, name, scratch=(), casts=(),
                 jobs=()):
    steps = math.prod(grid)
    hosted, strip_rows = [], []
    for cast in casts:
        rs = _strip_rows(cast, pl.cdiv(cast.rows_out, steps), cast.rows_out)
        if rs is not None and rs * cast.src.shape[2] <= HOSTED_STRIP_ELEMS:
            hosted.append(cast)
            strip_rows.append(rs)
    specs = [_strip_specs(cast, rs, grid) for cast, rs in zip(hosted, strip_rows)]
    riders = [job for job in jobs if _ada_blocks(job, ADA_HOSTED_COLS) in range(1, steps + 1)]
    rider_specs = [_ada_specs(job, ADA_HOSTED_COLS, lambda *ids: _linear_step(grid, ids))
                   for job in riders]
    n_in, n_src, n_job = len(in_specs), len(hosted), len(riders)

    def kernel_body(*refs):
        ins, src_refs = refs[:n_in], refs[n_in:n_in + n_src]
        job_ins = refs[n_in + n_src:n_in + n_src + 3 * n_job]
        o_ref = refs[n_in + n_src + 3 * n_job]
        pos = n_in + n_src + 3 * n_job + 1
        dst_groups = []
        for cast in hosted:
            dst_groups.append(refs[pos:pos + len(cast.pieces)])
            pos += len(cast.pieces)
        job_outs = refs[pos:pos + n_job]
        pos += n_job
        body(*ins, o_ref, *refs[pos:])
        for cast, rs, src_ref, dst_refs in zip(hosted, strip_rows, src_refs, dst_groups):
            _strip_cast_body(cast, rs, grid, src_ref, dst_refs)
        t = _linear_step(grid, [pl.program_id(a) for a in range(len(grid))])
        for k, job in enumerate(riders):
            @pl.when(t < _ada_blocks(job, ADA_HOSTED_COLS))
            def _(k=k):
                _ada_kernel(*job_ins[3 * k:3 * k + 3], job_outs[k])

    outs = pl.pallas_call(
        kernel_body, grid=grid,
        in_specs=list(in_specs) + [s[0] for s in specs] + [i for r in rider_specs for i in r[0]],
        out_specs=[out_spec] + [o for s in specs for o in s[1]] + [r[1] for r in rider_specs],
        out_shape=[out_shape] + [o for s in specs for o in s[2]] + [r[2] for r in rider_specs],
        scratch_shapes=list(scratch),
        compiler_params=_params(*(("arbitrary",) * len(grid))), name=name,
    )(*inputs, *[cast.src for cast in hosted], *[a for job in riders for a in (job.c_pad, job.w, job.b)])

    results, pos = [], 1
    for cast in casts:
        if any(cast is hc for hc in hosted):
            results.append(tuple(outs[pos:pos + len(cast.pieces)]))
            pos += len(cast.pieces)
        else:
            results.append(_run_cast(cast))
    pos = 1 + sum(len(cast.pieces) for cast in hosted)
    for job in jobs:
        if any(job is r for r in riders):
            results.append((outs[pos],))
            pos += 1
        else:
            results.append((_ada(job),))
    return outs[0], tuple(results)


class _AdaJob(NamedTuple):
    c_pad: jax.Array
    w: jax.Array
    b: jax.Array
    layer: int
    col0: int
    n: int


def _ada_blocks(job, tn):
    return job.n // tn if job.col0 % tn == 0 and job.n % tn == 0 else 0


def _ada_specs(job, tn, step_of):
    rows, d = job.c_pad.shape
    first, nb = job.col0 // tn, job.n // tn

    def col(*ids):
        return first + jnp.minimum(step_of(*ids), nb - 1)

    in_specs = [
        pl.BlockSpec((rows, d), lambda *ids: (0, 0)),
        pl.BlockSpec((None, d, tn), lambda *ids: (job.layer, 0, col(*ids))),
        pl.BlockSpec((None, 1, tn), lambda *ids: (job.layer, 0, col(*ids))),
    ]
    out_spec = pl.BlockSpec((rows, tn), lambda *ids: (0, col(*ids) - first))
    return in_specs, out_spec, jax.ShapeDtypeStruct((rows, job.n), F32)


def _ada_kernel(c_ref, w_ref, b_ref, o_ref):
    c = c_ref[...]
    sc = (c * jax.nn.sigmoid(c)).astype(BF16)
    o_ref[...] = jnp.dot(sc, w_ref[...].astype(BF16), preferred_element_type=F32) + b_ref[...]


def _ada(job):
    tn = _tile(math.gcd(job.col0, job.n) if job.col0 else job.n, 512)
    in_specs, out_spec, out_shape = _ada_specs(job, tn, lambda j: j)
    return pl.pallas_call(
        _ada_kernel, grid=(job.n // tn,), in_specs=in_specs, out_specs=out_spec,
        out_shape=out_shape, compiler_params=_params("arbitrary"), name="ada_matvec",
    )(job.c_pad, job.w, job.b)


def _norm_mod_kernel(x_ref, g_ref, sh_ref, sc_ref, o_ref):
    x = x_ref[...]
    ms = jnp.mean(x * x, axis=-1, keepdims=True)
    y = x * lax.rsqrt(ms + EPS) * g_ref[...]
    o_ref[...] = (y * (1.0 + sc_ref[0]) + sh_ref[0]).astype(o_ref.dtype)


def _norm_mod(x2, gain, mod3, n_mod, shift_j, scale_j, seq, out_dtype):
    m, d = x2.shape
    tm = _tile(seq, 512)
    per_b = seq // tm
    return pl.pallas_call(
        _norm_mod_kernel,
        grid=(m // tm,),
        in_specs=[
            pl.BlockSpec((tm, d), lambda i: (i, 0)),
            pl.BlockSpec((1, d), lambda i: (0, 0)),
            pl.BlockSpec((1, 1, d), lambda i: ((i // per_b) * n_mod + shift_j, 0, 0)),
            pl.BlockSpec((1, 1, d), lambda i: ((i // per_b) * n_mod + scale_j, 0, 0)),
        ],
        out_specs=pl.BlockSpec((tm, d), lambda i: (i, 0)),
        out_shape=jax.ShapeDtypeStruct((m, d), out_dtype),
        compiler_params=_params("arbitrary"),
        name="norm_mod",
    )(x2, gain.reshape(1, d), mod3, mod3)


def _ffn_in_kernel(h_ref, wg_ref, wu_ref, o_ref):
    h = h_ref[...]
    g = jnp.dot(h, wg_ref[...], preferred_element_type=F32)
    u = jnp.dot(h, wu_ref[...], preferred_element_type=F32)
    o_ref[...] = (g * _sigmoid(g) * u).astype(o_ref.dtype)


def _ffn_in_resume_kernel(h_ref, wg_ref, wu_ref, a0_ref, o_ref):
    @pl.when(pl.program_id(0) == 0)
    def _():
        o_ref[...] = a0_ref[...]

    @pl.when(pl.program_id(0) > 0)
    def _():
        _ffn_in_kernel(h_ref, wg_ref, wu_ref, o_ref)


def _ffn_in(h, wg, wu, bm, bn, casts, a0=None, jobs=()):
    m, d = h.shape
    fp = wg.shape[1]
    nj = fp // bn
    resume = a0 is not None
    first = 1 if resume and m > bm else 0
    in_specs = [
        pl.BlockSpec((bm, d), lambda i, j: (jnp.maximum(i, first), 0)),
        pl.BlockSpec((d, bn), lambda i, j: (0, j)),
        pl.BlockSpec((d, bn), lambda i, j: (0, j)),
    ]
    if resume:
        in_specs.append(pl.BlockSpec((bm, bn), lambda i, j: (0, jnp.where(i == 0, j, nj - 1))))
    return _hosted_call(
        _ffn_in_resume_kernel if resume else _ffn_in_kernel,
        (m // bm, nj), in_specs,
        pl.BlockSpec((bm, bn), lambda i, j: (i, j)),
        jax.ShapeDtypeStruct((m, fp), BF16),
        (h, wg, wu) + ((a0,) if resume else ()), "ffn_in", casts=casts, jobs=jobs)


def _ffn_in_first_kernel(h_ref, wg32_ref, wu32_ref, o_ref, wg_ref, wu_ref, *, n_valid):
    keep = pl.program_id(0) < n_valid
    wg = jnp.where(keep, wg32_ref[...], 0.0).astype(wg_ref.dtype)
    wu = jnp.where(keep, wu32_ref[...], 0.0).astype(wu_ref.dtype)
    wg_ref[...] = wg
    wu_ref[...] = wu
    h = h_ref[...]
    g = jnp.dot(h, wg, preferred_element_type=F32)
    u = jnp.dot(h, wu, preferred_element_type=F32)
    o_ref[...] = (g * _sigmoid(g) * u).astype(o_ref.dtype)


def _ffn_in_first(h, w_in, layer, bm):
    m, d = h.shape
    f = w_in.shape[2] // 2
    fp = _ffn_width(f)
    bn = _tile(math.gcd(f, fp), 256)
    nv = f // bn
    return pl.pallas_call(
        functools.partial(_ffn_in_first_kernel, n_valid=nv),
        grid=(fp // bn,),
        in_specs=[
            pl.BlockSpec((bm, d), lambda j: (0, 0)),
            pl.BlockSpec((None, d, bn), lambda j: (layer, 0, jnp.minimum(j, nv - 1))),
            pl.BlockSpec((None, d, bn), lambda j: (layer, 0, nv + jnp.minimum(j, nv - 1))),
        ],
        out_specs=[
            pl.BlockSpec((bm, bn), lambda j: (0, j)),
            pl.BlockSpec((d, bn), lambda j: (0, j)),
            pl.BlockSpec((d, bn), lambda j: (0, j)),
        ],
        out_shape=[jax.ShapeDtypeStruct((bm, fp), BF16),
                   jax.ShapeDtypeStruct((d, fp), BF16), jax.ShapeDtypeStruct((d, fp), BF16)],
        compiler_params=_params("arbitrary"),
        name="ffn_in_first",
    )(h, w_in, w_in)


def _res_matmul_kernel(a_ref, w_ref, res_ref, gate_ref, o_ref, acc_ref, *, coef, nk):
    k = pl.program_id(2)

    def partial_product():
        return jnp.dot(a_ref[...], w_ref[...], preferred_element_type=F32)

    def finish(total):
        o_ref[...] = res_ref[...] + (coef * gate_ref[0]) * total

    if nk == 1:
        finish(partial_product())
        return

    @pl.when(k == 0)
    def _():
        acc_ref[...] = partial_product()

    @pl.when((k > 0) & (k < nk - 1))
    def _():
        acc_ref[...] += partial_product()

    @pl.when(k == nk - 1)
    def _():
        finish(acc_ref[...] + partial_product())


def _res_matmul(a, w, res, mod3, gate_j, coef, seq, bm, bn, bk, name, casts=()):
    m, kdim = a.shape
    n = w.shape[1]
    per_b = seq // bm
    return _hosted_call(
        functools.partial(_res_matmul_kernel, coef=coef, nk=kdim // bk),
        (m // bm, n // bn, kdim // bk),
        [
            pl.BlockSpec((bm, bk), lambda i, j, k: (i, k)),
            pl.BlockSpec((bk, bn), lambda i, j, k: (k, j)),
            pl.BlockSpec((bm, bn), lambda i, j, k: (i, j)),
            pl.BlockSpec((1, 1, bn), lambda i, j, k: ((i // per_b) * N_MOD + gate_j, 0, j)),
        ],
        pl.BlockSpec((bm, bn), lambda i, j, k: (i, j)),
        jax.ShapeDtypeStruct((m, n), F32),
        (a, w, res, mod3), name, scratch=[pltpu.VMEM((bm, bn), F32)], casts=casts)


QSCALE = HEAD_DIM ** -0.5 * LOG2E


def _proj_call(body, h, wt, row0, n, bm, bn, extra_in, extra_specs, name, casts):
    m, d = h.shape
    b0 = row0 // bn
    return _hosted_call(
        body,
        (m // bm, n // bn),
        [
            pl.BlockSpec((bm, d), lambda i, j: (i, 0)),
            pl.BlockSpec((bn, d), lambda i, j: (b0 + j, 0)),
        ] + extra_specs,
        pl.BlockSpec((bm, bn), lambda i, j: (i, j)),
        jax.ShapeDtypeStruct((m, n), BF16),
        (h, wt, *extra_in), name, casts=casts)


def _block_scale(n_scaled):
    return jnp.where(pl.program_id(1) < n_scaled, QSCALE, 1.0).astype(F32)


def _proj_plain_kernel(h_ref, wt_ref, o_ref, *, n_scaled):
    acc = _nt_dot(h_ref[...], wt_ref[...])
    o_ref[...] = (acc * _block_scale(n_scaled)).astype(o_ref.dtype)


def _proj_rope_kernel(h_ref, wt_ref, cos_ref, sin_ref, o_ref, *, n_scaled):
    acc = _nt_dot(h_ref[...], wt_ref[...])
    cosf = cos_ref[...] * _block_scale(n_scaled)
    sinf = sin_ref[...] * _block_scale(n_scaled)
    lane = lax.broadcasted_iota(jnp.int32, cosf.shape, 1)
    for c in range(acc.shape[1] // HEAD_DIM):
        cols = slice(c * HEAD_DIM, (c + 1) * HEAD_DIM)
        t = acc[:, cols]
        partner = jnp.where(lane < ROPE_HALF,
                            pltpu.roll(t, HEAD_DIM - ROPE_HALF, 1),
                            pltpu.roll(t, ROPE_HALF, 1))
        o_ref[:, cols] = (t * cosf + partner * sinf).astype(o_ref.dtype)


def _proj_gate_kernel(h_ref, wt_ref, b_ref, o_ref):
    acc = _nt_dot(h_ref[...], wt_ref[...])
    o_ref[...] = _sigmoid(acc + b_ref[...]).astype(o_ref.dtype)


def _proj_plain(h, wt, row0, n, n_scaled, bm, bn, casts):
    return _proj_call(functools.partial(_proj_plain_kernel, n_scaled=n_scaled),
                      h, wt, row0, n, bm, bn, [], [], "proj_plain", casts)


def _proj_rope(h, wt, row0, n, n_scaled, cosf, sinf, bm, bn, casts):
    table = pl.BlockSpec((bm, HEAD_DIM), lambda i, j: (i, 0))
    return _proj_call(functools.partial(_proj_rope_kernel, n_scaled=n_scaled),
                      h, wt, row0, n, bm, bn, [cosf, sinf], [table, table], "proj_rope", casts)


def _proj_gate(h, wt, row0, n, bias, bm, bn, casts):
    return _proj_call(_proj_gate_kernel, h, wt, row0, n, bm, bn, [bias.reshape(1, n)],
                      [pl.BlockSpec((1, bn), lambda i, j: (0, j))], "proj_gate", casts)


def _vt_proj_kernel(wt_ref, h_ref, o_ref, *, tk):
    r = _nt_dot(wt_ref[...], h_ref[...])
    for t in range(o_ref.shape[0]):
        o_ref[t] = r[:, t * tk:(t + 1) * tk].astype(o_ref.dtype)


def _vt_proj(h, wt, row0, n, bm, bn, tk, casts=()):
    m, d = h.shape
    b0 = row0 // bn
    return _hosted_call(
        functools.partial(_vt_proj_kernel, tk=tk),
        (m // bm, n // bn),
        [
            pl.BlockSpec((bn, d), lambda i, j: (b0 + j, 0)),
            pl.BlockSpec((bm, d), lambda i, j: (i, 0)),
        ],
        pl.BlockSpec((bm // tk, bn, tk), lambda i, j: (i, j, 0)),
        jax.ShapeDtypeStruct((m // tk, n, tk), BF16),
        (wt, h), "proj_vt", casts=casts)


def _forget_kernel(h_ref, w_ref, b_ref, e_ref, o_ref, carry_ref):
    @pl.when(pl.program_id(1) == 0)
    def _():
        carry_ref[...] = jnp.zeros_like(carry_ref)

    ts = FORGET_SUBTILE if h_ref.shape[0] % FORGET_SUBTILE == 0 else h_ref.shape[0]
    row = lax.broadcasted_iota(jnp.int32, (ts, ts), 0)
    col = lax.broadcasted_iota(jnp.int32, (ts, ts), 1)
    tri = (row >= col).astype(F32)
    subs = [slice(s * ts, (s + 1) * ts) for s in range(h_ref.shape[0] // ts)]
    ffs = [_nt_dot(h_ref[rows, :], w_ref[...]) + b_ref[...] for rows in subs]
    logfs = [jnp.minimum(ff, 0.0) - jnp.log1p(jnp.exp(-jnp.abs(ff))) for ff in ffs]
    sums = [jnp.dot(tri, logf, preferred_element_type=F32, precision=lax.Precision.HIGHEST)
            for logf in logfs]
    carry = carry_ref[...]
    stacked = []
    for within in sums:
        cum = within + carry
        carry = cum[ts - 1:ts, :]
        rest = cum * (-LOG2E)
        pieces = []
        for _ in range(N_SPLIT):
            p = rest.astype(BF16)
            pieces.append(p)
            rest = rest - p.astype(F32)
        stacked.append(jnp.concatenate(pieces, axis=1))
    carry_ref[...] = carry
    for rows, pieces in zip(subs, stacked):
        spread = jnp.dot(pieces, e_ref[...], preferred_element_type=F32)
        o_ref[rows, :] = spread.astype(o_ref.dtype)


def _forget(h, w_ff_t, b_ff, n_heads, batch, seq):
    m, d = h.shape
    tm = _tile(seq, 2 * FORGET_SUBTILE)
    per_b = seq // tm
    route = np.zeros((N_SPLIT * LANE, n_heads * HEAD_DIM), np.float32)
    for p in range(N_SPLIT):
        for hh in range(n_heads):
            route[p * LANE + hh, hh * HEAD_DIM + p] = 1.0
    return pl.pallas_call(
        _forget_kernel,
        grid=(batch, per_b),
        in_specs=[
            pl.BlockSpec((tm, d), lambda b, i: (b * per_b + i, 0)),
            pl.BlockSpec((LANE, d), lambda b, i: (0, 0)),
            pl.BlockSpec((1, LANE), lambda b, i: (0, 0)),
            pl.BlockSpec(route.shape, lambda b, i: (0, 0)),
        ],
        out_specs=pl.BlockSpec((tm, n_heads * HEAD_DIM), lambda b, i: (b * per_b + i, 0)),
        out_shape=jax.ShapeDtypeStruct((m, n_heads * HEAD_DIM), BF16),
        scratch_shapes=[pltpu.VMEM((1, LANE), F32)],
        compiler_params=_params("arbitrary", "arbitrary"),
        name="forget_cumsum",
    )(h, w_ff_t, b_ff, jnp.asarray(route, BF16))


def _online_softmax_step(st, vt, m, l, acc):
    m_new = jnp.maximum(m, jnp.max(st, axis=0, keepdims=True))
    alpha = jnp.exp2(m - m_new)
    p = jnp.exp2(st - m_new)
    l_new = alpha * l + jnp.sum(p, axis=0, keepdims=True)
    acc_new = alpha * acc + jnp.dot(vt, p.astype(vt.dtype), preferred_element_type=F32)
    return m_new, l_new, acc_new


def _flash_scratch(n_chains, tq, dv):
    return ([pltpu.VMEM((2, tq, tq), F32) for _ in range(n_chains)]
            + [pltpu.VMEM((n_chains, 1, tq), F32), pltpu.VMEM((n_chains, 1, tq), F32),
               pltpu.VMEM((n_chains, dv, tq), F32)])


def _flash_loop(qi, n_chains, qk_fn, vt_fn, visible_fn, st_refs, m_ref, l_ref, acc_ref):
    def produce(kj, slot, diagonal):
        for c in range(n_chains):
            st = qk_fn(c, kj)
            if diagonal:
                st = jnp.where(visible_fn(), st, MASKED)
            st_refs[c][slot] = st

    def consume(c, kj, slot):
        m, l, acc = _online_softmax_step(st_refs[c][slot], vt_fn(c, kj), m_ref[c], l_ref[c], acc_ref[c])
        m_ref[c] = m
        l_ref[c] = l
        acc_ref[c] = acc

    def step(kj, slot, next_is_diagonal):
        for c in range(n_chains):
            st = qk_fn(c, kj + 1)
            if next_is_diagonal:
                st = jnp.where(visible_fn(), st, MASKED)
            st_refs[c][1 - slot] = st
            consume(c, kj, slot)

    def run_if(cond, fn):
        def body(_, carry):
            fn()
            return carry
        lax.fori_loop(0, cond.astype(jnp.int32), body, 0)

    tq = m_ref.shape[-1]
    m_ref[...] = jnp.full(m_ref.shape, MASKED, F32)
    l_ref[...] = jnp.zeros(l_ref.shape, F32)
    acc_ref[...] = jnp.zeros(acc_ref.shape, F32)

    run_if(qi == 0, lambda: produce(0, 0, True))
    run_if(qi > 0, lambda: produce(0, 0, False))

    n_pairs = jnp.maximum(qi - 1, 0) // 2
    kj0 = 2 * n_pairs

    def pair(i, carry):
        step(2 * i, 0, False)
        step(2 * i + 1, 1, False)
        return carry

    lax.fori_loop(0, n_pairs, pair, 0)
    two_left = (qi > 0) & (qi - kj0 == 2)
    one_left = (qi > 0) & (qi - kj0 == 1)
    run_if(two_left, lambda: step(kj0, 0, False))
    run_if(two_left, lambda: step(kj0 + 1, 1, True))
    run_if(one_left, lambda: step(kj0, 0, True))
    run_if(one_left, lambda: [consume(c, qi, 1) for c in range(n_chains)])
    run_if(jnp.logical_not(one_left), lambda: [consume(c, qi, 0) for c in range(n_chains)])


def _fox_kernel(q_ref, k_ref, kx_ref, vt_ref, o_ref, *scratch, tq, nh):
    qi = pl.program_id(2)
    st_refs, (m_ref, l_ref, acc_ref) = scratch[:nh], scratch[nh:]
    lane = lax.broadcasted_iota(jnp.int32, (tq, HEAD_DIM), 1)
    ones = (lane < N_SPLIT).astype(q_ref.dtype)

    def head(a):
        return slice(a * HEAD_DIM, (a + 1) * HEAD_DIM)

    def qk(a, kj):
        ks = pl.multiple_of(kj * tq, tq)
        q_aug = jnp.concatenate([q_ref[:, head(a)], ones], axis=1)
        k_aug = jnp.concatenate([k_ref[pl.ds(ks, tq), head(a)], kx_ref[pl.ds(ks, tq), head(a)]], axis=1)
        return _nt_dot(k_aug, q_aug)

    def visible():
        key = lax.broadcasted_iota(jnp.int32, (tq, tq), 0)
        qry = lax.broadcasted_iota(jnp.int32, (tq, tq), 1)
        return key <= qry

    _flash_loop(qi, nh, qk, lambda a, kj: vt_ref[kj, head(a), :], visible,
                st_refs, m_ref, l_ref, acc_ref)
    for a in range(nh):
        o_ref[:, head(a)] = (acc_ref[a] / l_ref[a]).T.astype(o_ref.dtype)


def _fox(qk, kx, vt, batch, seq, n_heads, tq, nh):
    m = qk.shape[0]
    nq = seq // tq
    gw = nh * HEAD_DIM
    ng = n_heads // nh
    return pl.pallas_call(
        functools.partial(_fox_kernel, tq=tq, nh=nh),
        grid=(batch, ng, nq),
        in_specs=[
            pl.BlockSpec((tq, gw), lambda b, g, i: (b * nq + i, g)),
            pl.BlockSpec((seq, gw), lambda b, g, i: (b, ng + g)),
            pl.BlockSpec((seq, gw), lambda b, g, i: (b, g)),
            pl.BlockSpec((nq, gw, tq), lambda b, g, i: (b, g, 0)),
        ],
        out_specs=pl.BlockSpec((tq, gw), lambda b, g, i: (b * nq + i, g)),
        out_shape=jax.ShapeDtypeStruct((m, n_heads * HEAD_DIM), BF16),
        scratch_shapes=_flash_scratch(nh, tq, HEAD_DIM),
        compiler_params=_params("arbitrary", "arbitrary", "arbitrary"),
        name="fox_attention",
    )(qk, qk, kx, vt)


def _diff_kernel(q_ref, k_ref, vt_ref, lam_ref, g_ref, o_ref, *scratch, tq, nh, lambda_init):
    qi = pl.program_id(2)
    n_maps = 2 * nh
    st_refs, (m_ref, l_ref, acc_ref) = scratch[:n_maps], scratch[n_maps:]
    hw = 2 * HEAD_DIM

    def cols(c):
        return slice(c * HEAD_DIM, (c + 1) * HEAD_DIM)

    def qk(c, kj):
        ks = pl.multiple_of(kj * tq, tq)
        return _nt_dot(k_ref[pl.ds(ks, tq), cols(c)], q_ref[:, cols(c)])

    def visible():
        key = lax.broadcasted_iota(jnp.int32, (tq, tq), 0) // CHUNK
        qry = lax.broadcasted_iota(jnp.int32, (tq, tq), 1) // CHUNK
        return key <= qry

    _flash_loop(qi, n_maps, qk, lambda c, kj: vt_ref[kj, (c // 2) * hw:(c // 2 + 1) * hw, :],
                visible, st_refs, m_ref, l_ref, acc_ref)

    lp = lam_ref[...]
    lam = (jnp.exp(jnp.sum(lp[0:1] * lp[1:2], axis=-1, keepdims=True))
           - jnp.exp(jnp.sum(lp[2:3] * lp[3:4], axis=-1, keepdims=True)) + lambda_init)
    for a in range(nh):
        y = acc_ref[2 * a] / l_ref[2 * a] - lam * (acc_ref[2 * a + 1] / l_ref[2 * a + 1])
        ms = jnp.mean(y * y, axis=0, keepdims=True)
        y = (y * lax.rsqrt(ms + EPS)).T * g_ref[...]
        o_ref[:, a * hw:(a + 1) * hw] = (y * (1.0 - lambda_init)).astype(o_ref.dtype)


def _diff(qk, vt, lam_p, subln, batch, seq, n_heads, tq, nh, lambda_init):
    m = qk.shape[0]
    nq = seq // tq
    hw = 2 * HEAD_DIM
    gw = nh * hw
    ng = n_heads // nh
    return pl.pallas_call(
        functools.partial(_diff_kernel, tq=tq, nh=nh, lambda_init=lambda_init),
        grid=(batch, ng, nq),
        in_specs=[
            pl.BlockSpec((tq, gw), lambda b, g, i: (b * nq + i, g)),
            pl.BlockSpec((seq, gw), lambda b, g, i: (b, ng + g)),
            pl.BlockSpec((nq, gw, tq), lambda b, g, i: (b, g, 0)),
            pl.BlockSpec((4, HEAD_DIM), lambda b, g, i: (0, 0)),
            pl.BlockSpec((1, hw), lambda b, g, i: (0, 0)),
        ],
        out_specs=pl.BlockSpec((tq, gw), lambda b, g, i: (b * nq + i, g)),
        out_shape=jax.ShapeDtypeStruct((m, n_heads * hw), BF16),
        scratch_shapes=_flash_scratch(2 * nh, tq, hw),
        compiler_params=_params("arbitrary", "arbitrary", "arbitrary"),
        name="diff_attention",
    )(qk, qk, vt, lam_p, subln.reshape(1, hw))


def _merge_kernel(ya_ref, yb_ref, wa_ref, wb_ref, sa_ref, sb_ref, o_ref):
    pa = jnp.dot(ya_ref[...], wa_ref[...], preferred_element_type=F32)
    pb = jnp.dot(yb_ref[...], wb_ref[...], preferred_element_type=F32)
    o_ref[...] = (sa_ref[...].astype(F32) * pa + sb_ref[...].astype(F32) * pb).astype(o_ref.dtype)


def _merge(ya, yb, wa, wb, gates, bm, bn):
    m = ya.shape[0]
    d = wa.shape[1]
    nb = d // bn
    return pl.pallas_call(
        _merge_kernel,
        grid=(m // bm, nb),
        in_specs=[
            pl.BlockSpec((bm, ya.shape[1]), lambda i, j: (i, 0)),
            pl.BlockSpec((bm, yb.shape[1]), lambda i, j: (i, 0)),
            pl.BlockSpec((wa.shape[0], bn), lambda i, j: (0, j)),
            pl.BlockSpec((wb.shape[0], bn), lambda i, j: (0, j)),
            pl.BlockSpec((bm, bn), lambda i, j: (i, j)),
            pl.BlockSpec((bm, bn), lambda i, j: (i, nb + j)),
        ],
        out_specs=pl.BlockSpec((bm, bn), lambda i, j: (i, j)),
        out_shape=jax.ShapeDtypeStruct((m, d), BF16),
        compiler_params=_params("arbitrary", "arbitrary"),
        name="gated_merge",
    )(ya, yb, wa, wb, gates, gates)


def _ffn_width(f):
    return _round_up(f, 1024) if f >= 1024 else f


def _ffn_in_cast(w_in, layer):
    _, d, f2 = w_in.shape
    f = f2 // 2
    return _StripCast(w_in, layer, 0, d, d, ((0, f, _ffn_width(f)), (f, f, _ffn_width(f))))


def _ffn(x2, gain, norm_mods, w_in, pre_cast, w_out, layer, gate_mod3_of, gate_j, seq, next_casts,
         jobs=()):
    m, d = x2.shape
    f = w_out.shape[1]
    fp = _ffn_width(f)
    bm = _tile(seq, 1024)
    h = _norm_mod(x2, gain, *norm_mods, seq, BF16)
    wo_cast = _StripCast(w_out, layer, 0, f, fp, ((0, d, d),))
    if pre_cast is None:
        a0, wg, wu = _ffn_in_first(h, w_in, layer, bm)
    else:
        a0, (wg, wu) = None, pre_cast
    a, ((wo,), *job_outs) = _ffn_in(h, wg, wu, bm, _tile(fp, 512), (wo_cast,), a0, jobs)
    bk = fp // 4 if (fp // 4) % LANE == 0 else fp
    x2, cast_outs = _res_matmul(a, wo, x2, gate_mod3_of(job_outs), gate_j, 0.5, seq, bm,
                                _tile(d, 1024), bk, "ffn_out", casts=next_casts)
    return x2, cast_outs, job_outs


def kernel(x, c, positions, ada_w, ada_b, norm_ffn1, ffn1_w_in, ffn1_w_out, norm_mix, w_in, b_forget, b_gate, diff_lambda, diff_subln, w_o_fox, w_o_diff, w_out, norm_ffn2, ffn2_w_in, ffn2_w_out, final_ada_w, final_ada_b, norm_final):
    batch, seq, d = x.shape
    depth = ada_w.shape[0]
    m = batch * seq
    n_fox = b_forget.shape[1]
    fox_w = n_fox * HEAD_DIM
    diff_w = w_o_diff.shape[1]
    n_diff = diff_w // (2 * HEAD_DIM)
    o_ff = 3 * fox_w
    o_dq = o_ff + n_fox
    n_rest = 3 * diff_w + 2 * d
    assert w_in.shape[2] == o_dq + n_rest
    assert n_fox <= LANE and seq % CHUNK == 0

    c_pad = jnp.pad(c, ((0, -batch % 8), (0, 0)))
    x2 = x.reshape(m, d)
    w_in_t = jnp.swapaxes(w_in, 1, 2)

    inv_freq = ROPE_THETA ** (-jnp.arange(0, ROPE_DIM, 2, dtype=F32) / ROPE_DIM)
    ang = positions.astype(F32).reshape(m, 1) * inv_freq
    cos, sin = jnp.cos(ang), jnp.sin(ang)
    cosf = jnp.concatenate([cos, cos, jnp.ones((m, HEAD_DIM - ROPE_DIM), F32)], axis=-1)
    sinf = jnp.concatenate([-sin, sin, jnp.zeros((m, HEAD_DIM - ROPE_DIM), F32)], axis=-1)

    bm = _tile(seq, 1024)
    tq = _tile(seq, 512)
    bn_p = _tile(math.gcd(fox_w, diff_w), 1024)
    for l in range(depth):
        lambda_init = 0.8 - 0.6 * math.exp(-0.3 * l)
        ada_b3 = ada_b.reshape(depth, 1, N_MOD * d)
        mod_head = _ada(_AdaJob(c_pad, ada_w, ada_b3, l, 0, 2 * d))
        mod_rest = _AdaJob(c_pad, ada_w, ada_b3, l, 2 * d, (N_MOD - 2) * d)

        def full_mod3(job_outs):
            mod = jnp.concatenate([mod_head, job_outs[0][0]], axis=1)[:batch]
            return mod.reshape(batch * N_MOD, 1, d)

        mixer_casts = (_StripCast(w_in_t, l, 0, o_ff, o_ff, ((0, d, d),)),
                       _StripCast(w_in_t, l, o_dq, n_rest, n_rest, ((0, d, d),)))
        x2, ((w_front,), (w_rest,)), job_outs = _ffn(
            x2, norm_ffn1[l], (mod_head[:batch].reshape(batch * 2, 1, d), 2, 0, 1), ffn1_w_in, None,
            ffn1_w_out, l, full_mod3, 2, seq, mixer_casts, jobs=(mod_rest,))
        mod3 = full_mod3(job_outs)

        h = _norm_mod(x2, norm_mix[l], mod3, N_MOD, 3, 4, seq, BF16)
        w_ff_t = jnp.pad(w_in_t[l, o_ff:o_dq, :].astype(BF16), ((0, LANE - n_fox), (0, 0)))
        b_ff = jnp.pad(b_forget[l].astype(F32), (0, LANE - n_fox)).reshape(1, LANE)

        qk_f, ((wa,),) = _proj_plain(h, w_front, 0, 2 * fox_w, fox_w // bn_p, bm, bn_p,
                                     (_whole(w_o_fox, l),))
        qk_d, ((wb,),) = _proj_rope(h, w_rest, 0, 2 * diff_w, diff_w // bn_p, cosf, sinf, bm, bn_p,
                                    (_whole(w_o_diff, l),))
        gates, ((wg, wu),) = _proj_gate(h, w_rest, 3 * diff_w, 2 * d, b_gate[l], bm, bn_p,
                                        (_ffn_in_cast(ffn2_w_in, l),))
        vt_f, ((wo,),) = _vt_proj(h, w_front, 2 * fox_w, fox_w, bm, bn_p, tq, (_whole(w_out, l),))
        vt_d, _ = _vt_proj(h, w_rest, 2 * diff_w, diff_w, bm, bn_p, tq)
        kx = _forget(h, w_ff_t, b_ff, n_fox, batch, seq)

        ya = _fox(qk_f, kx, vt_f, batch, seq, n_fox, tq, math.gcd(n_fox, FOX_HEADS_PER_STEP))
        yb = _diff(qk_d, vt_d, diff_lambda[l].astype(F32), diff_subln[l], batch, seq, n_diff,
                   tq, math.gcd(n_diff, DIFF_HEADS_PER_STEP), lambda_init)
        merged = _merge(ya, yb, wa, wb, gates, bm, _tile(d, 1024))
        x2, _ = _res_matmul(merged, wo, x2, mod3, 5, 1.0, seq, bm, _tile(d, 1024), d, "mixer_out")

        last = l == depth - 1
        final_job = _AdaJob(c_pad, final_ada_w[None], final_ada_b.reshape(1, 1, 2 * d), 0, 0, 2 * d)
        x2, _, final_outs = _ffn(x2, norm_ffn2[l], (mod3, N_MOD, 6, 7), ffn2_w_in, (wg, wu), ffn2_w_out,
                                 l, lambda _, mod3=mod3: mod3, 8, seq, (),
                                 jobs=(final_job,) if last else ())

    fmod = final_outs[0][0][:batch]
    out = _norm_mod(x2, norm_final, fmod.reshape(batch * 2, 1, d), 2, 0, 1, seq, F32)
    return out.reshape(batch, seq, d)
```

```python
import functools
import math
from typing import NamedTuple

import numpy as np
import jax
import jax.numpy as jnp
from jax import lax
from jax.experimental import pallas as pl
from jax.experimental.pallas import tpu as pltpu

HEAD_DIM = 128
CHUNK = 64
ROPE_THETA = 500000.0
ROPE_DIM = HEAD_DIM // 4
ROPE_HALF = ROPE_DIM // 2
EPS = 1e-6
N_MOD = 9
LOG2E = 1.4426950408889634
MASKED = -1e30
LANE = 128
VMEM_LIMIT = 56 * 1024 * 1024
BF16_ROWS = 16
CAST_BLOCK_ELEMS = 2 * 1024 * 1024
HOSTED_STRIP_ELEMS = 1024 * 1024
ADA_HOSTED_COLS = 256
N_SPLIT = 3
FORGET_SUBTILE = 512
FOX_HEADS_PER_STEP = 4
DIFF_HEADS_PER_STEP = 2

F32 = jnp.float32
BF16 = jnp.bfloat16


def _params(*sem):
    return pltpu.CompilerParams(dimension_semantics=sem, vmem_limit_bytes=VMEM_LIMIT)


def _tile(n, pref):
    t = min(n, pref)
    while n % t:
        t //= 2
    return t


def _round_up(n, m):
    return (n + m - 1) // m * m


def _sigmoid(x):
    return 0.5 * jnp.tanh(0.5 * x) + 0.5


def _nt_dot(a, b):
    return lax.dot_general(a, b, (((1,), (1,)), ((), ())), preferred_element_type=F32)


class _StripCast(NamedTuple):
    src: jax.Array
    layer: int
    row0: int
    rows: int
    rows_out: int
    pieces: tuple


def _whole(src, layer):
    _, r, c = src.shape
    return _StripCast(src, layer, 0, r, r, ((0, c, c),))


def _strip_rows(cast, lo, hi):
    g = math.gcd(math.gcd(cast.row0, cast.rows), cast.rows_out)
    for rs in range(_round_up(max(lo, 1), BF16_ROWS), min(hi, g) + 1, BF16_ROWS):
        if g % rs == 0:
            return rs
    return None


def _linear_step(grid, ids):
    t = ids[0]
    for g, i in zip(grid[1:], ids[1:]):
        t = t * g + i
    return t


def _strip_specs(cast, rs, grid):
    first, n_valid, n_strips = cast.row0 // rs, cast.rows // rs, cast.rows_out // rs
    cols = cast.src.shape[2]
    src_spec = pl.BlockSpec(
        (None, rs, cols),
        lambda *ids: (cast.layer, first + jnp.minimum(_linear_step(grid, ids), n_valid - 1), 0))
    out_specs = [pl.BlockSpec((rs, nco), lambda *ids: (jnp.minimum(_linear_step(grid, ids), n_strips - 1), 0))
                 for _, _, nco in cast.pieces]
    out_shapes = [jax.ShapeDtypeStruct((cast.rows_out, nco), BF16) for _, _, nco in cast.pieces]
    return src_spec, out_specs, out_shapes


def _strip_cast_body(cast, rs, grid, src_ref, dst_refs):
    t = _linear_step(grid, [pl.program_id(a) for a in range(len(grid))])
    strip = src_ref[...]
    for (c0, n, nco), dst in zip(cast.pieces, dst_refs):
        piece = strip[:, c0:c0 + n].astype(dst.dtype)
        if cast.rows < cast.rows_out:
            piece = jnp.where(t < cast.rows // rs, piece, jnp.zeros_like(piece))
        if nco > n:
            dst[:, :n] = piece
            dst[:, n:] = jnp.zeros((rs, nco - n), dst.dtype)
        else:
            dst[...] = piece


def _run_cast(cast):
    rs = _strip_rows(cast, 1, max(CAST_BLOCK_ELEMS // cast.src.shape[2], BF16_ROWS))
    rs = rs or _strip_rows(cast, 1, cast.rows_out)
    if rs is None:
        w = cast.src[cast.layer, cast.row0:cast.row0 + cast.rows]
        return tuple(jnp.pad(w[:, c0:c0 + n].astype(BF16), ((0, cast.rows_out - cast.rows), (0, nco - n)))
                     for c0, n, nco in cast.pieces)
    grid = (cast.rows_out // rs,)
    src_spec, out_specs, out_shapes = _strip_specs(cast, rs, grid)

    def body(src_ref, *dst_refs):
        _strip_cast_body(cast, rs, grid, src_ref, dst_refs)

    return tuple(pl.pallas_call(
        body, grid=grid, in_specs=[src_spec], out_specs=out_specs, out_shape=out_shapes,
        compiler_params=_params("arbitrary"), name="weight_cast",
    )(cast.src))


def _hosted_call(body, grid, in_specs, out_spec, out_shape, inputs, name, scratch=(), casts=(),
                 jobs=()):
    steps = math.prod(grid)
    hosted, strip_rows = [], []
    for cast in casts:
        rs = _strip_rows(cast, pl.cdiv(cast.rows_out, steps), cast.rows_out)
        if rs is not None and rs * cast.src.shape[2] <= HOSTED_STRIP_ELEMS:
            hosted.append(cast)
            strip_rows.append(rs)
    specs = [_strip_specs(cast, rs, grid) for cast, rs in zip(hosted, strip_rows)]
    riders = [job for job in jobs if _ada_blocks(job, ADA_HOSTED_COLS) in range(1, steps + 1)]
    rider_specs = [_ada_specs(job, ADA_HOSTED_COLS, lambda *ids: _linear_step(grid, ids))
                   for job in riders]
    n_in, n_src, n_job = len(in_specs), len(hosted), len(riders)

    def kernel_body(*refs):
        ins, src_refs = refs[:n_in], refs[n_in:n_in + n_src]
        job_ins = refs[n_in + n_src:n_in + n_src + 3 * n_job]
        o_ref = refs[n_in + n_src + 3 * n_job]
        pos = n_in + n_src + 3 * n_job + 1
        dst_groups = []
        for cast in hosted:
            dst_groups.append(refs[pos:pos + len(cast.pieces)])
            pos += len(cast.pieces)
        job_outs = refs[pos:pos + n_job]
        pos += n_job
        body(*ins, o_ref, *refs[pos:])
        for cast, rs, src_ref, dst_refs in zip(hosted, strip_rows, src_refs, dst_groups):
            _strip_cast_body(cast, rs, grid, src_ref, dst_refs)
        t = _linear_step(grid, [pl.program_id(a) for a in range(len(grid))])
        for k, job in enumerate(riders):
            @pl.when(t < _ada_blocks(job, ADA_HOSTED_COLS))
            def _(k=k):
                _ada_kernel(*job_ins[3 * k:3 * k + 3], job_outs[k])

    outs = pl.pallas_call(
        kernel_body, grid=grid,
        in_specs=list(in_specs) + [s[0] for s in specs] + [i for r in rider_specs for i in r[0]],
        out_specs=[out_spec] + [o for s in specs for o in s[1]] + [r[1] for r in rider_specs],
        out_shape=[out_shape] + [o for s in specs for o in s[2]] + [r[2] for r in rider_specs],
        scratch_shapes=list(scratch),
        compiler_params=_params(*(("arbitrary",) * len(grid))), name=name,
    )(*inputs, *[cast.src for cast in hosted], *[a for job in riders for a in (job.c_pad, job.w, job.b)])

    results, pos = [], 1
    for cast in casts:
        if any(cast is hc for hc in hosted):
            results.append(tuple(outs[pos:pos + len(cast.pieces)]))
            pos += len(cast.pieces)
        else:
            results.append(_run_cast(cast))
    pos = 1 + sum(len(cast.pieces) for cast in hosted)
    for job in jobs:
        if any(job is r for r in riders):
            results.append((outs[pos],))
            pos += 1
        else:
            results.append((_ada(job),))
    return outs[0], tuple(results)


class _AdaJob(NamedTuple):
    c_pad: jax.Array
    w: jax.Array
    b: jax.Array
    layer: int
    col0: int
    n: int


def _ada_blocks(job, tn):
    return job.n // tn if job.col0 % tn == 0 and job.n % tn == 0 else 0


def _ada_specs(job, tn, step_of):
    rows, d = job.c_pad.shape
    first, nb = job.col0 // tn, job.n // tn

    def col(*ids):
        return first + jnp.minimum(step_of(*ids), nb - 1)

    in_specs = [
        pl.BlockSpec((rows, d), lambda *ids: (0, 0)),
        pl.BlockSpec((None, d, tn), lambda *ids: (job.layer, 0, col(*ids))),
        pl.BlockSpec((None, 1, tn), lambda *ids: (job.layer, 0, col(*ids))),
    ]
    out_spec = pl.BlockSpec((rows, tn), lambda *ids: (0, col(*ids) - first))
    return in_specs, out_spec, jax.ShapeDtypeStruct((rows, job.n), F32)


def _ada_kernel(c_ref, w_ref, b_ref, o_ref):
    c = c_ref[...]
    sc = (c * jax.nn.sigmoid(c)).astype(BF16)
    o_ref[...] = jnp.dot(sc, w_ref[...].astype(BF16), preferred_element_type=F32) + b_ref[...]


def _ada(job):
    tn = _tile(math.gcd(job.col0, job.n) if job.col0 else job.n, 512)
    in_specs, out_spec, out_shape = _ada_specs(job, tn, lambda j: j)
    return pl.pallas_call(
        _ada_kernel, grid=(job.n // tn,), in_specs=in_specs, out_specs=out_spec,
        out_shape=out_shape, compiler_params=_params("arbitrary"), name="ada_matvec",
    )(job.c_pad, job.w, job.b)


def _norm_mod_kernel(x_ref, g_ref, sh_ref, sc_ref, o_ref):
    x = x_ref[...]
    ms = jnp.mean(x * x, axis=-1, keepdims=True)
    y = x * lax.rsqrt(ms + EPS) * g_ref[...]
    o_ref[...] = (y * (1.0 + sc_ref[0]) + sh_ref[0]).astype(o_ref.dtype)


def _norm_mod(x2, gain, mod3, n_mod, shift_j, scale_j, seq, out_dtype):
    m, d = x2.shape
    tm = _tile(seq, 512)
    per_b = seq // tm
    return pl.pallas_call(
        _norm_mod_kernel,
        grid=(m // tm,),
        in_specs=[
            pl.BlockSpec((tm, d), lambda i: (i, 0)),
            pl.BlockSpec((1, d), lambda i: (0, 0)),
            pl.BlockSpec((1, 1, d), lambda i: ((i // per_b) * n_mod + shift_j, 0, 0)),
            pl.BlockSpec((1, 1, d), lambda i: ((i // per_b) * n_mod + scale_j, 0, 0)),
        ],
        out_specs=pl.BlockSpec((tm, d), lambda i: (i, 0)),
        out_shape=jax.ShapeDtypeStruct((m, d), out_dtype),
        compiler_params=_params("arbitrary"),
        name="norm_mod",
    )(x2, gain.reshape(1, d), mod3, mod3)


def _ffn_in_kernel(h_ref, wg_ref, wu_ref, o_ref):
    h = h_ref[...]
    g = jnp.dot(h, wg_ref[...], preferred_element_type=F32)
    u = jnp.dot(h, wu_ref[...], preferred_element_type=F32)
    o_ref[...] = (g * _sigmoid(g) * u).astype(o_ref.dtype)


def _ffn_in_resume_kernel(h_ref, wg_ref, wu_ref, a0_ref, o_ref):
    @pl.when(pl.program_id(0) == 0)
    def _():
        o_ref[...] = a0_ref[...]

    @pl.when(pl.program_id(0) > 0)
    def _():
        _ffn_in_kernel(h_ref, wg_ref, wu_ref, o_ref)


def _ffn_in(h, wg, wu, bm, bn, casts, a0=None, jobs=()):
    m, d = h.shape
    fp = wg.shape[1]
    nj = fp // bn
    resume = a0 is not None
    first = 1 if resume and m > bm else 0
    in_specs = [
        pl.BlockSpec((bm, d), lambda i, j: (jnp.maximum(i, first), 0)),
        pl.BlockSpec((d, bn), lambda i, j: (0, j)),
        pl.BlockSpec((d, bn), lambda i, j: (0, j)),
    ]
    if resume:
        in_specs.append(pl.BlockSpec((bm, bn), lambda i, j: (0, jnp.where(i == 0, j, nj - 1))))
    return _hosted_call(
        _ffn_in_resume_kernel if resume else _ffn_in_kernel,
        (m // bm, nj), in_specs,
        pl.BlockSpec((bm, bn), lambda i, j: (i, j)),
        jax.ShapeDtypeStruct((m, fp), BF16),
        (h, wg, wu) + ((a0,) if resume else ()), "ffn_in", casts=casts, jobs=jobs)


def _ffn_in_first_kernel(h_ref, wg32_ref, wu32_ref, o_ref, wg_ref, wu_ref, *, n_valid):
    keep = pl.program_id(0) < n_valid
    wg = jnp.where(keep, wg32_ref[...], 0.0).astype(wg_ref.dtype)
    wu = jnp.where(keep, wu32_ref[...], 0.0).astype(wu_ref.dtype)
    wg_ref[...] = wg
    wu_ref[...] = wu
    h = h_ref[...]
    g = jnp.dot(h, wg, preferred_element_type=F32)
    u = jnp.dot(h, wu, preferred_element_type=F32)
    o_ref[...] = (g * _sigmoid(g) * u).astype(o_ref.dtype)


def _ffn_in_first(h, w_in, layer, bm):
    m, d = h.shape
    f = w_in.shape[2] // 2
    fp = _ffn_width(f)
    bn = _tile(math.gcd(f, fp), 256)
    nv = f // bn
    return pl.pallas_call(
        functools.partial(_ffn_in_first_kernel, n_valid=nv),
        grid=(fp // bn,),
        in_specs=[
            pl.BlockSpec((bm, d), lambda j: (0, 0)),
            pl.BlockSpec((None, d, bn), lambda j: (layer, 0, jnp.minimum(j, nv - 1))),
            pl.BlockSpec((None, d, bn), lambda j: (layer, 0, nv + jnp.minimum(j, nv - 1))),
        ],
        out_specs=[
            pl.BlockSpec((bm, bn), lambda j: (0, j)),
            pl.BlockSpec((d, bn), lambda j: (0, j)),
            pl.BlockSpec((d, bn), lambda j: (0, j)),
        ],
        out_shape=[jax.ShapeDtypeStruct((bm, fp), BF16),
                   jax.ShapeDtypeStruct((d, fp), BF16), jax.ShapeDtypeStruct((d, fp), BF16)],
        compiler_params=_params("arbitrary"),
        name="ffn_in_first",
    )(h, w_in, w_in)


def _res_matmul_kernel(a_ref, w_ref, res_ref, gate_ref, o_ref, acc_ref, *, coef, nk):
    k = pl.program_id(2)

    def partial_product():
        return jnp.dot(a_ref[...], w_ref[...], preferred_element_type=F32)

    def finish(total):
        o_ref[...] = res_ref[...] + (coef * gate_ref[0]) * total

    if nk == 1:
        finish(partial_product())
        return

    @pl.when(k == 0)
    def _():
        acc_ref[...] = partial_product()

    @pl.when((k > 0) & (k < nk - 1))
    def _():
        acc_ref[...] += partial_product()

    @pl.when(k == nk - 1)
    def _():
        finish(acc_ref[...] + partial_product())


def _res_matmul(a, w, res, mod3, gate_j, coef, seq, bm, bn, bk, name, casts=()):
    m, kdim = a.shape
    n = w.shape[1]
    per_b = seq // bm
    return _hosted_call(
        functools.partial(_res_matmul_kernel, coef=coef, nk=kdim // bk),
        (m // bm, n // bn, kdim // bk),
        [
            pl.BlockSpec((bm, bk), lambda i, j, k: (i, k)),
            pl.BlockSpec((bk, bn), lambda i, j, k: (k, j)),
            pl.BlockSpec((bm, bn), lambda i, j, k: (i, j)),
            pl.BlockSpec((1, 1, bn), lambda i, j, k: ((i // per_b) * N_MOD + gate_j, 0, j)),
        ],
        pl.BlockSpec((bm, bn), lambda i, j, k: (i, j)),
        jax.ShapeDtypeStruct((m, n), F32),
        (a, w, res, mod3), name, scratch=[pltpu.VMEM((bm, bn), F32)], casts=casts)


QSCALE = HEAD_DIM ** -0.5 * LOG2E


def _proj_call(body, h, wt, row0, n, bm, bn, extra_in, extra_specs, name, casts):
    m, d = h.shape
    b0 = row0 // bn
    return _hosted_call(
        body,
        (m // bm, n // bn),
        [
            pl.BlockSpec((bm, d), lambda i, j: (i, 0)),
            pl.BlockSpec((bn, d), lambda i, j: (b0 + j, 0)),
        ] + extra_specs,
        pl.BlockSpec((bm, bn), lambda i, j: (i, j)),
        jax.ShapeDtypeStruct((m, n), BF16),
        (h, wt, *extra_in), name, casts=casts)


def _block_scale(n_scaled):
    return jnp.where(pl.program_id(1) < n_scaled, QSCALE, 1.0).astype(F32)


def _proj_plain_kernel(h_ref, wt_ref, o_ref, *, n_scaled):
    acc = _nt_dot(h_ref[...], wt_ref[...])
    o_ref[...] = (acc * _block_scale(n_scaled)).astype(o_ref.dtype)


def _proj_rope_kernel(h_ref, wt_ref, cos_ref, sin_ref, o_ref, *, n_scaled):
    acc = _nt_dot(h_ref[...], wt_ref[...])
    cosf = cos_ref[...] * _block_scale(n_scaled)
    sinf = sin_ref[...] * _block_scale(n_scaled)
    lane = lax.broadcasted_iota(jnp.int32, cosf.shape, 1)
    for c in range(acc.shape[1] // HEAD_DIM):
        cols = slice(c * HEAD_DIM, (c + 1) * HEAD_DIM)
        t = acc[:, cols]
        partner = jnp.where(lane < ROPE_HALF,
                            pltpu.roll(t, HEAD_DIM - ROPE_HALF, 1),
                            pltpu.roll(t, ROPE_HALF, 1))
        o_ref[:, cols] = (t * cosf + partner * sinf).astype(o_ref.dtype)


def _proj_gate_kernel(h_ref, wt_ref, b_ref, o_ref):
    acc = _nt_dot(h_ref[...], wt_ref[...])
    o_ref[...] = _sigmoid(acc + b_ref[...]).astype(o_ref.dtype)


def _proj_plain(h, wt, row0, n, n_scaled, bm, bn, casts):
    return _proj_call(functools.partial(_proj_plain_kernel, n_scaled=n_scaled),
                      h, wt, row0, n, bm, bn, [], [], "proj_plain", casts)


def _proj_rope(h, wt, row0, n, n_scaled, cosf, sinf, bm, bn, casts):
    table = pl.BlockSpec((bm, HEAD_DIM), lambda i, j: (i, 0))
    return _proj_call(functools.partial(_proj_rope_kernel, n_scaled=n_scaled),
                      h, wt, row0, n, bm, bn, [cosf, sinf], [table, table], "proj_rope", casts)


def _proj_gate(h, wt, row0, n, bias, bm, bn, casts):
    return _proj_call(_proj_gate_kernel, h, wt, row0, n, bm, bn, [bias.reshape(1, n)],
                      [pl.BlockSpec((1, bn), lambda i, j: (0, j))], "proj_gate", casts)


def _vt_proj_kernel(wt_ref, h_ref, o_ref, *, tk):
    r = _nt_dot(wt_ref[...], h_ref[...])
    for t in range(o_ref.shape[0]):
        o_ref[t] = r[:, t * tk:(t + 1) * tk].astype(o_ref.dtype)


def _vt_proj(h, wt, row0, n, bm, bn, tk, casts=()):
    m, d = h.shape
    b0 = row0 // bn
    return _hosted_call(
        functools.partial(_vt_proj_kernel, tk=tk),
        (m // bm, n // bn),
        [
            pl.BlockSpec((bn, d), lambda i, j: (b0 + j, 0)),
            pl.BlockSpec((bm, d), lambda i, j: (i, 0)),
        ],
        pl.BlockSpec((bm // tk, bn, tk), lambda i, j: (i, j, 0)),
        jax.ShapeDtypeStruct((m // tk, n, tk), BF16),
        (wt, h), "proj_vt", casts=casts)


def _forget_kernel(h_ref, w_ref, b_ref, e_ref, o_ref, carry_ref):
    @pl.when(pl.program_id(1) == 0)
    def _():
        carry_ref[...] = jnp.zeros_like(carry_ref)

    ts = FORGET_SUBTILE if h_ref.shape[0] % FORGET_SUBTILE == 0 else h_ref.shape[0]
    row = lax.broadcasted_iota(jnp.int32, (ts, ts), 0)
    col = lax.broadcasted_iota(jnp.int32, (ts, ts), 1)
    tri = (row >= col).astype(F32)
    subs = [slice(s * ts, (s + 1) * ts) for s in range(h_ref.shape[0] // ts)]
    ffs = [_nt_dot(h_ref[rows, :], w_ref[...]) + b_ref[...] for rows in subs]
    logfs = [jnp.minimum(ff, 0.0) - jnp.log1p(jnp.exp(-jnp.abs(ff))) for ff in ffs]
    sums = [jnp.dot(tri, logf, preferred_element_type=F32, precision=lax.Precision.HIGHEST)
            for logf in logfs]
    carry = carry_ref[...]
    stacked = []
    for within in sums:
        cum = within + carry
        carry = cum[ts - 1:ts, :]
        rest = cum * (-LOG2E)
        pieces = []
        for _ in range(N_SPLIT):
            p = rest.astype(BF16)
            pieces.append(p)
            rest = rest - p.astype(F32)
        stacked.append(jnp.concatenate(pieces, axis=1))
    carry_ref[...] = carry
    for rows, pieces in zip(subs, stacked):
        spread = jnp.dot(pieces, e_ref[...], preferred_element_type=F32)
        o_ref[rows, :] = spread.astype(o_ref.dtype)


def _forget(h, w_ff_t, b_ff, n_heads, batch, seq):
    m, d = h.shape
    tm = _tile(seq, 2 * FORGET_SUBTILE)
    per_b = seq // tm
    route = np.zeros((N_SPLIT * LANE, n_heads * HEAD_DIM), np.float32)
    for p in range(N_SPLIT):
        for hh in range(n_heads):
            route[p * LANE + hh, hh * HEAD_DIM + p] = 1.0
    return pl.pallas_call(
        _forget_kernel,
        grid=(batch, per_b),
        in_specs=[
            pl.BlockSpec((tm, d), lambda b, i: (b * per_b + i, 0)),
            pl.BlockSpec((LANE, d), lambda b, i: (0, 0)),
            pl.BlockSpec((1, LANE), lambda b, i: (0, 0)),
            pl.BlockSpec(route.shape, lambda b, i: (0, 0)),
        ],
        out_specs=pl.BlockSpec((tm, n_heads * HEAD_DIM), lambda b, i: (b * per_b + i, 0)),
        out_shape=jax.ShapeDtypeStruct((m, n_heads * HEAD_DIM), BF16),
        scratch_shapes=[pltpu.VMEM((1, LANE), F32)],
        compiler_params=_params("arbitrary", "arbitrary"),
        name="forget_cumsum",
    )(h, w_ff_t, b_ff, jnp.asarray(route, BF16))


def _online_softmax_step(st, st_max, vt, m, l, acc):
    m_new = jnp.maximum(m, st_max)
    alpha = jnp.exp2(m - m_new)
    p = jnp.exp2(st - m_new)
    l_new = alpha * l + jnp.sum(p, axis=0, keepdims=True)
    acc_new = alpha * acc + jnp.dot(vt, p.astype(vt.dtype), preferred_element_type=F32)
    return m_new, l_new, acc_new


def _flash_scratch(n_chains, tq, dv):
    return ([pltpu.VMEM((2, tq, tq), F32) for _ in range(n_chains)]
            + [pltpu.VMEM((2 * n_chains, 1, tq), F32)]
            + [pltpu.VMEM((n_chains, 1, tq), F32), pltpu.VMEM((n_chains, 1, tq), F32),
               pltpu.VMEM((n_chains, dv, tq), F32)])


def _flash_loop(qi, n_chains, qk_fn, vt_fn, visible_fn, st_refs, mx_ref, m_ref, l_ref, acc_ref):
    def produce(kj, slot, diagonal):
        for c in range(n_chains):
            st = qk_fn(c, kj)
            if diagonal:
                st = jnp.where(visible_fn(), st, MASKED)
            st_refs[c][slot] = st
            mx_ref[2 * c + slot] = jnp.max(st, axis=0, keepdims=True)

    def consume(c, kj, slot):
        m, l, acc = _online_softmax_step(st_refs[c][slot], mx_ref[2 * c + slot], vt_fn(c, kj),
                                         m_ref[c], l_ref[c], acc_ref[c])
        m_ref[c] = m
        l_ref[c] = l
        acc_ref[c] = acc

    def step(kj, slot, next_is_diagonal):
        for c in range(n_chains):
            st = qk_fn(c, kj + 1)
            if next_is_diagonal:
                st = jnp.where(visible_fn(), st, MASKED)
            st_refs[c][1 - slot] = st
            mx_ref[2 * c + 1 - slot] = jnp.max(st, axis=0, keepdims=True)
            consume(c, kj, slot)

    def run_if(cond, fn):
        def body(_, carry):
            fn()
            return carry
        lax.fori_loop(0, cond.astype(jnp.int32), body, 0)

    tq = m_ref.shape[-1]
    m_ref[...] = jnp.full(m_ref.shape, MASKED, F32)
    l_ref[...] = jnp.zeros(l_ref.shape, F32)
    acc_ref[...] = jnp.zeros(acc_ref.shape, F32)

    run_if(qi == 0, lambda: produce(0, 0, True))
    run_if(qi > 0, lambda: produce(0, 0, False))

    n_pairs = jnp.maximum(qi - 1, 0) // 2
    kj0 = 2 * n_pairs

    def pair(i, carry):
        step(2 * i, 0, False)
        step(2 * i + 1, 1, False)
        return carry

    lax.fori_loop(0, n_pairs, pair, 0)
    two_left = (qi > 0) & (qi - kj0 == 2)
    one_left = (qi > 0) & (qi - kj0 == 1)
    run_if(two_left, lambda: step(kj0, 0, False))
    run_if(two_left, lambda: step(kj0 + 1, 1, True))
    run_if(one_left, lambda: step(kj0, 0, True))
    run_if(one_left, lambda: [consume(c, qi, 1) for c in range(n_chains)])
    run_if(jnp.logical_not(one_left), lambda: [consume(c, qi, 0) for c in range(n_chains)])


def _fox_kernel(q_ref, k_ref, kx_ref, vt_ref, o_ref, *scratch, tq, nh):
    qi = pl.program_id(2)
    st_refs, (mx_ref, m_ref, l_ref, acc_ref) = scratch[:nh], scratch[nh:]
    lane = lax.broadcasted_iota(jnp.int32, (tq, HEAD_DIM), 1)
    ones = (lane < N_SPLIT).astype(q_ref.dtype)

    def head(a):
        return slice(a * HEAD_DIM, (a + 1) * HEAD_DIM)

    def qk(a, kj):
        ks = pl.multiple_of(kj * tq, tq)
        q_aug = jnp.concatenate([q_ref[:, head(a)], ones], axis=1)
        k_aug = jnp.concatenate([k_ref[pl.ds(ks, tq), head(a)], kx_ref[pl.ds(ks, tq), head(a)]], axis=1)
        return _nt_dot(k_aug, q_aug)

    def visible():
        key = lax.broadcasted_iota(jnp.int32, (tq, tq), 0)
        qry = lax.broadcasted_iota(jnp.int32, (tq, tq), 1)
        return key <= qry

    _flash_loop(qi, nh, qk, lambda a, kj: vt_ref[kj, head(a), :], visible,
                st_refs, mx_ref, m_ref, l_ref, acc_ref)
    for a in range(nh):
        o_ref[:, head(a)] = (acc_ref[a] / l_ref[a]).T.astype(o_ref.dtype)


def _fox(qk, kx, vt, batch, seq, n_heads, tq, nh):
    m = qk.shape[0]
    nq = seq // tq
    gw = nh * HEAD_DIM
    ng = n_heads // nh
    return pl.pallas_call(
        functools.partial(_fox_kernel, tq=tq, nh=nh),
        grid=(batch, ng, nq),
        in_specs=[
            pl.BlockSpec((tq, gw), lambda b, g, i: (b * nq + i, g)),
            pl.BlockSpec((seq, gw), lambda b, g, i: (b, ng + g)),
            pl.BlockSpec((seq, gw), lambda b, g, i: (b, g)),
            pl.BlockSpec((nq, gw, tq), lambda b, g, i: (b, g, 0)),
        ],
        out_specs=pl.BlockSpec((tq, gw), lambda b, g, i: (b * nq + i, g)),
        out_shape=jax.ShapeDtypeStruct((m, n_heads * HEAD_DIM), BF16),
        scratch_shapes=_flash_scratch(nh, tq, HEAD_DIM),
        compiler_params=_params("arbitrary", "arbitrary", "arbitrary"),
        name="fox_attention",
    )(qk, qk, kx, vt)


def _diff_kernel(q_ref, k_ref, vt_ref, lam_ref, g_ref, o_ref, *scratch, tq, nh, lambda_init):
    qi = pl.program_id(2)
    n_maps = 2 * nh
    st_refs, (mx_ref, m_ref, l_ref, acc_ref) = scratch[:n_maps], scratch[n_maps:]
    hw = 2 * HEAD_DIM

    def cols(c):
        return slice(c * HEAD_DIM, (c + 1) * HEAD_DIM)

    def qk(c, kj):
        ks = pl.multiple_of(kj * tq, tq)
        return _nt_dot(k_ref[pl.ds(ks, tq), cols(c)], q_ref[:, cols(c)])

    def visible():
        key = lax.broadcasted_iota(jnp.int32, (tq, tq), 0) // CHUNK
        qry = lax.broadcasted_iota(jnp.int32, (tq, tq), 1) // CHUNK
        return key <= qry

    _flash_loop(qi, n_maps, qk, lambda c, kj: vt_ref[kj, (c // 2) * hw:(c // 2 + 1) * hw, :],
                visible, st_refs, mx_ref, m_ref, l_ref, acc_ref)

    lp = lam_ref[...]
    lam = (jnp.exp(jnp.sum(lp[0:1] * lp[1:2], axis=-1, keepdims=True))
           - jnp.exp(jnp.sum(lp[2:3] * lp[3:4], axis=-1, keepdims=True)) + lambda_init)
    for a in range(nh):
        y = acc_ref[2 * a] / l_ref[2 * a] - lam * (acc_ref[2 * a + 1] / l_ref[2 * a + 1])
        ms = jnp.mean(y * y, axis=0, keepdims=True)
        y = (y * lax.rsqrt(ms + EPS)).T * g_ref[...]
        o_ref[:, a * hw:(a + 1) * hw] = (y * (1.0 - lambda_init)).astype(o_ref.dtype)


def _diff(qk, vt, lam_p, subln, batch, seq, n_heads, tq, nh, lambda_init):
    m = qk.shape[0]
    nq = seq // tq
    hw = 2 * HEAD_DIM
    gw = nh * hw
    ng = n_heads // nh
    return pl.pallas_call(
        functools.partial(_diff_kernel, tq=tq, nh=nh, lambda_init=lambda_init),
        grid=(batch, ng, nq),
        in_specs=[
            pl.BlockSpec((tq, gw), lambda b, g, i: (b * nq + i, g)),
            pl.BlockSpec((seq, gw), lambda b, g, i: (b, ng + g)),
            pl.BlockSpec((nq, gw, tq), lambda b, g, i: (b, g, 0)),
            pl.BlockSpec((4, HEAD_DIM), lambda b, g, i: (0, 0)),
            pl.BlockSpec((1, hw), lambda b, g, i: (0, 0)),
        ],
        out_specs=pl.BlockSpec((tq, gw), lambda b, g, i: (b * nq + i, g)),
        out_shape=jax.ShapeDtypeStruct((m, n_heads * hw), BF16),
        scratch_shapes=_flash_scratch(2 * nh, tq, hw),
        compiler_params=_params("arbitrary", "arbitrary", "arbitrary"),
        name="diff_attention",
    )(qk, qk, vt, lam_p, subln.reshape(1, hw))


def _merge_kernel(ya_ref, yb_ref, wa_ref, wb_ref, sa_ref, sb_ref, o_ref):
    pa = jnp.dot(ya_ref[...], wa_ref[...], preferred_element_type=F32)
    pb = jnp.dot(yb_ref[...], wb_ref[...], preferred_element_type=F32)
    o_ref[...] = (sa_ref[...].astype(F32) * pa + sb_ref[...].astype(F32) * pb).astype(o_ref.dtype)


def _merge(ya, yb, wa, wb, gates, bm, bn):
    m = ya.shape[0]
    d = wa.shape[1]
    nb = d // bn
    return pl.pallas_call(
        _merge_kernel,
        grid=(m // bm, nb),
        in_specs=[
            pl.BlockSpec((bm, ya.shape[1]), lambda i, j: (i, 0)),
            pl.BlockSpec((bm, yb.shape[1]), lambda i, j: (i, 0)),
            pl.BlockSpec((wa.shape[0], bn), lambda i, j: (0, j)),
            pl.BlockSpec((wb.shape[0], bn), lambda i, j: (0, j)),
            pl.BlockSpec((bm, bn), lambda i, j: (i, j)),
            pl.BlockSpec((bm, bn), lambda i, j: (i, nb + j)),
        ],
        out_specs=pl.BlockSpec((bm, bn), lambda i, j: (i, j)),
        out_shape=jax.ShapeDtypeStruct((m, d), BF16),
        compiler_params=_params("arbitrary", "arbitrary"),
        name="gated_merge",
    )(ya, yb, wa, wb, gates, gates)


def _ffn_width(f):
    return _round_up(f, 1024) if f >= 1024 else f


def _ffn_in_cast(w_in, layer):
    _, d, f2 = w_in.shape
    f = f2 // 2
    return _StripCast(w_in, layer, 0, d, d, ((0, f, _ffn_width(f)), (f, f, _ffn_width(f))))


def _ffn(x2, gain, norm_mods, w_in, pre_cast, w_out, layer, gate_mod3_of, gate_j, seq, next_casts,
         jobs=()):
    m, d = x2.shape
    f = w_out.shape[1]
    fp = _ffn_width(f)
    bm = _tile(seq, 1024)
    h = _norm_mod(x2, gain, *norm_mods, seq, BF16)
    wo_cast = _StripCast(w_out, layer, 0, f, fp, ((0, d, d),))
    if pre_cast is None:
        a0, wg, wu = _ffn_in_first(h, w_in, layer, bm)
    else:
        a0, (wg, wu) = None, pre_cast
    a, ((wo,), *job_outs) = _ffn_in(h, wg, wu, bm, _tile(fp, 512), (wo_cast,), a0, jobs)
    bk = fp // 4 if (fp // 4) % LANE == 0 else fp
    x2, cast_outs = _res_matmul(a, wo, x2, gate_mod3_of(job_outs), gate_j, 0.5, seq, bm,
                                _tile(d, 1024), bk, "ffn_out", casts=next_casts)
    return x2, cast_outs, job_outs


def kernel(x, c, positions, ada_w, ada_b, norm_ffn1, ffn1_w_in, ffn1_w_out, norm_mix, w_in, b_forget, b_gate, diff_lambda, diff_subln, w_o_fox, w_o_diff, w_out, norm_ffn2, ffn2_w_in, ffn2_w_out, final_ada_w, final_ada_b, norm_final):
    batch, seq, d = x.shape
    depth = ada_w.shape[0]
    m = batch * seq
    n_fox = b_forget.shape[1]
    fox_w = n_fox * HEAD_DIM
    diff_w = w_o_diff.shape[1]
    n_diff = diff_w // (2 * HEAD_DIM)
    o_ff = 3 * fox_w
    o_dq = o_ff + n_fox
    n_rest = 3 * diff_w + 2 * d
    assert w_in.shape[2] == o_dq + n_rest
    assert n_fox <= LANE and seq % CHUNK == 0

    c_pad = jnp.pad(c, ((0, -batch % 8), (0, 0)))
    x2 = x.reshape(m, d)
    w_in_t = jnp.swapaxes(w_in, 1, 2)

    inv_freq = ROPE_THETA ** (-jnp.arange(0, ROPE_DIM, 2, dtype=F32) / ROPE_DIM)
    ang = positions.astype(F32).reshape(m, 1) * inv_freq
    cos, sin = jnp.cos(ang), jnp.sin(ang)
    cosf = jnp.concatenate([cos, cos, jnp.ones((m, HEAD_DIM - ROPE_DIM), F32)], axis=-1)
    sinf = jnp.concatenate([-sin, sin, jnp.zeros((m, HEAD_DIM - ROPE_DIM), F32)], axis=-1)

    bm = _tile(seq, 1024)
    tq = _tile(seq, 512)
    bn_p = _tile(math.gcd(fox_w, diff_w), 1024)
    for l in range(depth):
        lambda_init = 0.8 - 0.6 * math.exp(-0.3 * l)
        ada_b3 = ada_b.reshape(depth, 1, N_MOD * d)
        mod_head = _ada(_AdaJob(c_pad, ada_w, ada_b3, l, 0, 2 * d))
        mod_rest = _AdaJob(c_pad, ada_w, ada_b3, l, 2 * d, (N_MOD - 2) * d)

        def full_mod3(job_outs):
            mod = jnp.concatenate([mod_head, job_outs[0][0]], axis=1)[:batch]
            return mod.reshape(batch * N_MOD, 1, d)

        mixer_casts = (_StripCast(w_in_t, l, 0, o_ff, o_ff, ((0, d, d),)),
                       _StripCast(w_in_t, l, o_dq, n_rest, n_rest, ((0, d, d),)))
        x2, ((w_front,), (w_rest,)), job_outs = _ffn(
            x2, norm_ffn1[l], (mod_head[:batch].reshape(batch * 2, 1, d), 2, 0, 1), ffn1_w_in, None,
            ffn1_w_out, l, full_mod3, 2, seq, mixer_casts, jobs=(mod_rest,))
        mod3 = full_mod3(job_outs)

        h = _norm_mod(x2, norm_mix[l], mod3, N_MOD, 3, 4, seq, BF16)
        w_ff_t = jnp.pad(w_in_t[l, o_ff:o_dq, :].astype(BF16), ((0, LANE - n_fox), (0, 0)))
        b_ff = jnp.pad(b_forget[l].astype(F32), (0, LANE - n_fox)).reshape(1, LANE)

        qk_f, ((wa,),) = _proj_plain(h, w_front, 0, 2 * fox_w, fox_w // bn_p, bm, bn_p,
                                     (_whole(w_o_fox, l),))
        qk_d, ((wb,),) = _proj_rope(h, w_rest, 0, 2 * diff_w, diff_w // bn_p, cosf, sinf, bm, bn_p,
                                    (_whole(w_o_diff, l),))
        gates, ((wg, wu),) = _proj_gate(h, w_rest, 3 * diff_w, 2 * d, b_gate[l], bm, bn_p,
                                        (_ffn_in_cast(ffn2_w_in, l),))
        vt_f, ((wo,),) = _vt_proj(h, w_front, 2 * fox_w, fox_w, bm, bn_p, tq, (_whole(w_out, l),))
        vt_d, _ = _vt_proj(h, w_rest, 2 * diff_w, diff_w, bm, bn_p, tq)
        kx = _forget(h, w_ff_t, b_ff, n_fox, batch, seq)

        ya = _fox(qk_f, kx, vt_f, batch, seq, n_fox, tq, math.gcd(n_fox, FOX_HEADS_PER_STEP))
        yb = _diff(qk_d, vt_d, diff_lambda[l].astype(F32), diff_subln[l], batch, seq, n_diff,
                   tq, math.gcd(n_diff, DIFF_HEADS_PER_STEP), lambda_init)
        merged = _merge(ya, yb, wa, wb, gates, bm, _tile(d, 1024))
        x2, _ = _res_matmul(merged, wo, x2, mod3, 5, 1.0, seq, bm, _tile(d, 1024), d, "mixer_out")

        last = l == depth - 1
        final_job = _AdaJob(c_pad, final_ada_w[None], final_ada_b.reshape(1, 1, 2 * d), 0, 0, 2 * d)
        x2, _, final_outs = _ffn(x2, norm_ffn2[l], (mod3, N_MOD, 6, 7), ffn2_w_in, (wg, wu), ffn2_w_out,
                                 l, lambda _, mod3=mod3: mod3, 8, seq, (),
                                 jobs=(final_job,) if last else ())

    fmod = final_outs[0][0][:batch]
    out = _norm_mod(x2, norm_final, fmod.reshape(batch * 2, 1, d), 2, 0, 1, seq, F32)
    return out.reshape(batch, seq, d)
```

```python
import functools
import math
from typing import NamedTuple

import numpy as np
import jax
import jax.numpy as jnp
from jax import lax
from jax.experimental import pallas as pl
from jax.experimental.pallas import tpu as pltpu

HEAD_DIM = 128
CHUNK = 64
ROPE_THETA = 500000.0
ROPE_DIM = HEAD_DIM // 4
ROPE_HALF = ROPE_DIM // 2
EPS = 1e-6
N_MOD = 9
LOG2E = 1.4426950408889634
MASKED = -1e30
LANE = 128
VMEM_LIMIT = 56 * 1024 * 1024
BF16_ROWS = 16
CAST_BLOCK_ELEMS = 2 * 1024 * 1024
HOSTED_STRIP_ELEMS = 1024 * 1024
ADA_HOSTED_COLS = 256
N_SPLIT = 3
FORGET_SUBTILE = 512
FOX_HEADS_PER_STEP = 4
DIFF_HEADS_PER_STEP = 2

F32 = jnp.float32
BF16 = jnp.bfloat16


def _params(*sem):
    return pltpu.CompilerParams(dimension_semantics=sem, vmem_limit_bytes=VMEM_LIMIT)


def _tile(n, pref):
    t = min(n, pref)
    while n % t:
        t //= 2
    return t


def _round_up(n, m):
    return (n + m - 1) // m * m


def _sigmoid(x):
    return 0.5 * jnp.tanh(0.5 * x) + 0.5


def _nt_dot(a, b):
    return lax.dot_general(a, b, (((1,), (1,)), ((), ())), preferred_element_type=F32)


class _StripCast(NamedTuple):
    src: jax.Array
    layer: int
    row0: int
    rows: int
    rows_out: int
    pieces: tuple


def _whole(src, layer):
    _, r, c = src.shape
    return _StripCast(src, layer, 0, r, r, ((0, c, c),))


def _strip_rows(cast, lo, hi):
    g = math.gcd(math.gcd(cast.row0, cast.rows), cast.rows_out)
    for rs in range(_round_up(max(lo, 1), BF16_ROWS), min(hi, g) + 1, BF16_ROWS):
        if g % rs == 0:
            return rs
    return None


def _linear_step(grid, ids):
    t = ids[0]
    for g, i in zip(grid[1:], ids[1:]):
        t = t * g + i
    return t


def _strip_specs(cast, rs, grid):
    first, n_valid, n_strips = cast.row0 // rs, cast.rows // rs, cast.rows_out // rs
    cols = cast.src.shape[2]
    src_spec = pl.BlockSpec(
        (None, rs, cols),
        lambda *ids: (cast.layer, first + jnp.minimum(_linear_step(grid, ids), n_valid - 1), 0))
    out_specs = [pl.BlockSpec((rs, nco), lambda *ids: (jnp.minimum(_linear_step(grid, ids), n_strips - 1), 0))
                 for _, _, nco in cast.pieces]
    out_shapes = [jax.ShapeDtypeStruct((cast.rows_out, nco), BF16) for _, _, nco in cast.pieces]
    return src_spec, out_specs, out_shapes


def _strip_cast_body(cast, rs, grid, src_ref, dst_refs):
    t = _linear_step(grid, [pl.program_id(a) for a in range(len(grid))])
    strip = src_ref[...]
    for (c0, n, nco), dst in zip(cast.pieces, dst_refs):
        piece = strip[:, c0:c0 + n].astype(dst.dtype)
        if cast.rows < cast.rows_out:
            piece = jnp.where(t < cast.rows // rs, piece, jnp.zeros_like(piece))
        if nco > n:
            dst[:, :n] = piece
            dst[:, n:] = jnp.zeros((rs, nco - n), dst.dtype)
        else:
            dst[...] = piece


def _run_cast(cast):
    rs = _strip_rows(cast, 1, max(CAST_BLOCK_ELEMS // cast.src.shape[2], BF16_ROWS))
    rs = rs or _strip_rows(cast, 1, cast.rows_out)
    if rs is None:
        w = cast.src[cast.layer, cast.row0:cast.row0 + cast.rows]
        return tuple(jnp.pad(w[:, c0:c0 + n].astype(BF16), ((0, cast.rows_out - cast.rows), (0, nco - n)))
                     for c0, n, nco in cast.pieces)
    grid = (cast.rows_out // rs,)
    src_spec, out_specs, out_shapes = _strip_specs(cast, rs, grid)

    def body(src_ref, *dst_refs):
        _strip_cast_body(cast, rs, grid, src_ref, dst_refs)

    return tuple(pl.pallas_call(
        body, grid=grid, in_specs=[src_spec], out_specs=out_specs, out_shape=out_shapes,
        compiler_params=_params("arbitrary"), name="weight_cast",
    )(cast.src))


def _hosted_call(body, grid, in_specs, out_spec, out_shape, inputs, name, scratch=(), casts=(),
                 jobs=()):
    steps = math.prod(grid)
    hosted, strip_rows = [], []
    for cast in casts:
        rs = _strip_rows(cast, pl.cdiv(cast.rows_out, steps), cast.rows_out)
        if rs is not None and rs * cast.src.shape[2] <= HOSTED_STRIP_ELEMS:
            hosted.append(cast)
            strip_rows.append(rs)
    specs = [_strip_specs(cast, rs, grid) for cast, rs in zip(hosted, strip_rows)]
    riders = [job for job in jobs if _ada_blocks(job, ADA_HOSTED_COLS) in range(1, steps + 1)]
    rider_specs = [_ada_specs(job, ADA_HOSTED_COLS, lambda *ids: _linear_step(grid, ids))
                   for job in riders]
    n_in, n_src, n_job = len(in_specs), len(hosted), len(riders)

    def kernel_body(*refs):
        ins, src_refs = refs[:n_in], refs[n_in:n_in + n_src]
        job_ins = refs[n_in + n_src:n_in + n_src + 3 * n_job]
        o_ref = refs[n_in + n_src + 3 * n_job]
        pos = n_in + n_src + 3 * n_job + 1
        dst_groups = []
        for cast in hosted:
            dst_groups.append(refs[pos:pos + len(cast.pieces)])
            pos += len(cast.pieces)
        job_outs = refs[pos:pos + n_job]
        pos += n_job
        body(*ins, o_ref, *refs[pos:])
        for cast, rs, src_ref, dst_refs in zip(hosted, strip_rows, src_refs, dst_groups):
            _strip_cast_body(cast, rs, grid, src_ref, dst_refs)
        t = _linear_step(grid, [pl.program_id(a) for a in range(len(grid))])
        for k, job in enumerate(riders):
            @pl.when(t < _ada_blocks(job, ADA_HOSTED_COLS))
            def _(k=k):
                _ada_kernel(*job_ins[3 * k:3 * k + 3], job_outs[k])

    outs = pl.pallas_call(
        kernel_body, grid=grid,
        in_specs=list(in_specs) + [s[0] for s in specs] + [i for r in rider_specs for i in r[0]],
        out_specs=[out_spec] + [o for s in specs for o in s[1]] + [r[1] for r in rider_specs],
        out_shape=[out_shape] + [o for s in specs for o in s[2]] + [r[2] for r in rider_specs],
        scratch_shapes=list(scratch),
        compiler_params=_params(*(("arbitrary",) * len(grid))), name=name,
    )(*inputs, *[cast.src for cast in hosted], *[a for job in riders for a in (job.c_pad, job.w, job.b)])

    results, pos = [], 1
    for cast in casts:
        if any(cast is hc for hc in hosted):
            results.append(tuple(outs[pos:pos + len(cast.pieces)]))
            pos += len(cast.pieces)
        else:
            results.append(_run_cast(cast))
    pos = 1 + sum(len(cast.pieces) for cast in hosted)
    for job in jobs:
        if any(job is r for r in riders):
            results.append((outs[pos],))
            pos += 1
        else:
            results.append((_ada(job),))
    return outs[0], tuple(results)


class _AdaJob(NamedTuple):
    c_pad: jax.Array
    w: jax.Array
    b: jax.Array
    layer: int
    col0: int
    n: int


def _ada_blocks(job, tn):
    return job.n // tn if job.col0 % tn == 0 and job.n % tn == 0 else 0


def _ada_specs(job, tn, step_of):
    rows, d = job.c_pad.shape
    first, nb = job.col0 // tn, job.n // tn

    def col(*ids):
        return first + jnp.minimum(step_of(*ids), nb - 1)

    in_specs = [
        pl.BlockSpec((rows, d), lambda *ids: (0, 0)),
        pl.BlockSpec((None, d, tn), lambda *ids: (job.layer, 0, col(*ids))),
        pl.BlockSpec((None, 1, tn), lambda *ids: (job.layer, 0, col(*ids))),
    ]
    out_spec = pl.BlockSpec((rows, tn), lambda *ids: (0, col(*ids) - first))
    return in_specs, out_spec, jax.ShapeDtypeStruct((rows, job.n), F32)


def _ada_kernel(c_ref, w_ref, b_ref, o_ref):
    c = c_ref[...]
    sc = (c * jax.nn.sigmoid(c)).astype(BF16)
    o_ref[...] = jnp.dot(sc, w_ref[...].astype(BF16), preferred_element_type=F32) + b_ref[...]


def _ada(job):
    tn = _tile(math.gcd(job.col0, job.n) if job.col0 else job.n, 512)
    in_specs, out_spec, out_shape = _ada_specs(job, tn, lambda j: j)
    return pl.pallas_call(
        _ada_kernel, grid=(job.n // tn,), in_specs=in_specs, out_specs=out_spec,
        out_shape=out_shape, compiler_params=_params("arbitrary"), name="ada_matvec",
    )(job.c_pad, job.w, job.b)


def _norm_mod_kernel(x_ref, g_ref, sh_ref, sc_ref, o_ref):
    x = x_ref[...]
    ms = jnp.mean(x * x, axis=-1, keepdims=True)
    y = x * lax.rsqrt(ms + EPS) * g_ref[...]
    o_ref[...] = (y * (1.0 + sc_ref[0]) + sh_ref[0]).astype(o_ref.dtype)


def _norm_mod(x2, gain, mod3, n_mod, shift_j, scale_j, seq, out_dtype):
    m, d = x2.shape
    tm = _tile(seq, 512)
    per_b = seq // tm
    return pl.pallas_call(
        _norm_mod_kernel,
        grid=(m // tm,),
        in_specs=[
            pl.BlockSpec((tm, d), lambda i: (i, 0)),
            pl.BlockSpec((1, d), lambda i: (0, 0)),
            pl.BlockSpec((1, 1, d), lambda i: ((i // per_b) * n_mod + shift_j, 0, 0)),
            pl.BlockSpec((1, 1, d), lambda i: ((i // per_b) * n_mod + scale_j, 0, 0)),
        ],
        out_specs=pl.BlockSpec((tm, d), lambda i: (i, 0)),
        out_shape=jax.ShapeDtypeStruct((m, d), out_dtype),
        compiler_params=_params("arbitrary"),
        name="norm_mod",
    )(x2, gain.reshape(1, d), mod3, mod3)


def _ffn_in_kernel(h_ref, wg_ref, wu_ref, o_ref):
    h = h_ref[...]
    g = jnp.dot(h, wg_ref[...], preferred_element_type=F32)
    u = jnp.dot(h, wu_ref[...], preferred_element_type=F32)
    o_ref[...] = (g * _sigmoid(g) * u).astype(o_ref.dtype)


def _ffn_in_resume_kernel(h_ref, wg_ref, wu_ref, a0_ref, o_ref):
    @pl.when(pl.program_id(0) == 0)
    def _():
        o_ref[...] = a0_ref[...]

    @pl.when(pl.program_id(0) > 0)
    def _():
        _ffn_in_kernel(h_ref, wg_ref, wu_ref, o_ref)


def _ffn_in(h, wg, wu, bm, bn, casts, a0=None, jobs=()):
    m, d = h.shape
    fp = wg.shape[1]
    nj = fp // bn
    resume = a0 is not None
    first = 1 if resume and m > bm else 0
    in_specs = [
        pl.BlockSpec((bm, d), lambda i, j: (jnp.maximum(i, first), 0)),
        pl.BlockSpec((d, bn), lambda i, j: (0, j)),
        pl.BlockSpec((d, bn), lambda i, j: (0, j)),
    ]
    if resume:
        in_specs.append(pl.BlockSpec((bm, bn), lambda i, j: (0, jnp.where(i == 0, j, nj - 1))))
    return _hosted_call(
        _ffn_in_resume_kernel if resume else _ffn_in_kernel,
        (m // bm, nj), in_specs,
        pl.BlockSpec((bm, bn), lambda i, j: (i, j)),
        jax.ShapeDtypeStruct((m, fp), BF16),
        (h, wg, wu) + ((a0,) if resume else ()), "ffn_in", casts=casts, jobs=jobs)


def _ffn_in_first_kernel(h_ref, wg32_ref, wu32_ref, o_ref, wg_ref, wu_ref, *, n_valid):
    keep = pl.program_id(0) < n_valid
    wg = jnp.where(keep, wg32_ref[...], 0.0).astype(wg_ref.dtype)
    wu = jnp.where(keep, wu32_ref[...], 0.0).astype(wu_ref.dtype)
    wg_ref[...] = wg
    wu_ref[...] = wu
    h = h_ref[...]
    g = jnp.dot(h, wg, preferred_element_type=F32)
    u = jnp.dot(h, wu, preferred_element_type=F32)
    o_ref[...] = (g * _sigmoid(g) * u).astype(o_ref.dtype)


def _ffn_in_first(h, w_in, layer, bm):
    m, d = h.shape
    f = w_in.shape[2] // 2
    fp = _ffn_width(f)
    bn = _tile(math.gcd(f, fp), 256)
    nv = f // bn
    return pl.pallas_call(
        functools.partial(_ffn_in_first_kernel, n_valid=nv),
        grid=(fp // bn,),
        in_specs=[
            pl.BlockSpec((bm, d), lambda j: (0, 0)),
            pl.BlockSpec((None, d, bn), lambda j: (layer, 0, jnp.minimum(j, nv - 1))),
            pl.BlockSpec((None, d, bn), lambda j: (layer, 0, nv + jnp.minimum(j, nv - 1))),
        ],
        out_specs=[
            pl.BlockSpec((bm, bn), lambda j: (0, j)),
            pl.BlockSpec((d, bn), lambda j: (0, j)),
            pl.BlockSpec((d, bn), lambda j: (0, j)),
        ],
        out_shape=[jax.ShapeDtypeStruct((bm, fp), BF16),
                   jax.ShapeDtypeStruct((d, fp), BF16), jax.ShapeDtypeStruct((d, fp), BF16)],
        compiler_params=_params("arbitrary"),
        name="ffn_in_first",
    )(h, w_in, w_in)


def _res_matmul_kernel(a_ref, w_ref, res_ref, gate_ref, o_ref, acc_ref, *, coef, nk):
    k = pl.program_id(2)

    def partial_product():
        return jnp.dot(a_ref[...], w_ref[...], preferred_element_type=F32)

    def finish(total):
        o_ref[...] = res_ref[...] + (coef * gate_ref[0]) * total

    if nk == 1:
        finish(partial_product())
        return

    @pl.when(k == 0)
    def _():
        acc_ref[...] = partial_product()

    @pl.when((k > 0) & (k < nk - 1))
    def _():
        acc_ref[...] += partial_product()

    @pl.when(k == nk - 1)
    def _():
        finish(acc_ref[...] + partial_product())


def _res_matmul(a, w, res, mod3, gate_j, coef, seq, bm, bn, bk, name, casts=()):
    m, kdim = a.shape
    n = w.shape[1]
    per_b = seq // bm
    return _hosted_call(
        functools.partial(_res_matmul_kernel, coef=coef, nk=kdim // bk),
        (m // bm, n // bn, kdim // bk),
        [
            pl.BlockSpec((bm, bk), lambda i, j, k: (i, k)),
            pl.BlockSpec((bk, bn), lambda i, j, k: (k, j)),
            pl.BlockSpec((bm, bn), lambda i, j, k: (i, j)),
            pl.BlockSpec((1, 1, bn), lambda i, j, k: ((i // per_b) * N_MOD + gate_j, 0, j)),
        ],
        pl.BlockSpec((bm, bn), lambda i, j, k: (i, j)),
        jax.ShapeDtypeStruct((m, n), F32),
        (a, w, res, mod3), name, scratch=[pltpu.VMEM((bm, bn), F32)], casts=casts)


QSCALE = HEAD_DIM ** -0.5 * LOG2E


def _proj_call(body, h, wt, row0, n, bm, bn, extra_in, extra_specs, name, casts):
    m, d = h.shape
    b0 = row0 // bn
    return _hosted_call(
        body,
        (m // bm, n // bn),
        [
            pl.BlockSpec((bm, d), lambda i, j: (i, 0)),
            pl.BlockSpec((bn, d), lambda i, j: (b0 + j, 0)),
        ] + extra_specs,
        pl.BlockSpec((bm, bn), lambda i, j: (i, j)),
        jax.ShapeDtypeStruct((m, n), BF16),
        (h, wt, *extra_in), name, casts=casts)


def _block_scale(n_scaled):
    return jnp.where(pl.program_id(1) < n_scaled, QSCALE, 1.0).astype(F32)


def _proj_plain_kernel(h_ref, wt_ref, o_ref, *, n_scaled):
    acc = _nt_dot(h_ref[...], wt_ref[...])
    o_ref[...] = (acc * _block_scale(n_scaled)).astype(o_ref.dtype)


def _proj_rope_kernel(h_ref, wt_ref, cos_ref, sin_ref, o_ref, *, n_scaled):
    acc = _nt_dot(h_ref[...], wt_ref[...])
    cosf = cos_ref[...] * _block_scale(n_scaled)
    sinf = sin_ref[...] * _block_scale(n_scaled)
    lane = lax.broadcasted_iota(jnp.int32, cosf.shape, 1)
    for c in range(acc.shape[1] // HEAD_DIM):
        cols = slice(c * HEAD_DIM, (c + 1) * HEAD_DIM)
        t = acc[:, cols]
        partner = jnp.where(lane < ROPE_HALF,
                            pltpu.roll(t, HEAD_DIM - ROPE_HALF, 1),
                            pltpu.roll(t, ROPE_HALF, 1))
        o_ref[:, cols] = (t * cosf + partner * sinf).astype(o_ref.dtype)


def _proj_gate_kernel(h_ref, wt_ref, b_ref, o_ref):
    acc = _nt_dot(h_ref[...], wt_ref[...])
    o_ref[...] = _sigmoid(acc + b_ref[...]).astype(o_ref.dtype)


def _proj_plain(h, wt, row0, n, n_scaled, bm, bn, casts):
    return _proj_call(functools.partial(_proj_plain_kernel, n_scaled=n_scaled),
                      h, wt, row0, n, bm, bn, [], [], "proj_plain", casts)


def _proj_rope(h, wt, row0, n, n_scaled, cosf, sinf, bm, bn, casts):
    table = pl.BlockSpec((bm, HEAD_DIM), lambda i, j: (i, 0))
    return _proj_call(functools.partial(_proj_rope_kernel, n_scaled=n_scaled),
                      h, wt, row0, n, bm, bn, [cosf, sinf], [table, table], "proj_rope", casts)


def _proj_gate(h, wt, row0, n, bias, bm, bn, casts):
    return _proj_call(_proj_gate_kernel, h, wt, row0, n, bm, bn, [bias.reshape(1, n)],
                      [pl.BlockSpec((1, bn), lambda i, j: (0, j))], "proj_gate", casts)


def _vt_proj_kernel(wt_ref, h_ref, o_ref, *, tk):
    r = _nt_dot(wt_ref[...], h_ref[...])
    for t in range(o_ref.shape[0]):
        o_ref[t] = r[:, t * tk:(t + 1) * tk].astype(o_ref.dtype)


def _vt_proj(h, wt, row0, n, bm, bn, tk, casts=()):
    m, d = h.shape
    b0 = row0 // bn
    return _hosted_call(
        functools.partial(_vt_proj_kernel, tk=tk),
        (m // bm, n // bn),
        [
            pl.BlockSpec((bn, d), lambda i, j: (b0 + j, 0)),
            pl.BlockSpec((bm, d), lambda i, j: (i, 0)),
        ],
        pl.BlockSpec((bm // tk, bn, tk), lambda i, j: (i, j, 0)),
        jax.ShapeDtypeStruct((m // tk, n, tk), BF16),
        (wt, h), "proj_vt", casts=casts)


def _forget_kernel(h_ref, w_ref, b_ref, e_ref, o_ref, carry_ref):
    @pl.when(pl.program_id(1) == 0)
    def _():
        carry_ref[...] = jnp.zeros_like(carry_ref)

    ts = FORGET_SUBTILE if h_ref.shape[0] % FORGET_SUBTILE == 0 else h_ref.shape[0]
    row = lax.broadcasted_iota(jnp.int32, (ts, ts), 0)
    col = lax.broadcasted_iota(jnp.int32, (ts, ts), 1)
    tri = (row >= col).astype(F32)
    subs = [slice(s * ts, (s + 1) * ts) for s in range(h_ref.shape[0] // ts)]
    ffs = [_nt_dot(h_ref[rows, :], w_ref[...]) + b_ref[...] for rows in subs]
    logfs = [jnp.minimum(ff, 0.0) - jnp.log1p(jnp.exp(-jnp.abs(ff))) for ff in ffs]
    sums = [jnp.dot(tri, logf, preferred_element_type=F32, precision=lax.Precision.HIGHEST)
            for logf in logfs]
    carry = carry_ref[...]
    stacked = []
    for within in sums:
        cum = within + carry
        carry = cum[ts - 1:ts, :]
        rest = cum * (-LOG2E)
        pieces = []
        for _ in range(N_SPLIT):
            p = rest.astype(BF16)
            pieces.append(p)
            rest = rest - p.astype(F32)
        stacked.append(jnp.concatenate(pieces, axis=1))
    carry_ref[...] = carry
    for rows, pieces in zip(subs, stacked):
        spread = jnp.dot(pieces, e_ref[...], preferred_element_type=F32)
        o_ref[rows, :] = spread.astype(o_ref.dtype)


def _forget(h, w_ff_t, b_ff, n_heads, batch, seq):
    m, d = h.shape
    tm = _tile(seq, 2 * FORGET_SUBTILE)
    per_b = seq // tm
    route = np.zeros((N_SPLIT * LANE, n_heads * HEAD_DIM), np.float32)
    for p in range(N_SPLIT):
        for hh in range(n_heads):
            route[p * LANE + hh, hh * HEAD_DIM + p] = 1.0
    return pl.pallas_call(
        _forget_kernel,
        grid=(batch, per_b),
        in_specs=[
            pl.BlockSpec((tm, d), lambda b, i: (b * per_b + i, 0)),
            pl.BlockSpec((LANE, d), lambda b, i: (0, 0)),
            pl.BlockSpec((1, LANE), lambda b, i: (0, 0)),
            pl.BlockSpec(route.shape, lambda b, i: (0, 0)),
        ],
        out_specs=pl.BlockSpec((tm, n_heads * HEAD_DIM), lambda b, i: (b * per_b + i, 0)),
        out_shape=jax.ShapeDtypeStruct((m, n_heads * HEAD_DIM), BF16),
        scratch_shapes=[pltpu.VMEM((1, LANE), F32)],
        compiler_params=_params("arbitrary", "arbitrary"),
        name="forget_cumsum",
    )(h, w_ff_t, b_ff, jnp.asarray(route, BF16))


def _online_softmax_step(st, st_max, vt, m, l, acc):
    m_new = jnp.maximum(m, st_max)
    alpha = jnp.exp2(m - m_new)
    p = jnp.exp2(st - m_new)
    l_new = alpha * l + jnp.sum(p, axis=0, keepdims=True)
    acc_new = alpha * acc + jnp.dot(vt, p.astype(vt.dtype), preferred_element_type=F32)
    return m_new, l_new, acc_new


def _flash_scratch(n_chains, tq, dv):
    return ([pltpu.VMEM((2, tq, tq), F32) for _ in range(n_chains)]
            + [pltpu.VMEM((2 * n_chains, 1, tq), F32)]
            + [pltpu.VMEM((n_chains, 1, tq), F32), pltpu.VMEM((n_chains, 1, tq), F32),
               pltpu.VMEM((n_chains, dv, tq), F32)])


def _flash_loop(qi, n_chains, qk_fn, vt_fn, visible_fn, st_refs, mx_ref, m_ref, l_ref, acc_ref):
    def produce(kj, slot, diagonal):
        for c in range(n_chains):
            st = qk_fn(c, kj)
            if diagonal:
                st = jnp.where(visible_fn(), st, MASKED)
            st_refs[c][slot] = st
            mx_ref[2 * c + slot] = jnp.max(st, axis=0, keepdims=True)

    def consume(c, kj, slot):
        m, l, acc = _online_softmax_step(st_refs[c][slot], mx_ref[2 * c + slot], vt_fn(c, kj),
                                         m_ref[c], l_ref[c], acc_ref[c])
        m_ref[c] = m
        l_ref[c] = l
        acc_ref[c] = acc

    def step(kj, slot, next_is_diagonal):
        for c in range(n_chains):
            st = qk_fn(c, kj + 1)
            if next_is_diagonal:
                st = jnp.where(visible_fn(), st, MASKED)
            st_refs[c][1 - slot] = st
            mx_ref[2 * c + 1 - slot] = jnp.max(st, axis=0, keepdims=True)
            consume(c, kj, slot)

    def run_if(cond, fn):
        def body(_, carry):
            fn()
            return carry
        lax.fori_loop(0, cond.astype(jnp.int32), body, 0)

    tq = m_ref.shape[-1]
    m_ref[...] = jnp.full(m_ref.shape, MASKED, F32)
    l_ref[...] = jnp.zeros(l_ref.shape, F32)
    acc_ref[...] = jnp.zeros(acc_ref.shape, F32)

    run_if(qi == 0, lambda: produce(0, 0, True))
    run_if(qi > 0, lambda: produce(0, 0, False))

    n_pairs = jnp.maximum(qi - 1, 0) // 2
    kj0 = 2 * n_pairs

    def pair(i, carry):
        step(2 * i, 0, False)
        step(2 * i + 1, 1, False)
        return carry

    lax.fori_loop(0, n_pairs, pair, 0)
    two_left = (qi > 0) & (qi - kj0 == 2)
    one_left = (qi > 0) & (qi - kj0 == 1)
    run_if(two_left, lambda: step(kj0, 0, False))
    run_if(two_left, lambda: step(kj0 + 1, 1, True))
    run_if(one_left, lambda: step(kj0, 0, True))
    run_if(one_left, lambda: [consume(c, qi, 1) for c in range(n_chains)])
    run_if(jnp.logical_not(one_left), lambda: [consume(c, qi, 0) for c in range(n_chains)])


def _fox_kernel(q_ref, k_ref, kx_ref, vt_ref, o_ref, *scratch, tq, nh):
    qi = pl.program_id(2)
    st_refs, (mx_ref, m_ref, l_ref, acc_ref) = scratch[:nh], scratch[nh:]
    lane = lax.broadcasted_iota(jnp.int32, (tq, HEAD_DIM), 1)
    ones = (lane < N_SPLIT).astype(q_ref.dtype)

    def head(a):
        return slice(a * HEAD_DIM, (a + 1) * HEAD_DIM)

    def qk(a, kj):
        ks = pl.multiple_of(kj * tq, tq)
        q_aug = jnp.concatenate([q_ref[:, head(a)], ones], axis=1)
        k_aug = jnp.concatenate([k_ref[pl.ds(ks, tq), head(a)], kx_ref[pl.ds(ks, tq), head(a)]], axis=1)
        return _nt_dot(k_aug, q_aug)

    def visible():
        key = lax.broadcasted_iota(jnp.int32, (tq, tq), 0)
        qry = lax.broadcasted_iota(jnp.int32, (tq, tq), 1)
        return key <= qry

    _flash_loop(qi, nh, qk, lambda a, kj: vt_ref[kj, head(a), :], visible,
                st_refs, mx_ref, m_ref, l_ref, acc_ref)
    for a in range(nh):
        o_ref[:, head(a)] = (acc_ref[a] / l_ref[a]).T.astype(o_ref.dtype)


def _fox(qk, kx, vt, batch, seq, n_heads, tq, nh):
    m = qk.shape[0]
    nq = seq // tq
    gw = nh * HEAD_DIM
    ng = n_heads // nh
    return pl.pallas_call(
        functools.partial(_fox_kernel, tq=tq, nh=nh),
        grid=(batch, ng, nq),
        in_specs=[
            pl.BlockSpec((tq, gw), lambda b, g, i: (b * nq + i, g)),
            pl.BlockSpec((seq, gw), lambda b, g, i: (b, ng + g)),
            pl.BlockSpec((seq, gw), lambda b, g, i: (b, g)),
            pl.BlockSpec((nq, gw, tq), lambda b, g, i: (b, g, 0)),
        ],
        out_specs=pl.BlockSpec((tq, gw), lambda b, g, i: (b * nq + i, g)),
        out_shape=jax.ShapeDtypeStruct((m, n_heads * HEAD_DIM), BF16),
        scratch_shapes=_flash_scratch(nh, tq, HEAD_DIM),
        compiler_params=_params("arbitrary", "arbitrary", "arbitrary"),
        name="fox_attention",
    )(qk, qk, kx, vt)


def _diff_kernel(q_ref, k_ref, vt_ref, lam_ref, g_ref, o_ref, *scratch, tq, nh, lambda_init):
    qi = pl.program_id(2)
    n_maps = 2 * nh
    st_refs, (mx_ref, m_ref, l_ref, acc_ref) = scratch[:n_maps], scratch[n_maps:]
    hw = 2 * HEAD_DIM

    def cols(c):
        return slice(c * HEAD_DIM, (c + 1) * HEAD_DIM)

    def qk(c, kj):
        ks = pl.multiple_of(kj * tq, tq)
        return _nt_dot(k_ref[pl.ds(ks, tq), cols(c)], q_ref[:, cols(c)])

    def visible():
        key = lax.broadcasted_iota(jnp.int32, (tq, tq), 0) // CHUNK
        qry = lax.broadcasted_iota(jnp.int32, (tq, tq), 1) // CHUNK
        return key <= qry

    _flash_loop(qi, n_maps, qk, lambda c, kj: vt_ref[kj, (c // 2) * hw:(c // 2 + 1) * hw, :],
                visible, st_refs, mx_ref, m_ref, l_ref, acc_ref)

    lp = lam_ref[...]
    lam = (jnp.exp(jnp.sum(lp[0:1] * lp[1:2], axis=-1, keepdims=True))
           - jnp.exp(jnp.sum(lp[2:3] * lp[3:4], axis=-1, keepdims=True)) + lambda_init)
    for a in range(nh):
        y = acc_ref[2 * a] / l_ref[2 * a] - lam * (acc_ref[2 * a + 1] / l_ref[2 * a + 1])
        ms = jnp.mean(y * y, axis=0, keepdims=True)
        y = (y * lax.rsqrt(ms + EPS)).T * g_ref[...]
        o_ref[:, a * hw:(a + 1) * hw] = (y * (1.0 - lambda_init)).astype(o_ref.dtype)


def _diff(qk, vt, lam_p, subln, batch, seq, n_heads, tq, nh, lambda_init):
    m = qk.shape[0]
    nq = seq // tq
    hw = 2 * HEAD_DIM
    gw = nh * hw
    ng = n_heads // nh
    return pl.pallas_call(
        functools.partial(_diff_kernel, tq=tq, nh=nh, lambda_init=lambda_init),
        grid=(batch, ng, nq),
        in_specs=[
            pl.BlockSpec((tq, gw), lambda b, g, i: (b * nq + i, g)),
            pl.BlockSpec((seq, gw), lambda b, g, i: (b, ng + g)),
            pl.BlockSpec((nq, gw, tq), lambda b, g, i: (b, g, 0)),
            pl.BlockSpec((4, HEAD_DIM), lambda b, g, i: (0, 0)),
            pl.BlockSpec((1, hw), lambda b, g, i: (0, 0)),
        ],
        out_specs=pl.BlockSpec((tq, gw), lambda b, g, i: (b * nq + i, g)),
        out_shape=jax.ShapeDtypeStruct((m, n_heads * hw), BF16),
        scratch_shapes=_flash_scratch(2 * nh, tq, hw),
        compiler_params=_params("arbitrary", "arbitrary", "arbitrary"),
        name="diff_attention",
    )(qk, qk, vt, lam_p, subln.reshape(1, hw))


def _merge_kernel(ya_ref, yb_ref, wa_ref, wb_ref, sa_ref, sb_ref, o_ref):
    pa = jnp.dot(ya_ref[...], wa_ref[...], preferred_element_type=F32)
    pb = jnp.dot(yb_ref[...], wb_ref[...], preferred_element_type=F32)
    o_ref[...] = (sa_ref[...].astype(F32) * pa + sb_ref[...].astype(F32) * pb).astype(o_ref.dtype)


def _merge(ya, yb, wa, wb, gates, bm, bn):
    m = ya.shape[0]
    d = wa.shape[1]
    nb = d // bn
    return pl.pallas_call(
        _merge_kernel,
        grid=(m // bm, nb),
        in_specs=[
            pl.BlockSpec((bm, ya.shape[1]), lambda i, j: (i, 0)),
            pl.BlockSpec((bm, yb.shape[1]), lambda i, j: (i, 0)),
            pl.BlockSpec((wa.shape[0], bn), lambda i, j: (0, j)),
            pl.BlockSpec((wb.shape[0], bn), lambda i, j: (0, j)),
            pl.BlockSpec((bm, bn), lambda i, j: (i, j)),
            pl.BlockSpec((bm, bn), lambda i, j: (i, nb + j)),
        ],
        out_specs=pl.BlockSpec((bm, bn), lambda i, j: (i, j)),
        out_shape=jax.ShapeDtypeStruct((m, d), BF16),
        compiler_params=_params("arbitrary", "arbitrary"),
        name="gated_merge",
    )(ya, yb, wa, wb, gates, gates)


def _ffn_width(f):
    return _round_up(f, 1024) if f >= 1024 else f


def _ffn_in_cast(w_in, layer):
    _, d, f2 = w_in.shape
    f = f2 // 2
    return _StripCast(w_in, layer, 0, d, d, ((0, f, _ffn_width(f)), (f, f, _ffn_width(f))))


def _ffn(x2, gain, norm_mods, w_in, pre_cast, w_out, layer, gate_mod3_of, gate_j, seq, next_casts,
         jobs=()):
    m, d = x2.shape
    f = w_out.shape[1]
    fp = _ffn_width(f)
    bm = _tile(seq, 1024)
    h = _norm_mod(x2, gain, *norm_mods, seq, BF16)
    wo_cast = _StripCast(w_out, layer, 0, f, fp, ((0, d, d),))
    if pre_cast is None:
        a0, wg, wu = _ffn_in_first(h, w_in, layer, bm)
    else:
        a0, (wg, wu) = None, pre_cast
    a, ((wo,), *job_outs) = _ffn_in(h, wg, wu, bm, _tile(fp, 512), (wo_cast,), a0, jobs)
    bk = fp // 4 if (fp // 4) % LANE == 0 else fp
    x2, cast_outs = _res_matmul(a, wo, x2, gate_mod3_of(job_outs), gate_j, 0.5, seq, bm,
                                _tile(d, 1024), bk, "ffn_out", casts=next_casts)
    return x2, cast_outs, job_outs


def kernel(x, c, positions, ada_w, ada_b, norm_ffn1, ffn1_w_in, ffn1_w_out, norm_mix, w_in, b_forget, b_gate, diff_lambda, diff_subln, w_o_fox, w_o_diff, w_out, norm_ffn2, ffn2_w_in, ffn2_w_out, final_ada_w, final_ada_b, norm_final):
    batch, seq, d = x.shape
    depth = ada_w.shape[0]
    m = batch * seq
    n_fox = b_forget.shape[1]
    fox_w = n_fox * HEAD_DIM
    diff_w = w_o_diff.shape[1]
    n_diff = diff_w // (2 * HEAD_DIM)
    o_ff = 3 * fox_w
    o_dq = o_ff + n_fox
    n_rest = 3 * diff_w + 2 * d
    assert w_in.shape[2] == o_dq + n_rest
    assert n_fox <= LANE and seq % CHUNK == 0

    c_pad = jnp.pad(c, ((0, -batch % 8), (0, 0)))
    x2 = x.reshape(m, d)
    w_in_t = jnp.swapaxes(w_in, 1, 2)

    inv_freq = ROPE_THETA ** (-jnp.arange(0, ROPE_DIM, 2, dtype=F32) / ROPE_DIM)
    rest = jnp.zeros((HEAD_DIM - ROPE_DIM,), F32)
    inv_lane = jnp.concatenate([inv_freq, inv_freq, rest])
    sign_lane = jnp.concatenate([-jnp.ones((ROPE_HALF,), F32), jnp.ones((ROPE_HALF,), F32), rest])
    ang = positions.astype(F32).reshape(m, 1) * inv_lane
    cosf = jnp.cos(ang)
    sinf = jnp.sin(ang) * sign_lane

    bm = _tile(seq, 1024)
    tq = _tile(seq, 512)
    bn_p = _tile(math.gcd(fox_w, diff_w), 1024)
    for l in range(depth):
        lambda_init = 0.8 - 0.6 * math.exp(-0.3 * l)
        ada_b3 = ada_b.reshape(depth, 1, N_MOD * d)
        mod_head = _ada(_AdaJob(c_pad, ada_w, ada_b3, l, 0, 2 * d))
        mod_rest = _AdaJob(c_pad, ada_w, ada_b3, l, 2 * d, (N_MOD - 2) * d)

        def full_mod3(job_outs):
            mod = jnp.concatenate([mod_head, job_outs[0][0]], axis=1)[:batch]
            return mod.reshape(batch * N_MOD, 1, d)

        mixer_casts = (_StripCast(w_in_t, l, 0, o_ff, o_ff, ((0, d, d),)),
                       _StripCast(w_in_t, l, o_dq, n_rest, n_rest, ((0, d, d),)))
        x2, ((w_front,), (w_rest,)), job_outs = _ffn(
            x2, norm_ffn1[l], (mod_head[:batch].reshape(batch * 2, 1, d), 2, 0, 1), ffn1_w_in, None,
            ffn1_w_out, l, full_mod3, 2, seq, mixer_casts, jobs=(mod_rest,))
        mod3 = full_mod3(job_outs)

        h = _norm_mod(x2, norm_mix[l], mod3, N_MOD, 3, 4, seq, BF16)
        w_ff_t = jnp.pad(w_in_t[l, o_ff:o_dq, :].astype(BF16), ((0, LANE - n_fox), (0, 0)))
        b_ff = jnp.pad(b_forget[l].astype(F32), (0, LANE - n_fox)).reshape(1, LANE)

        qk_f, ((wa,),) = _proj_plain(h, w_front, 0, 2 * fox_w, fox_w // bn_p, bm, bn_p,
                                     (_whole(w_o_fox, l),))
        qk_d, ((wb,),) = _proj_rope(h, w_rest, 0, 2 * diff_w, diff_w // bn_p, cosf, sinf, bm, bn_p,
                                    (_whole(w_o_diff, l),))
        gates, ((wg, wu),) = _proj_gate(h, w_rest, 3 * diff_w, 2 * d, b_gate[l], bm, bn_p,
                                        (_ffn_in_cast(ffn2_w_in, l),))
        vt_f, ((wo,),) = _vt_proj(h, w_front, 2 * fox_w, fox_w, bm, bn_p, tq, (_whole(w_out, l),))
        vt_d, _ = _vt_proj(h, w_rest, 2 * diff_w, diff_w, bm, bn_p, tq)
        kx = _forget(h, w_ff_t, b_ff, n_fox, batch, seq)

        ya = _fox(qk_f, kx, vt_f, batch, seq, n_fox, tq, math.gcd(n_fox, FOX_HEADS_PER_STEP))
        yb = _diff(qk_d, vt_d, diff_lambda[l].astype(F32), diff_subln[l], batch, seq, n_diff,
                   tq, math.gcd(n_diff, DIFF_HEADS_PER_STEP), lambda_init)
        merged = _merge(ya, yb, wa, wb, gates, bm, _tile(d, 1024))
        x2, _ = _res_matmul(merged, wo, x2, mod3, 5, 1.0, seq, bm, _tile(d, 1024), d, "mixer_out")

        last = l == depth - 1
        final_job = _AdaJob(c_pad, final_ada_w[None], final_ada_b.reshape(1, 1, 2 * d), 0, 0, 2 * d)
        x2, _, final_outs = _ffn(x2, norm_ffn2[l], (mod3, N_MOD, 6, 7), ffn2_w_in, (wg, wu), ffn2_w_out,
                                 l, lambda _, mod3=mod3: mod3, 8, seq, (),
                                 jobs=(final_job,) if last else ())

    fmod = final_outs[0][0][:batch]
    out = _norm_mod(x2, norm_final, fmod.reshape(batch * 2, 1, d), 2, 0, 1, seq, F32)
    return out.reshape(batch, seq, d)
```

```python
import functools
import math
from typing import NamedTuple

import numpy as np
import jax
import jax.numpy as jnp
from jax import lax
from jax.experimental import pallas as pl
from jax.experimental.pallas import tpu as pltpu

HEAD_DIM = 128
CHUNK = 64
ROPE_THETA = 500000.0
ROPE_DIM = HEAD_DIM // 4
ROPE_HALF = ROPE_DIM // 2
EPS = 1e-6
N_MOD = 9
LOG2E = 1.4426950408889634
MASKED = -1e30
LANE = 128
VMEM_LIMIT = 56 * 1024 * 1024
BF16_ROWS = 16
CAST_BLOCK_ELEMS = 2 * 1024 * 1024
HOSTED_STRIP_ELEMS = 1024 * 1024
ADA_HOSTED_COLS = 256
N_SPLIT = 3
FORGET_SUBTILE = 512
FOX_HEADS_PER_STEP = 4
FOX_QUERY_SPLIT = 2
DIFF_HEADS_PER_STEP = 2

F32 = jnp.float32
BF16 = jnp.bfloat16


def _params(*sem):
    return pltpu.CompilerParams(dimension_semantics=sem, vmem_limit_bytes=VMEM_LIMIT)


def _tile(n, pref):
    t = min(n, pref)
    while n % t:
        t //= 2
    return t


def _round_up(n, m):
    return (n + m - 1) // m * m


def _sigmoid(x):
    return 0.5 * jnp.tanh(0.5 * x) + 0.5


def _nt_dot(a, b):
    return lax.dot_general(a, b, (((1,), (1,)), ((), ())), preferred_element_type=F32)


class _StripCast(NamedTuple):
    src: jax.Array
    layer: int
    row0: int
    rows: int
    rows_out: int
    pieces: tuple


def _whole(src, layer):
    _, r, c = src.shape
    return _StripCast(src, layer, 0, r, r, ((0, c, c),))


def _strip_rows(cast, lo, hi):
    g = math.gcd(math.gcd(cast.row0, cast.rows), cast.rows_out)
    for rs in range(_round_up(max(lo, 1), BF16_ROWS), min(hi, g) + 1, BF16_ROWS):
        if g % rs == 0:
            return rs
    return None


def _linear_step(grid, ids):
    t = ids[0]
    for g, i in zip(grid[1:], ids[1:]):
        t = t * g + i
    return t


def _strip_specs(cast, rs, grid):
    first, n_valid, n_strips = cast.row0 // rs, cast.rows // rs, cast.rows_out // rs
    cols = cast.src.shape[2]
    src_spec = pl.BlockSpec(
        (None, rs, cols),
        lambda *ids: (cast.layer, first + jnp.minimum(_linear_step(grid, ids), n_valid - 1), 0))
    out_specs = [pl.BlockSpec((rs, nco), lambda *ids: (jnp.minimum(_linear_step(grid, ids), n_strips - 1), 0))
                 for _, _, nco in cast.pieces]
    out_shapes = [jax.ShapeDtypeStruct((cast.rows_out, nco), BF16) for _, _, nco in cast.pieces]
    return src_spec, out_specs, out_shapes


def _strip_cast_body(cast, rs, grid, src_ref, dst_refs):
    t = _linear_step(grid, [pl.program_id(a) for a in range(len(grid))])
    strip = src_ref[...]
    for (c0, n, nco), dst in zip(cast.pieces, dst_refs):
        piece = strip[:, c0:c0 + n].astype(dst.dtype)
        if cast.rows < cast.rows_out:
            piece = jnp.where(t < cast.rows // rs, piece, jnp.zeros_like(piece))
        if nco > n:
            dst[:, :n] = piece
            dst[:, n:] = jnp.zeros((rs, nco - n), dst.dtype)
        else:
            dst[...] = piece


def _run_cast(cast):
    rs = _strip_rows(cast, 1, max(CAST_BLOCK_ELEMS // cast.src.shape[2], BF16_ROWS))
    rs = rs or _strip_rows(cast, 1, cast.rows_out)
    if rs is None:
        w = cast.src[cast.layer, cast.row0:cast.row0 + cast.rows]
        return tuple(jnp.pad(w[:, c0:c0 + n].astype(BF16), ((0, cast.rows_out - cast.rows), (0, nco - n)))
                     for c0, n, nco in cast.pieces)
    grid = (cast.rows_out // rs,)
    src_spec, out_specs, out_shapes = _strip_specs(cast, rs, grid)

    def body(src_ref, *dst_refs):
        _strip_cast_body(cast, rs, grid, src_ref, dst_refs)

    return tuple(pl.pallas_call(
        body, grid=grid, in_specs=[src_spec], out_specs=out_specs, out_shape=out_shapes,
        compiler_params=_params("arbitrary"), name="weight_cast",
    )(cast.src))


def _hosted_call(body, grid, in_specs, out_spec, out_shape, inputs, name, scratch=(), casts=(),
                 jobs=()):
    steps = math.prod(grid)
    hosted, strip_rows = [], []
    for cast in casts:
        rs = _strip_rows(cast, pl.cdiv(cast.rows_out, steps), cast.rows_out)
        if rs is not None and rs * cast.src.shape[2] <= HOSTED_STRIP_ELEMS:
            hosted.append(cast)
            strip_rows.append(rs)
    specs = [_strip_specs(cast, rs, grid) for cast, rs in zip(hosted, strip_rows)]
    riders = [job for job in jobs if _ada_blocks(job, ADA_HOSTED_COLS) in range(1, steps + 1)]
    rider_specs = [_ada_specs(job, ADA_HOSTED_COLS, lambda *ids: _linear_step(grid, ids))
                   for job in riders]
    n_in, n_src, n_job = len(in_specs), len(hosted), len(riders)

    def kernel_body(*refs):
        ins, src_refs = refs[:n_in], refs[n_in:n_in + n_src]
        job_ins = refs[n_in + n_src:n_in + n_src + 3 * n_job]
        o_ref = refs[n_in + n_src + 3 * n_job]
        pos = n_in + n_src + 3 * n_job + 1
        dst_groups = []
        for cast in hosted:
            dst_groups.append(refs[pos:pos + len(cast.pieces)])
            pos += len(cast.pieces)
        job_outs = refs[pos:pos + n_job]
        pos += n_job
        body(*ins, o_ref, *refs[pos:])
        for cast, rs, src_ref, dst_refs in zip(hosted, strip_rows, src_refs, dst_groups):
            _strip_cast_body(cast, rs, grid, src_ref, dst_refs)
        t = _linear_step(grid, [pl.program_id(a) for a in range(len(grid))])
        for k, job in enumerate(riders):
            @pl.when(t < _ada_blocks(job, ADA_HOSTED_COLS))
            def _(k=k):
                _ada_kernel(*job_ins[3 * k:3 * k + 3], job_outs[k])

    outs = pl.pallas_call(
        kernel_body, grid=grid,
        in_specs=list(in_specs) + [s[0] for s in specs] + [i for r in rider_specs for i in r[0]],
        out_specs=[out_spec] + [o for s in specs for o in s[1]] + [r[1] for r in rider_specs],
        out_shape=[out_shape] + [o for s in specs for o in s[2]] + [r[2] for r in rider_specs],
        scratch_shapes=list(scratch),
        compiler_params=_params(*(("arbitrary",) * len(grid))), name=name,
    )(*inputs, *[cast.src for cast in hosted], *[a for job in riders for a in (job.c_pad, job.w, job.b)])

    results, pos = [], 1
    for cast in casts:
        if any(cast is hc for hc in hosted):
            results.append(tuple(outs[pos:pos + len(cast.pieces)]))
            pos += len(cast.pieces)
        else:
            results.append(_run_cast(cast))
    pos = 1 + sum(len(cast.pieces) for cast in hosted)
    for job in jobs:
        if any(job is r for r in riders):
            results.append((outs[pos],))
            pos += 1
        else:
            results.append((_ada(job),))
    return outs[0], tuple(results)


class _AdaJob(NamedTuple):
    c_pad: jax.Array
    w: jax.Array
    b: jax.Array
    layer: int
    col0: int
    n: int


def _ada_blocks(job, tn):
    return job.n // tn if job.col0 % tn == 0 and job.n % tn == 0 else 0


def _ada_specs(job, tn, step_of):
    rows, d = job.c_pad.shape
    first, nb = job.col0 // tn, job.n // tn

    def col(*ids):
        return first + jnp.minimum(step_of(*ids), nb - 1)

    in_specs = [
        pl.BlockSpec((rows, d), lambda *ids: (0, 0)),
        pl.BlockSpec((None, d, tn), lambda *ids: (job.layer, 0, col(*ids))),
        pl.BlockSpec((None, 1, tn), lambda *ids: (job.layer, 0, col(*ids))),
    ]
    out_spec = pl.BlockSpec((rows, tn), lambda *ids: (0, col(*ids) - first))
    return in_specs, out_spec, jax.ShapeDtypeStruct((rows, job.n), F32)


def _ada_kernel(c_ref, w_ref, b_ref, o_ref):
    c = c_ref[...]
    sc = (c * jax.nn.sigmoid(c)).astype(BF16)
    o_ref[...] = jnp.dot(sc, w_ref[...].astype(BF16), preferred_element_type=F32) + b_ref[...]


def _ada(job):
    tn = _tile(math.gcd(job.col0, job.n) if job.col0 else job.n, 512)
    in_specs, out_spec, out_shape = _ada_specs(job, tn, lambda j: j)
    return pl.pallas_call(
        _ada_kernel, grid=(job.n // tn,), in_specs=in_specs, out_specs=out_spec,
        out_shape=out_shape, compiler_params=_params("arbitrary"), name="ada_matvec",
    )(job.c_pad, job.w, job.b)


def _norm_mod_kernel(x_ref, g_ref, sh_ref, sc_ref, o_ref):
    x = x_ref[...]
    ms = jnp.mean(x * x, axis=-1, keepdims=True)
    y = x * lax.rsqrt(ms + EPS) * g_ref[...]
    o_ref[...] = (y * (1.0 + sc_ref[0]) + sh_ref[0]).astype(o_ref.dtype)


def _norm_mod(x2, gain, mod3, n_mod, shift_j, scale_j, seq, out_dtype):
    m, d = x2.shape
    tm = _tile(seq, 512)
    per_b = seq // tm
    return pl.pallas_call(
        _norm_mod_kernel,
        grid=(m // tm,),
        in_specs=[
            pl.BlockSpec((tm, d), lambda i: (i, 0)),
            pl.BlockSpec((1, d), lambda i: (0, 0)),
            pl.BlockSpec((1, 1, d), lambda i: ((i // per_b) * n_mod + shift_j, 0, 0)),
            pl.BlockSpec((1, 1, d), lambda i: ((i // per_b) * n_mod + scale_j, 0, 0)),
        ],
        out_specs=pl.BlockSpec((tm, d), lambda i: (i, 0)),
        out_shape=jax.ShapeDtypeStruct((m, d), out_dtype),
        compiler_params=_params("arbitrary"),
        name="norm_mod",
    )(x2, gain.reshape(1, d), mod3, mod3)


def _ffn_in_kernel(h_ref, wg_ref, wu_ref, o_ref):
    h = h_ref[...]
    g = jnp.dot(h, wg_ref[...], preferred_element_type=F32)
    u = jnp.dot(h, wu_ref[...], preferred_element_type=F32)
    o_ref[...] = (g * _sigmoid(g) * u).astype(o_ref.dtype)


def _ffn_in_resume_kernel(h_ref, wg_ref, wu_ref, a0_ref, o_ref):
    @pl.when(pl.program_id(0) == 0)
    def _():
        o_ref[...] = a0_ref[...]

    @pl.when(pl.program_id(0) > 0)
    def _():
        _ffn_in_kernel(h_ref, wg_ref, wu_ref, o_ref)


def _ffn_in(h, wg, wu, bm, bn, casts, a0=None, jobs=()):
    m, d = h.shape
    fp = wg.shape[1]
    nj = fp // bn
    resume = a0 is not None
    first = 1 if resume and m > bm else 0
    in_specs = [
        pl.BlockSpec((bm, d), lambda i, j: (jnp.maximum(i, first), 0)),
        pl.BlockSpec((d, bn), lambda i, j: (0, j)),
        pl.BlockSpec((d, bn), lambda i, j: (0, j)),
    ]
    if resume:
        in_specs.append(pl.BlockSpec((bm, bn), lambda i, j: (0, jnp.where(i == 0, j, nj - 1))))
    return _hosted_call(
        _ffn_in_resume_kernel if resume else _ffn_in_kernel,
        (m // bm, nj), in_specs,
        pl.BlockSpec((bm, bn), lambda i, j: (i, j)),
        jax.ShapeDtypeStruct((m, fp), BF16),
        (h, wg, wu) + ((a0,) if resume else ()), "ffn_in", casts=casts, jobs=jobs)


def _ffn_in_first_kernel(h_ref, wg32_ref, wu32_ref, o_ref, wg_ref, wu_ref, *, n_valid):
    keep = pl.program_id(0) < n_valid
    wg = jnp.where(keep, wg32_ref[...], 0.0).astype(wg_ref.dtype)
    wu = jnp.where(keep, wu32_ref[...], 0.0).astype(wu_ref.dtype)
    wg_ref[...] = wg
    wu_ref[...] = wu
    h = h_ref[...]
    g = jnp.dot(h, wg, preferred_element_type=F32)
    u = jnp.dot(h, wu, preferred_element_type=F32)
    o_ref[...] = (g * _sigmoid(g) * u).astype(o_ref.dtype)


def _ffn_in_first(h, w_in, layer, bm):
    m, d = h.shape
    f = w_in.shape[2] // 2
    fp = _ffn_width(f)
    bn = _tile(math.gcd(f, fp), 256)
    nv = f // bn
    return pl.pallas_call(
        functools.partial(_ffn_in_first_kernel, n_valid=nv),
        grid=(fp // bn,),
        in_specs=[
            pl.BlockSpec((bm, d), lambda j: (0, 0)),
            pl.BlockSpec((None, d, bn), lambda j: (layer, 0, jnp.minimum(j, nv - 1))),
            pl.BlockSpec((None, d, bn), lambda j: (layer, 0, nv + jnp.minimum(j, nv - 1))),
        ],
        out_specs=[
            pl.BlockSpec((bm, bn), lambda j: (0, j)),
            pl.BlockSpec((d, bn), lambda j: (0, j)),
            pl.BlockSpec((d, bn), lambda j: (0, j)),
        ],
        out_shape=[jax.ShapeDtypeStruct((bm, fp), BF16),
                   jax.ShapeDtypeStruct((d, fp), BF16), jax.ShapeDtypeStruct((d, fp), BF16)],
        compiler_params=_params("arbitrary"),
        name="ffn_in_first",
    )(h, w_in, w_in)


def _res_matmul_kernel(a_ref, w_ref, res_ref, gate_ref, o_ref, acc_ref, *, coef, nk):
    k = pl.program_id(2)

    def partial_product():
        return jnp.dot(a_ref[...], w_ref[...], preferred_element_type=F32)

    def finish(total):
        o_ref[...] = res_ref[...] + (coef * gate_ref[0]) * total

    if nk == 1:
        finish(partial_product())
        return

    @pl.when(k == 0)
    def _():
        acc_ref[...] = partial_product()

    @pl.when((k > 0) & (k < nk - 1))
    def _():
        acc_ref[...] += partial_product()

    @pl.when(k == nk - 1)
    def _():
        finish(acc_ref[...] + partial_product())


def _res_matmul(a, w, res, mod3, gate_j, coef, seq, bm, bn, bk, name, casts=()):
    m, kdim = a.shape
    n = w.shape[1]
    per_b = seq // bm
    return _hosted_call(
        functools.partial(_res_matmul_kernel, coef=coef, nk=kdim // bk),
        (m // bm, n // bn, kdim // bk),
        [
            pl.BlockSpec((bm, bk), lambda i, j, k: (i, k)),
            pl.BlockSpec((bk, bn), lambda i, j, k: (k, j)),
            pl.BlockSpec((bm, bn), lambda i, j, k: (i, j)),
            pl.BlockSpec((1, 1, bn), lambda i, j, k: ((i // per_b) * N_MOD + gate_j, 0, j)),
        ],
        pl.BlockSpec((bm, bn), lambda i, j, k: (i, j)),
        jax.ShapeDtypeStruct((m, n), F32),
        (a, w, res, mod3), name, scratch=[pltpu.VMEM((bm, bn), F32)], casts=casts)


QSCALE = HEAD_DIM ** -0.5 * LOG2E


def _proj_call(body, h, wt, row0, n, bm, bn, extra_in, extra_specs, name, casts):
    m, d = h.shape
    b0 = row0 // bn
    return _hosted_call(
        body,
        (m // bm, n // bn),
        [
            pl.BlockSpec((bm, d), lambda i, j: (i, 0)),
            pl.BlockSpec((bn, d), lambda i, j: (b0 + j, 0)),
        ] + extra_specs,
        pl.BlockSpec((bm, bn), lambda i, j: (i, j)),
        jax.ShapeDtypeStruct((m, n), BF16),
        (h, wt, *extra_in), name, casts=casts)


def _block_scale(n_scaled):
    return jnp.where(pl.program_id(1) < n_scaled, QSCALE, 1.0).astype(F32)


def _proj_plain_kernel(h_ref, wt_ref, o_ref, *, n_scaled):
    acc = _nt_dot(h_ref[...], wt_ref[...])
    o_ref[...] = (acc * _block_scale(n_scaled)).astype(o_ref.dtype)


def _proj_rope_kernel(h_ref, wt_ref, cos_ref, sin_ref, o_ref, *, n_scaled):
    acc = _nt_dot(h_ref[...], wt_ref[...])
    cosf = cos_ref[...] * _block_scale(n_scaled)
    sinf = sin_ref[...] * _block_scale(n_scaled)
    lane = lax.broadcasted_iota(jnp.int32, cosf.shape, 1)
    for c in range(acc.shape[1] // HEAD_DIM):
        cols = slice(c * HEAD_DIM, (c + 1) * HEAD_DIM)
        t = acc[:, cols]
        partner = jnp.where(lane < ROPE_HALF,
                            pltpu.roll(t, HEAD_DIM - ROPE_HALF, 1),
                            pltpu.roll(t, ROPE_HALF, 1))
        o_ref[:, cols] = (t * cosf + partner * sinf).astype(o_ref.dtype)


def _proj_gate_kernel(h_ref, wt_ref, b_ref, o_ref):
    acc = _nt_dot(h_ref[...], wt_ref[...])
    o_ref[...] = _sigmoid(acc + b_ref[...]).astype(o_ref.dtype)


def _proj_plain(h, wt, row0, n, n_scaled, bm, bn, casts):
    return _proj_call(functools.partial(_proj_plain_kernel, n_scaled=n_scaled),
                      h, wt, row0, n, bm, bn, [], [], "proj_plain", casts)


def _proj_rope(h, wt, row0, n, n_scaled, cosf, sinf, bm, bn, casts):
    table = pl.BlockSpec((bm, HEAD_DIM), lambda i, j: (i, 0))
    return _proj_call(functools.partial(_proj_rope_kernel, n_scaled=n_scaled),
                      h, wt, row0, n, bm, bn, [cosf, sinf], [table, table], "proj_rope", casts)


def _proj_gate(h, wt, row0, n, bias, bm, bn, casts):
    return _proj_call(_proj_gate_kernel, h, wt, row0, n, bm, bn, [bias.reshape(1, n)],
                      [pl.BlockSpec((1, bn), lambda i, j: (0, j))], "proj_gate", casts)


def _vt_proj_kernel(wt_ref, h_ref, o_ref, *, tk):
    r = _nt_dot(wt_ref[...], h_ref[...])
    for t in range(o_ref.shape[0]):
        o_ref[t] = r[:, t * tk:(t + 1) * tk].astype(o_ref.dtype)


def _vt_proj(h, wt, row0, n, bm, bn, tk, casts=()):
    m, d = h.shape
    b0 = row0 // bn
    return _hosted_call(
        functools.partial(_vt_proj_kernel, tk=tk),
        (m // bm, n // bn),
        [
            pl.BlockSpec((bn, d), lambda i, j: (b0 + j, 0)),
            pl.BlockSpec((bm, d), lambda i, j: (i, 0)),
        ],
        pl.BlockSpec((bm // tk, bn, tk), lambda i, j: (i, j, 0)),
        jax.ShapeDtypeStruct((m // tk, n, tk), BF16),
        (wt, h), "proj_vt", casts=casts)


def _forget_kernel(h_ref, w_ref, b_ref, e_ref, o_ref, carry_ref):
    @pl.when(pl.program_id(1) == 0)
    def _():
        carry_ref[...] = jnp.zeros_like(carry_ref)

    ts = FORGET_SUBTILE if h_ref.shape[0] % FORGET_SUBTILE == 0 else h_ref.shape[0]
    row = lax.broadcasted_iota(jnp.int32, (ts, ts), 0)
    col = lax.broadcasted_iota(jnp.int32, (ts, ts), 1)
    tri = (row >= col).astype(F32)
    subs = [slice(s * ts, (s + 1) * ts) for s in range(h_ref.shape[0] // ts)]
    ffs = [_nt_dot(h_ref[rows, :], w_ref[...]) + b_ref[...] for rows in subs]
    logfs = [jnp.minimum(ff, 0.0) - jnp.log1p(jnp.exp(-jnp.abs(ff))) for ff in ffs]
    sums = [jnp.dot(tri, logf, preferred_element_type=F32, precision=lax.Precision.HIGHEST)
            for logf in logfs]
    carry = carry_ref[...]
    stacked = []
    for within in sums:
        cum = within + carry
        carry = cum[ts - 1:ts, :]
        rest = cum * (-LOG2E)
        pieces = []
        for _ in range(N_SPLIT):
            p = rest.astype(BF16)
            pieces.append(p)
            rest = rest - p.astype(F32)
        stacked.append(jnp.concatenate(pieces, axis=1))
    carry_ref[...] = carry
    for rows, pieces in zip(subs, stacked):
        spread = jnp.dot(pieces, e_ref[...], preferred_element_type=F32)
        o_ref[rows, :] = spread.astype(o_ref.dtype)


def _forget(h, w_ff_t, b_ff, n_heads, batch, seq):
    m, d = h.shape
    tm = _tile(seq, 2 * FORGET_SUBTILE)
    per_b = seq // tm
    route = np.zeros((N_SPLIT * LANE, n_heads * HEAD_DIM), np.float32)
    for p in range(N_SPLIT):
        for hh in range(n_heads):
            route[p * LANE + hh, hh * HEAD_DIM + p] = 1.0
    return pl.pallas_call(
        _forget_kernel,
        grid=(batch, per_b),
        in_specs=[
            pl.BlockSpec((tm, d), lambda b, i: (b * per_b + i, 0)),
            pl.BlockSpec((LANE, d), lambda b, i: (0, 0)),
            pl.BlockSpec((1, LANE), lambda b, i: (0, 0)),
            pl.BlockSpec(route.shape, lambda b, i: (0, 0)),
        ],
        out_specs=pl.BlockSpec((tm, n_heads * HEAD_DIM), lambda b, i: (b * per_b + i, 0)),
        out_shape=jax.ShapeDtypeStruct((m, n_heads * HEAD_DIM), BF16),
        scratch_shapes=[pltpu.VMEM((1, LANE), F32)],
        compiler_params=_params("arbitrary", "arbitrary"),
        name="forget_cumsum",
    )(h, w_ff_t, b_ff, jnp.asarray(route, BF16))


def _online_softmax_step(st, st_max, vt, m, l, acc):
    m_new = jnp.maximum(m, st_max)
    alpha = jnp.exp2(m - m_new)
    p = jnp.exp2(st - m_new)
    l_new = alpha * l + jnp.sum(p, axis=0, keepdims=True)
    acc_new = alpha * acc + jnp.dot(vt, p.astype(vt.dtype), preferred_element_type=F32)
    return m_new, l_new, acc_new


def _flash_scratch(n_chains, tk, wq, dv):
    return ([pltpu.VMEM((2, tk, wq), F32) for _ in range(n_chains)]
            + [pltpu.VMEM((2 * n_chains, 1, wq), F32)]
            + [pltpu.VMEM((n_chains, 1, wq), F32), pltpu.VMEM((n_chains, 1, wq), F32),
               pltpu.VMEM((n_chains, dv, wq), F32)])


def _flash_loop(qi, n_chains, qk_fn, vt_fn, visible_fn, st_refs, mx_ref, m_ref, l_ref, acc_ref):
    def produce(kj, slot, diagonal):
        for c in range(n_chains):
            st = qk_fn(c, kj)
            if diagonal:
                st = jnp.where(visible_fn(c), st, MASKED)
            st_refs[c][slot] = st
            mx_ref[2 * c + slot] = jnp.max(st, axis=0, keepdims=True)

    def consume(c, kj, slot):
        m, l, acc = _online_softmax_step(st_refs[c][slot], mx_ref[2 * c + slot], vt_fn(c, kj),
                                         m_ref[c], l_ref[c], acc_ref[c])
        m_ref[c] = m
        l_ref[c] = l
        acc_ref[c] = acc

    def step(kj, slot, next_is_diagonal):
        for c in range(n_chains):
            st = qk_fn(c, kj + 1)
            if next_is_diagonal:
                st = jnp.where(visible_fn(c), st, MASKED)
            st_refs[c][1 - slot] = st
            mx_ref[2 * c + 1 - slot] = jnp.max(st, axis=0, keepdims=True)
            consume(c, kj, slot)

    def run_if(cond, fn):
        def body(_, carry):
            fn()
            return carry
        lax.fori_loop(0, cond.astype(jnp.int32), body, 0)

    tq = m_ref.shape[-1]
    m_ref[...] = jnp.full(m_ref.shape, MASKED, F32)
    l_ref[...] = jnp.zeros(l_ref.shape, F32)
    acc_ref[...] = jnp.zeros(acc_ref.shape, F32)

    run_if(qi == 0, lambda: produce(0, 0, True))
    run_if(qi > 0, lambda: produce(0, 0, False))

    n_pairs = jnp.maximum(qi - 1, 0) // 2
    kj0 = 2 * n_pairs

    def pair(i, carry):
        step(2 * i, 0, False)
        step(2 * i + 1, 1, False)
        return carry

    lax.fori_loop(0, n_pairs, pair, 0)
    two_left = (qi > 0) & (qi - kj0 == 2)
    one_left = (qi > 0) & (qi - kj0 == 1)
    run_if(two_left, lambda: step(kj0, 0, False))
    run_if(two_left, lambda: step(kj0 + 1, 1, True))
    run_if(one_left, lambda: step(kj0, 0, True))
    run_if(one_left, lambda: [consume(c, qi, 1) for c in range(n_chains)])
    run_if(jnp.logical_not(one_left), lambda: [consume(c, qi, 0) for c in range(n_chains)])


def _fox_kernel(q_ref, k_ref, kx_ref, vt_ref, o_ref, *scratch, tq, nh):
    qi = pl.program_id(2)
    n_chains = FOX_QUERY_SPLIT * nh
    wq = tq // FOX_QUERY_SPLIT
    st_refs, (mx_ref, m_ref, l_ref, acc_ref) = scratch[:n_chains], scratch[n_chains:]
    lane = lax.broadcasted_iota(jnp.int32, (wq, HEAD_DIM), 1)
    ones = (lane < N_SPLIT).astype(q_ref.dtype)

    def head(c):
        a = c // FOX_QUERY_SPLIT
        return slice(a * HEAD_DIM, (a + 1) * HEAD_DIM)

    def rows(c):
        part = c % FOX_QUERY_SPLIT
        return slice(part * wq, (part + 1) * wq)

    def qk(c, kj):
        ks = pl.multiple_of(kj * tq, tq)
        q_aug = jnp.concatenate([q_ref[rows(c), head(c)], ones], axis=1)
        k_aug = jnp.concatenate([k_ref[pl.ds(ks, tq), head(c)], kx_ref[pl.ds(ks, tq), head(c)]], axis=1)
        return _nt_dot(k_aug, q_aug)

    def visible(c):
        key = lax.broadcasted_iota(jnp.int32, (tq, wq), 0)
        qry = lax.broadcasted_iota(jnp.int32, (tq, wq), 1) + (c % FOX_QUERY_SPLIT) * wq
        return key <= qry

    _flash_loop(qi, n_chains, qk, lambda c, kj: vt_ref[kj, head(c), :], visible,
                st_refs, mx_ref, m_ref, l_ref, acc_ref)
    for c in range(n_chains):
        o_ref[rows(c), head(c)] = (acc_ref[c] / l_ref[c]).T.astype(o_ref.dtype)


def _fox(qk, kx, vt, batch, seq, n_heads, tq, nh):
    m = qk.shape[0]
    nq = seq // tq
    gw = nh * HEAD_DIM
    ng = n_heads // nh
    return pl.pallas_call(
        functools.partial(_fox_kernel, tq=tq, nh=nh),
        grid=(batch, ng, nq),
        in_specs=[
            pl.BlockSpec((tq, gw), lambda b, g, i: (b * nq + i, g)),
            pl.BlockSpec((seq, gw), lambda b, g, i: (b, ng + g)),
            pl.BlockSpec((seq, gw), lambda b, g, i: (b, g)),
            pl.BlockSpec((nq, gw, tq), lambda b, g, i: (b, g, 0)),
        ],
        out_specs=pl.BlockSpec((tq, gw), lambda b, g, i: (b * nq + i, g)),
        out_shape=jax.ShapeDtypeStruct((m, n_heads * HEAD_DIM), BF16),
        scratch_shapes=_flash_scratch(FOX_QUERY_SPLIT * nh, tq, tq // FOX_QUERY_SPLIT, HEAD_DIM),
        compiler_params=_params("arbitrary", "arbitrary", "arbitrary"),
        name="fox_attention",
    )(qk, qk, kx, vt)


def _diff_kernel(q_ref, k_ref, vt_ref, lam_ref, g_ref, o_ref, *scratch, tq, nh, lambda_init):
    qi = pl.program_id(2)
    n_maps = 2 * nh
    st_refs, (mx_ref, m_ref, l_ref, acc_ref) = scratch[:n_maps], scratch[n_maps:]
    hw = 2 * HEAD_DIM

    def cols(c):
        return slice(c * HEAD_DIM, (c + 1) * HEAD_DIM)

    def qk(c, kj):
        ks = pl.multiple_of(kj * tq, tq)
        return _nt_dot(k_ref[pl.ds(ks, tq), cols(c)], q_ref[:, cols(c)])

    def visible(_):
        key = lax.broadcasted_iota(jnp.int32, (tq, tq), 0) // CHUNK
        qry = lax.broadcasted_iota(jnp.int32, (tq, tq), 1) // CHUNK
        return key <= qry

    _flash_loop(qi, n_maps, qk, lambda c, kj: vt_ref[kj, (c // 2) * hw:(c // 2 + 1) * hw, :],
                visible, st_refs, mx_ref, m_ref, l_ref, acc_ref)

    lp = lam_ref[...]
    lam = (jnp.exp(jnp.sum(lp[0:1] * lp[1:2], axis=-1, keepdims=True))
           - jnp.exp(jnp.sum(lp[2:3] * lp[3:4], axis=-1, keepdims=True)) + lambda_init)
    for a in range(nh):
        y = acc_ref[2 * a] / l_ref[2 * a] - lam * (acc_ref[2 * a + 1] / l_ref[2 * a + 1])
        ms = jnp.mean(y * y, axis=0, keepdims=True)
        y = (y * lax.rsqrt(ms + EPS)).T * g_ref[...]
        o_ref[:, a * hw:(a + 1) * hw] = (y * (1.0 - lambda_init)).astype(o_ref.dtype)


def _diff(qk, vt, lam_p, subln, batch, seq, n_heads, tq, nh, lambda_init):
    m = qk.shape[0]
    nq = seq // tq
    hw = 2 * HEAD_DIM
    gw = nh * hw
    ng = n_heads // nh
    return pl.pallas_call(
        functools.partial(_diff_kernel, tq=tq, nh=nh, lambda_init=lambda_init),
        grid=(batch, ng, nq),
        in_specs=[
            pl.BlockSpec((tq, gw), lambda b, g, i: (b * nq + i, g)),
            pl.BlockSpec((seq, gw), lambda b, g, i: (b, ng + g)),
            pl.BlockSpec((nq, gw, tq), lambda b, g, i: (b, g, 0)),
            pl.BlockSpec((4, HEAD_DIM), lambda b, g, i: (0, 0)),
            pl.BlockSpec((1, hw), lambda b, g, i: (0, 0)),
        ],
        out_specs=pl.BlockSpec((tq, gw), lambda b, g, i: (b * nq + i, g)),
        out_shape=jax.ShapeDtypeStruct((m, n_heads * hw), BF16),
        scratch_shapes=_flash_scratch(2 * nh, tq, tq, hw),
        compiler_params=_params("arbitrary", "arbitrary", "arbitrary"),
        name="diff_attention",
    )(qk, qk, vt, lam_p, subln.reshape(1, hw))


def _merge_kernel(ya_ref, yb_ref, wa_ref, wb_ref, sa_ref, sb_ref, o_ref):
    pa = jnp.dot(ya_ref[...], wa_ref[...], preferred_element_type=F32)
    pb = jnp.dot(yb_ref[...], wb_ref[...], preferred_element_type=F32)
    o_ref[...] = (sa_ref[...].astype(F32) * pa + sb_ref[...].astype(F32) * pb).astype(o_ref.dtype)


def _merge(ya, yb, wa, wb, gates, bm, bn):
    m = ya.shape[0]
    d = wa.shape[1]
    nb = d // bn
    return pl.pallas_call(
        _merge_kernel,
        grid=(m // bm, nb),
        in_specs=[
            pl.BlockSpec((bm, ya.shape[1]), lambda i, j: (i, 0)),
            pl.BlockSpec((bm, yb.shape[1]), lambda i, j: (i, 0)),
            pl.BlockSpec((wa.shape[0], bn), lambda i, j: (0, j)),
            pl.BlockSpec((wb.shape[0], bn), lambda i, j: (0, j)),
            pl.BlockSpec((bm, bn), lambda i, j: (i, j)),
            pl.BlockSpec((bm, bn), lambda i, j: (i, nb + j)),
        ],
        out_specs=pl.BlockSpec((bm, bn), lambda i, j: (i, j)),
        out_shape=jax.ShapeDtypeStruct((m, d), BF16),
        compiler_params=_params("arbitrary", "arbitrary"),
        name="gated_merge",
    )(ya, yb, wa, wb, gates, gates)


def _ffn_width(f):
    return _round_up(f, 1024) if f >= 1024 else f


def _ffn_in_cast(w_in, layer):
    _, d, f2 = w_in.shape
    f = f2 // 2
    return _StripCast(w_in, layer, 0, d, d, ((0, f, _ffn_width(f)), (f, f, _ffn_width(f))))


def _ffn(x2, gain, norm_mods, w_in, pre_cast, w_out, layer, gate_mod3_of, gate_j, seq, next_casts,
         jobs=()):
    m, d = x2.shape
    f = w_out.shape[1]
    fp = _ffn_width(f)
    bm = _tile(seq, 1024)
    h = _norm_mod(x2, gain, *norm_mods, seq, BF16)
    wo_cast = _StripCast(w_out, layer, 0, f, fp, ((0, d, d),))
    if pre_cast is None:
        a0, wg, wu = _ffn_in_first(h, w_in, layer, bm)
    else:
        a0, (wg, wu) = None, pre_cast
    a, ((wo,), *job_outs) = _ffn_in(h, wg, wu, bm, _tile(fp, 512), (wo_cast,), a0, jobs)
    bk = fp // 4 if (fp // 4) % LANE == 0 else fp
    x2, cast_outs = _res_matmul(a, wo, x2, gate_mod3_of(job_outs), gate_j, 0.5, seq, bm,
                                _tile(d, 1024), bk, "ffn_out", casts=next_casts)
    return x2, cast_outs, job_outs


def kernel(x, c, positions, ada_w, ada_b, norm_ffn1, ffn1_w_in, ffn1_w_out, norm_mix, w_in, b_forget, b_gate, diff_lambda, diff_subln, w_o_fox, w_o_diff, w_out, norm_ffn2, ffn2_w_in, ffn2_w_out, final_ada_w, final_ada_b, norm_final):
    batch, seq, d = x.shape
    depth = ada_w.shape[0]
    m = batch * seq
    n_fox = b_forget.shape[1]
    fox_w = n_fox * HEAD_DIM
    diff_w = w_o_diff.shape[1]
    n_diff = diff_w // (2 * HEAD_DIM)
    o_ff = 3 * fox_w
    o_dq = o_ff + n_fox
    n_rest = 3 * diff_w + 2 * d
    assert w_in.shape[2] == o_dq + n_rest
    assert n_fox <= LANE and seq % CHUNK == 0

    c_pad = jnp.pad(c, ((0, -batch % 8), (0, 0)))
    x2 = x.reshape(m, d)
    w_in_t = jnp.swapaxes(w_in, 1, 2)

    inv_freq = ROPE_THETA ** (-jnp.arange(0, ROPE_DIM, 2, dtype=F32) / ROPE_DIM)
    ang = positions.astype(F32).reshape(m, 1) * inv_freq
    cos, sin = jnp.cos(ang), jnp.sin(ang)
    cosf = jnp.concatenate([cos, cos, jnp.ones((m, HEAD_DIM - ROPE_DIM), F32)], axis=-1)
    sinf = jnp.concatenate([-sin, sin, jnp.zeros((m, HEAD_DIM - ROPE_DIM), F32)], axis=-1)

    bm = _tile(seq, 1024)
    tq = _tile(seq, 512)
    bn_p = _tile(math.gcd(fox_w, diff_w), 1024)
    for l in range(depth):
        lambda_init = 0.8 - 0.6 * math.exp(-0.3 * l)
        ada_b3 = ada_b.reshape(depth, 1, N_MOD * d)
        mod_head = _ada(_AdaJob(c_pad, ada_w, ada_b3, l, 0, 2 * d))
        mod_rest = _AdaJob(c_pad, ada_w, ada_b3, l, 2 * d, (N_MOD - 2) * d)

        def full_mod3(job_outs):
            mod = jnp.concatenate([mod_head, job_outs[0][0]], axis=1)[:batch]
            return mod.reshape(batch * N_MOD, 1, d)

        mixer_casts = (_StripCast(w_in_t, l, 0, o_ff, o_ff, ((0, d, d),)),
                       _StripCast(w_in_t, l, o_dq, n_rest, n_rest, ((0, d, d),)))
        x2, ((w_front,), (w_rest,)), job_outs = _ffn(
            x2, norm_ffn1[l], (mod_head[:batch].reshape(batch * 2, 1, d), 2, 0, 1), ffn1_w_in, None,
            ffn1_w_out, l, full_mod3, 2, seq, mixer_casts, jobs=(mod_rest,))
        mod3 = full_mod3(job_outs)

        h = _norm_mod(x2, norm_mix[l], mod3, N_MOD, 3, 4, seq, BF16)
        w_ff_t = jnp.pad(w_in_t[l, o_ff:o_dq, :].astype(BF16), ((0, LANE - n_fox), (0, 0)))
        b_ff = jnp.pad(b_forget[l].astype(F32), (0, LANE - n_fox)).reshape(1, LANE)

        qk_f, ((wa,),) = _proj_plain(h, w_front, 0, 2 * fox_w, fox_w // bn_p, bm, bn_p,
                                     (_whole(w_o_fox, l),))
        qk_d, ((wb,),) = _proj_rope(h, w_rest, 0, 2 * diff_w, diff_w // bn_p, cosf, sinf, bm, bn_p,
                                    (_whole(w_o_diff, l),))
        gates, ((wg, wu),) = _proj_gate(h, w_rest, 3 * diff_w, 2 * d, b_gate[l], bm, bn_p,
                                        (_ffn_in_cast(ffn2_w_in, l),))
        vt_f, ((wo,),) = _vt_proj(h, w_front, 2 * fox_w, fox_w, bm, bn_p, tq, (_whole(w_out, l),))
        vt_d, _ = _vt_proj(h, w_rest, 2 * diff_w, diff_w, bm, bn_p, tq)
        kx = _forget(h, w_ff_t, b_ff, n_fox, batch, seq)

        ya = _fox(qk_f, kx, vt_f, batch, seq, n_fox, tq, math.gcd(n_fox, FOX_HEADS_PER_STEP))
        yb = _diff(qk_d, vt_d, diff_lambda[l].astype(F32), diff_subln[l], batch, seq, n_diff,
                   tq, math.gcd(n_diff, DIFF_HEADS_PER_STEP), lambda_init)
        merged = _merge(ya, yb, wa, wb, gates, bm, _tile(d, 1024))
        x2, _ = _res_matmul(merged, wo, x2, mod3, 5, 1.0, seq, bm, _tile(d, 1024), d, "mixer_out")

        last = l == depth - 1
        final_job = _AdaJob(c_pad, final_ada_w[None], final_ada_b.reshape(1, 1, 2 * d), 0, 0, 2 * d)
        x2, _, final_outs = _ffn(x2, norm_ffn2[l], (mod3, N_MOD, 6, 7), ffn2_w_in, (wg, wu), ffn2_w_out,
                                 l, lambda _, mod3=mod3: mod3, 8, seq, (),
                                 jobs=(final_job,) if last else ())

    fmod = final_outs[0][0][:batch]
    out = _norm_mod(x2, norm_final, fmod.reshape(batch * 2, 1, d), 2, 0, 1, seq, F32)
    return out.reshape(batch, seq, d)
```
